```python
import math
import jax
import jax.numpy as jnp
from jax import lax
import numpy as np

D_MODEL = 1024
BATCH = 4
SEQ = 8192
DEPTH = 2

P_DIM = 256
N_BRANCH = 4
BRANCH_W = 256
DA_HEADS = 4
DA_QK = 32
DA_V = 2 * DA_QK
DSA_HEADS = 4
DSA_HD = 64
IDX_HEADS = 4
IDX_HD = 32
TOPK_MAX = 256
RET_HEADS = 4
RET_QK = 64
RET_V = 64
RET_CHUNK = 128
GLA_HEADS = 4
GLA_QK = 32
GLA_V = 64
GLA_RANK = 16
GLA_GATE_TEMP = 16.0
GLA_CHUNK = 64
D_FF = 2816
CONV_W = 3
Q_BLOCK = 128
N_SOFTMAX_HEADS = DA_HEADS + DSA_HEADS
LN_EPS = 1e-5
NEG_INF = -1e30
DEEPNORM_ALPHA = (2.0 * DEPTH) ** 0.25
DEEPNORM_BETA = (8.0 * DEPTH) ** -0.25

IN_SIZES = (
    DA_HEADS * 2 * DA_QK, DA_HEADS * 2 * DA_QK, DA_HEADS * DA_V,
    DSA_HEADS * DSA_HD, DSA_HEADS * DSA_HD, DSA_HEADS * DSA_HD,
    IDX_HEADS * IDX_HD, IDX_HD, IDX_HEADS,
    RET_HEADS * RET_QK, RET_HEADS * RET_QK, RET_HEADS * RET_V, RET_HEADS * RET_V,
    GLA_HEADS * GLA_QK, GLA_HEADS * GLA_QK, GLA_HEADS * GLA_V, GLA_RANK, GLA_HEADS * GLA_V,
    N_BRANCH * D_MODEL,
)
N_IN = sum(IN_SIZES)

kernel_name = "hybrid_gated_diff_dsa_ret_gla_deepnorm"


def _split_columns(z):
    offsets = np.cumsum(IN_SIZES)[:-1].tolist()
    return jnp.split(z, offsets, axis=-1)


def _layer_norm(x, w, b):
    xf = x.astype(jnp.float32)
    mu = jnp.mean(xf, -1, keepdims=True)
    var = jnp.mean(jnp.square(xf - mu), -1, keepdims=True)
    y = (xf - mu) * lax.rsqrt(var + LN_EPS) * w.astype(jnp.float32) + b.astype(jnp.float32)
    return y.astype(x.dtype)


def _head_rms_norm(o, w):
    return o * lax.rsqrt(jnp.mean(o * o, -1, keepdims=True) + LN_EPS) * w


def _head_layer_norm(o, w):
    mu = jnp.mean(o, -1, keepdims=True)
    var = jnp.mean(jnp.square(o - mu), -1, keepdims=True)
    return (o - mu) * lax.rsqrt(var + LN_EPS) * w


def _alibi_slopes():
    i = jnp.arange(1, N_SOFTMAX_HEADS + 1, dtype=jnp.float32)
    return jnp.power(2.0, -(8.0 / N_SOFTMAX_HEADS) * i)


def _to_qblocks(t):
    bn, s = t.shape[:2]
    return jnp.swapaxes(t.reshape((bn, s // Q_BLOCK, Q_BLOCK) + t.shape[2:]), 0, 1)


def _from_qblocks(t):
    t = jnp.swapaxes(t, 0, 1)
    return t.reshape((t.shape[0], t.shape[1] * t.shape[2]) + t.shape[3:])


def _diff_attention(q, k, v, lam, norm_w, lam_init, slopes):
    bn, s = q.shape[:2]
    nb = s // Q_BLOCK
    scale = DA_QK ** -0.5
    kpos = jnp.arange(s)

    def block(args):
        qi, start = args
        qpos = start + jnp.arange(Q_BLOCK)
        dist = (qpos[:, None] - kpos[None, :]).astype(jnp.float32)
        bias = jnp.where(dist >= 0, -slopes[:, None, None] * dist, NEG_INF)
        sc = jnp.einsum('bqhmd,bkhmd->bhmqk', qi, k) * scale + bias[None, :, None]
        pr = jax.nn.softmax(sc, axis=-1)
        a = pr[:, :, 0] - lam * pr[:, :, 1]
        return jnp.einsum('bhqk,bkhe->bqhe', a, v)

    o = lax.map(block, (_to_qblocks(q), jnp.arange(nb) * Q_BLOCK))
    o = _from_qblocks(o)
    o = _head_rms_norm(o, norm_w) * (1.0 - lam_init)
    return o.reshape(bn, s, DA_HEADS * DA_V)


def _dsa_attention(q, k, v, iq, ik, iw, slopes):
    bn, s = q.shape[:2]
    topk = min(TOPK_MAX, s // 4)
    nb = s // Q_BLOCK
    kpos = jnp.arange(s)
    iscale = IDX_HD ** -0.5
    scale = DSA_HD ** -0.5

    def block(args):
        qi, iqi, iwi, start = args
        qpos = start + jnp.arange(Q_BLOCK)
        causal = kpos[None, :] <= qpos[:, None]
        logits = jnp.einsum('bqhd,bkd->bqhk', iqi, ik) * iscale
        score = jnp.einsum('bqh,bqhk->bqk', iwi, jax.nn.relu(logits))
        score = jnp.where(causal[None], score, -jnp.inf)
        _, idx = lax.top_k(score, topk)
        valid = idx <= qpos[None, :, None]
        kg = jax.vmap(lambda kb, ib: kb[ib])(k, idx)
        vg = jax.vmap(lambda vb, ib: vb[ib])(v, idx)
        dist = (qpos[None, :, None] - idx).astype(jnp.float32)
        sc = jnp.einsum('bqhd,bqkhd->bhqk', qi, kg) * scale - slopes[None, :, None, None] * dist[:, None]
        sc = jnp.where(valid[:, None], sc, NEG_INF)
        pr = jax.nn.softmax(sc, axis=-1)
        return jnp.einsum('bhqk,bqkhd->bqhd', pr, vg)

    o = lax.map(block, (_to_qblocks(q), _to_qblocks(iq), _to_qblocks(iw), jnp.arange(nb) * Q_BLOCK))
    return _from_qblocks(o).reshape(bn, s, DSA_HEADS * DSA_HD)


def _retention(q, k, v):
    bn, s, h, dk = q.shape
    dv = v.shape[-1]
    c = RET_CHUNK
    nc = s // c
    log_g = jnp.log1p(-jnp.power(2.0, -5.0 - jnp.arange(h, dtype=jnp.float32)))
    pos = jnp.arange(c, dtype=jnp.float32)
    rel = pos[:, None] - pos[None, :]
    intra = jnp.where(rel >= 0, jnp.exp(log_g[:, None, None] * jnp.maximum(rel, 0.0)), 0.0)
    q_decay = jnp.exp(log_g[:, None] * (pos[None, :] + 1.0))
    k_decay = jnp.exp(log_g[:, None] * (c - 1.0 - pos[None, :]))
    chunk_decay = jnp.exp(log_g * c)
    k = k * dk ** -0.5

    def to_chunks(t):
        return t.reshape(bn, nc, c, h, t.shape[-1]).transpose(1, 0, 3, 2, 4)

    def step(state, inp):
        qc, kc, vc = inp
        att = jnp.einsum('bhid,bhjd->bhij', qc, kc) * intra
        o = (jnp.einsum('bhij,bhje->bhie', att, vc)
             + jnp.einsum('bhid,bhde->bhie', qc, state) * q_decay[..., None])
        state = (state * chunk_decay[:, None, None]
                 + jnp.einsum('bhjd,bhje->bhde', kc * k_decay[..., None], vc))
        return state, o

    state0 = jnp.zeros((bn, h, dk, dv), jnp.float32)
    _, o = lax.scan(step, state0, (to_chunks(q), to_chunks(k), to_chunks(v)))
    return o.transpose(1, 0, 3, 2, 4).reshape(bn, s, h, dv)


def _gla(q, k, v, log_a):
    bn, s, h, dk = q.shape
    dv = v.shape[-1]
    c = GLA_CHUNK
    nc = s // c
    q = q * dk ** -0.5
    causal = jnp.tril(jnp.ones((c, c), dtype=bool))

    def to_chunks(t):
        return t.reshape(bn, nc, c, h, t.shape[-1]).transpose(1, 0, 3, 2, 4)

    def step(state, inp):
        qc, kc, vc, ac = inp
        b = jnp.cumsum(ac, axis=2)
        diff = b[:, :, :, None, :] - b[:, :, None, :, :]
        dec = jnp.where(causal[:, :, None], jnp.exp(jnp.minimum(diff, 0.0)), 0.0)
        att = jnp.einsum('bhid,bhjd,bhijd->bhij', qc, kc, dec)
        o = (jnp.einsum('bhij,bhje->bhie', att, vc)
             + jnp.einsum('bhid,bhde->bhie', qc * jnp.exp(b), state))
        b_last = b[:, :, -1]
        state = (state * jnp.exp(b_last)[..., None]
                 + jnp.einsum('bhjd,bhje->bhde', kc * jnp.exp(b_last[:, :, None, :] - b), vc))
        return state, o

    state0 = jnp.zeros((bn, h, dk, dv), jnp.float32)
    _, o = lax.scan(step, state0, (to_chunks(q), to_chunks(k), to_chunks(v), to_chunks(log_a)))
    return o.transpose(1, 0, 3, 2, 4).reshape(bn, s, h, dv)


def _causal_dwconv(x, w, b):
    s = x.shape[1]
    xp = jnp.pad(x, ((0, 0), (CONV_W - 1, 0), (0, 0)))
    y = b + w[0] * xp[:, 0:s]
    for j in range(1, CONV_W):
        y = y + w[j] * xp[:, j:j + s]
    return y


def setup_inputs(seed: int = 0) -> dict:
    key = jax.random.key(seed)
    ks = jax.random.split(key, 24)
    f32 = jnp.float32

    def nrm(k, shape, scale):
        return jax.random.normal(k, shape, f32) * scale

    beta = DEEPNORM_BETA
    return {
        "x": nrm(ks[0], (BATCH, SEQ, D_MODEL), 1.0),
        "p": nrm(ks[1], (DEPTH, BATCH, SEQ, P_DIM), 1.0),
        "w_in": nrm(ks[2], (DEPTH, D_MODEL, N_IN), D_MODEL ** -0.5),
        "a_lambda": nrm(ks[3], (DEPTH, 4, DA_QK), 0.1),
        "a_norm_w": 1.0 + nrm(ks[4], (DEPTH, DA_V), 0.02),
        "ret_norm_w": 1.0 + nrm(ks[5], (DEPTH, RET_HEADS * RET_V), 0.02),
        "gla_w_a2": nrm(ks[6], (DEPTH, GLA_RANK, GLA_HEADS * GLA_QK), GLA_RANK ** -0.5),
        "gla_b_a": nrm(ks[7], (DEPTH, GLA_HEADS * GLA_QK), 0.02),
        "gla_norm_w": 1.0 + nrm(ks[8], (DEPTH, GLA_V), 0.02),
        "w_branch": nrm(ks[9], (DEPTH, N_BRANCH, BRANCH_W, D_MODEL), BRANCH_W ** -0.5 * beta),
        "w_out": nrm(ks[10], (DEPTH, D_MODEL, D_MODEL), D_MODEL ** -0.5 * beta),
        "ln1_w": 1.0 + nrm(ks[11], (DEPTH, D_MODEL), 0.02),
        "ln1_b": nrm(ks[12], (DEPTH, D_MODEL), 0.02),
        "w_ffn_up": nrm(ks[13], (DEPTH, D_MODEL, 2 * D_FF), D_MODEL ** -0.5),
        "ffn_conv_w": nrm(ks[14], (DEPTH, CONV_W, D_FF), CONV_W ** -0.5),
        "ffn_conv_b": nrm(ks[15], (DEPTH, D_FF), 0.02),
        "w_ffn_down": nrm(ks[16], (DEPTH, D_FF, D_MODEL), D_FF ** -0.5 * beta),
        "w_ple_gate": nrm(ks[17], (DEPTH, D_MODEL, D_MODEL), D_MODEL ** -0.5),
        "w_ple_proj": nrm(ks[18], (DEPTH, P_DIM, D_MODEL), P_DIM ** -0.5 * beta),
        "ln2_w": 1.0 + nrm(ks[19], (DEPTH, D_MODEL), 0.02),
        "ln2_b": nrm(ks[20], (DEPTH, D_MODEL), 0.02),
    }


def reference(x, p, w_in, a_lambda, a_norm_w, ret_norm_w, gla_w_a2, gla_b_a, gla_norm_w,
              w_branch, w_out, ln1_w, ln1_b, w_ffn_up, ffn_conv_w, ffn_conv_b, w_ffn_down,
              w_ple_gate, w_ple_proj, ln2_w, ln2_b):
    f32 = jnp.float32
    bn, s, _ = x.shape
    slopes = _alibi_slopes()
    slopes_a, slopes_b = slopes[0::2], slopes[1::2]
    for i in range(DEPTH):
        z = (x @ w_in[i]).astype(f32)
        (a_q, a_k, a_v, b_q, b_k, b_v, b_iq, b_ik, b_iw,
         c_q, c_k, c_v, c_g, d_q, d_k, d_v, d_a, d_g, m_g) = _split_columns(z)

        lam_init = 0.8 - 0.6 * math.exp(-0.3 * i)
        lam_p = a_lambda[i].astype(f32)
        lam = jnp.exp(jnp.sum(lam_p[0] * lam_p[1])) - jnp.exp(jnp.sum(lam_p[2] * lam_p[3])) + lam_init
        y_a = _diff_attention(a_q.reshape(bn, s, DA_HEADS, 2, DA_QK), a_k.reshape(bn, s, DA_HEADS, 2, DA_QK),
                              a_v.reshape(bn, s, DA_HEADS, DA_V), lam, a_norm_w[i].astype(f32), lam_init, slopes_a)

        y_b = _dsa_attention(b_q.reshape(bn, s, DSA_HEADS, DSA_HD), b_k.reshape(bn, s, DSA_HEADS, DSA_HD),
                             b_v.reshape(bn, s, DSA_HEADS, DSA_HD), b_iq.reshape(bn, s, IDX_HEADS, IDX_HD),
                             b_ik, b_iw * IDX_HEADS ** -0.5, slopes_b)

        o_c = _retention(c_q.reshape(bn, s, RET_HEADS, RET_QK), c_k.reshape(bn, s, RET_HEADS, RET_QK),
                         c_v.reshape(bn, s, RET_HEADS, RET_V))
        o_c = _head_layer_norm(o_c, ret_norm_w[i].astype(f32).reshape(RET_HEADS, RET_V))
        y_c = jax.nn.silu(c_g) * o_c.reshape(bn, s, RET_HEADS * RET_V)

        log_a = jax.nn.log_sigmoid(d_a @ gla_w_a2[i].astype(f32) + gla_b_a[i].astype(f32)) / GLA_GATE_TEMP
        o_d = _gla(d_q.reshape(bn, s, GLA_HEADS, GLA_QK), d_k.reshape(bn, s, GLA_HEADS, GLA_QK),
                   d_v.reshape(bn, s, GLA_HEADS, GLA_V), log_a.reshape(bn, s, GLA_HEADS, GLA_QK))
        o_d = _head_rms_norm(o_d, gla_norm_w[i].astype(f32))
        y_d = jax.nn.silu(d_g) * o_d.reshape(bn, s, GLA_HEADS * GLA_V)

        gates = jax.nn.sigmoid(m_g).reshape(bn, s, N_BRANCH, D_MODEL)
        branches = (y_a, y_b, y_c, y_d)
        merged = gates[:, :, 0] * (branches[0].astype(x.dtype) @ w_branch[i, 0])
        for n in range(1, N_BRANCH):
            merged = merged + gates[:, :, n] * (branches[n].astype(x.dtype) @ w_branch[i, n])
        mix = merged.astype(x.dtype) @ w_out[i]
        h = _layer_norm(DEEPNORM_ALPHA * x + mix, ln1_w[i], ln1_b[i])

        u, g = jnp.split(h @ w_ffn_up[i], 2, axis=-1)
        g = _causal_dwconv(g, ffn_conv_w[i], ffn_conv_b[i])
        f = (jax.nn.gelu(g) * u) @ w_ffn_down[i]

        e = jax.nn.sigmoid(h @ w_ple_gate[i]) * (p[i] @ w_ple_proj[i])
        x = _layer_norm(DEEPNORM_ALPHA * h + f + e, ln2_w[i], ln2_b[i])
    return x
```

```python
import functools
import math

import numpy as np
import jax
import jax.numpy as jnp
from jax import lax
from jax.experimental import pallas as pl
from jax.experimental.pallas import tpu as pltpu

D_MODEL = 1024
DEPTH = 2
P_DIM = 256
N_BRANCH = 4
BRANCH_W = 256
DA_HEADS = 4
DA_QK = 32
DA_V = 64
DSA_HEADS = 4
DSA_HD = 64
IDX_HEADS = 4
IDX_HD = 32
TOPK_MAX = 256
RET_HEADS = 4
RET_QK = 64
RET_V = 64
GLA_HEADS = 4
GLA_QK = 32
GLA_V = 64
GLA_RANK = 16
GLA_GATE_TEMP = 16.0
D_FF = 2816
CONV_W = 3
N_SOFTMAX_HEADS = DA_HEADS + DSA_HEADS
LN_EPS = 1e-5
NEG_INF = -1e30
DEEPNORM_ALPHA = (2.0 * DEPTH) ** 0.25

IN_SIZES = (256, 256, 256, 256, 256, 256, 128, 32, 4, 256, 256, 256, 256, 128, 128, 256, 16, 256, 4096)
IN_NAMES = ("a_q", "a_k", "a_v", "b_q", "b_k", "b_v", "b_iq", "b_ik", "b_iw",
            "c_q", "c_k", "c_v", "c_g", "d_q", "d_k", "d_v", "d_a", "d_g", "m_g")

_BF16 = jnp.bfloat16
_F32 = jnp.float32
_INT_MIN = -2 ** 31

CH = 256
TQ = 128
RET_C = 128
GLA_SUB = 16
TS_DENSE = 256
FF_COLS = 256
V7X_VMEM_LIMIT = 56 * 1024 * 1024


def _dot(a, b, precision=None):
    return jnp.dot(a, b, preferred_element_type=_F32, precision=precision)


def _dot_nt(a, b):
    return lax.dot_general(a, b, (((1,), (1,)), ((), ())), preferred_element_type=_F32)


def _iota(shape, dim):
    return lax.broadcasted_iota(jnp.int32, shape, dim)


def _block_diag_tile(m_t, ngroups):
    r, tq = m_t.shape
    tiled = jnp.concatenate([m_t] * ngroups, axis=1)
    keep = (_iota(tiled.shape, 0) // (r // ngroups)) == (_iota(tiled.shape, 1) // tq)
    return jnp.where(keep, tiled, jnp.zeros_like(tiled))


def _layer_norm(x, w, b):
    mu = jnp.mean(x, -1, keepdims=True)
    var = jnp.mean(jnp.square(x - mu), -1, keepdims=True)
    return (x - mu) * lax.rsqrt(var + LN_EPS) * w + b


def _group_mean(x, group):
    lane_g = _iota(x.shape, 1) // group
    out = jnp.zeros_like(x)
    for h in range(x.shape[1] // group):
        mk = lane_g == h
        mh = jnp.sum(jnp.where(mk, x, 0.0), axis=1, keepdims=True) * (1.0 / group)
        out = jnp.where(mk, mh, out)
    return out


def _const_spec(shape):
    nd = len(shape)
    return pl.BlockSpec(shape, lambda *_: (0,) * nd, pipeline_mode=pl.Buffered(1))


def _proj_kernel(x_ref, wn32_ref, wn16_ref, wt16_ref, wt32_ref, n32_ref, n16_ref, t16_ref, t32_ref):
    x = x_ref[...].astype(_BF16)
    n32_ref[...] = _dot(x, wn32_ref[...])
    n16_ref[...] = _dot(x, wn16_ref[...]).astype(_BF16)
    t16_ref[0, 0] = _dot_nt(wt16_ref[...], x).astype(_BF16)
    t32_ref[0, 0] = _dot_nt(wt32_ref[...], x)


def _project(x2, wn32, wn16, wt16, wt32, bn, s):
    ns = s // CH
    n32, n16, t16, t32 = wn32.shape[1], wn16.shape[1], wt16.shape[0], wt32.shape[0]
    return pl.pallas_call(
        _proj_kernel,
        grid=(bn, ns),
        in_specs=[
            pl.BlockSpec((CH, D_MODEL), lambda b, i: (b * ns + i, 0)),
            _const_spec(wn32.shape), _const_spec(wn16.shape), _const_spec(wt16.shape), _const_spec(wt32.shape),
        ],
        out_specs=[
            pl.BlockSpec((CH, n32), lambda b, i: (b * ns + i, 0)),
            pl.BlockSpec((CH, n16), lambda b, i: (b * ns + i, 0)),
            pl.BlockSpec((1, 1, t16, CH), lambda b, i: (b, i, 0, 0)),
            pl.BlockSpec((1, 1, t32, CH), lambda b, i: (b, i, 0, 0)),
        ],
        out_shape=[
            jax.ShapeDtypeStruct((bn * s, n32), _F32),
            jax.ShapeDtypeStruct((bn * s, n16), _BF16),
            jax.ShapeDtypeStruct((bn, ns, t16, CH), _BF16),
            jax.ShapeDtypeStruct((bn, ns, t32, CH), _F32),
        ],
        compiler_params=pltpu.CompilerParams(
            dimension_semantics=("parallel", "parallel"), vmem_limit_bytes=V7X_VMEM_LIMIT),
        name="proj",
    )(x2, wn32, wn16, wt16, wt32)


def _softmax_chunk(t, vt_c, m_ref, l_ref, acc_ref, w):
    m_old = m_ref[...]
    m_new = jnp.maximum(m_old, jnp.max(t, axis=0, keepdims=True))
    p = jnp.exp(t - m_new)
    alpha = jnp.exp(m_old - m_new)
    l_ref[...] = alpha * l_ref[...] + jnp.sum(p, axis=0, keepdims=True)
    m_ref[...] = m_new
    pb = p.astype(_BF16)
    for h in range(4):
        cols = slice(h * w, (h + 1) * w)
        acc_ref[h] = alpha[:, cols] * acc_ref[h] + _dot(vt_c[h * 64:(h + 1) * 64, :], pb[:, cols])


def _slope_row(slopes, reps, tq):
    return jnp.concatenate([jnp.full((1, tq), s, _F32) for s in slopes for _ in range(reps)], axis=1)


def _attn_a_kernel(lam_ref, nw_ref, qt_ref, k_ref, vt_ref, o_ref,
                   bd_ref, ab_ref, m_ref, l_ref, acc_ref, *, tq, slopes, lam_init):
    g = 2 * DA_HEADS
    q0 = pl.program_id(1) * tq
    scale = DA_QK ** -0.5
    bd_ref[...] = _block_diag_tile(qt_ref[0, 0], g)
    slope_row = _slope_row(slopes, 2, tq)
    ab_ref[...] = slope_row * _iota((CH, g * tq), 0).astype(_F32)
    m_ref[...] = jnp.full(m_ref.shape, NEG_INF, _F32)
    l_ref[...] = jnp.zeros(l_ref.shape, _F32)
    acc_ref[...] = jnp.zeros(acc_ref.shape, _F32)
    nfull = q0 // CH

    def chunk(j, masked):
        k0 = pl.multiple_of(j * CH, CH)
        s = _dot(k_ref[0, pl.ds(k0, CH), :], bd_ref[...])
        t = s * scale + ab_ref[...] + slope_row * (k0 - q0).astype(_F32)
        if masked:
            rel = _iota(t.shape, 0) - (_iota(t.shape, 1) & (tq - 1))
            t = jnp.where(rel <= q0 - k0, t, NEG_INF)
        _softmax_chunk(t, vt_ref[0, j], m_ref, l_ref, acc_ref, 2 * tq)

    def body(j, c):
        chunk(j, False)
        return c

    lax.fori_loop(0, nfull, body, 0)
    chunk(nfull, True)

    lp = lam_ref[...]
    lam = (jnp.exp(jnp.sum(lp[0:1] * lp[1:2], axis=1, keepdims=True))
           - jnp.exp(jnp.sum(lp[2:3] * lp[3:4], axis=1, keepdims=True)) + lam_init)
    linv = 1.0 / l_ref[...]
    outs = []
    for h in range(DA_HEADS):
        a = acc_ref[h] * linv[:, h * 2 * tq:(h + 1) * 2 * tq]
        o = a[:, :tq] - lam * a[:, tq:]
        ms = jnp.mean(o * o, axis=0, keepdims=True)
        outs.append(o * lax.rsqrt(ms + LN_EPS) * nw_ref[...] * (1.0 - lam_init))
    o_ref[0] = jnp.concatenate(outs, axis=0).T.astype(_BF16)


def _diff_attention(lam_p, norm_w, n16, t16, bn, s, slopes, lam_init):
    tq = TQ
    ns, per = s // CH, CH // tq
    g = 2 * DA_HEADS
    kern = functools.partial(_attn_a_kernel, tq=tq, slopes=slopes, lam_init=lam_init)
    return pl.pallas_call(
        kern,
        grid=(bn, s // tq),
        in_specs=[
            _const_spec(lam_p.shape), _const_spec(norm_w.shape),
            pl.BlockSpec((1, 1, 256, tq), lambda b, i: (b, i // per, 0, i % per)),
            pl.BlockSpec((1, s, 256), lambda b, i: (b, 0, 0)),
            pl.BlockSpec((1, ns, 256, CH), lambda b, i: (b, 0, 1, 0)),
        ],
        out_specs=pl.BlockSpec((1, tq, 256), lambda b, i: (b, i, 0)),
        out_shape=jax.ShapeDtypeStruct((bn, s, 256), _BF16),
        scratch_shapes=[
            pltpu.VMEM((256, g * tq), _BF16), pltpu.VMEM((CH, g * tq), _F32),
            pltpu.VMEM((1, g * tq), _F32), pltpu.VMEM((1, g * tq), _F32),
            pltpu.VMEM((DA_HEADS, DA_V, 2 * tq), _F32),
        ],
        compiler_params=pltpu.CompilerParams(
            dimension_semantics=("parallel", "arbitrary"), vmem_limit_bytes=V7X_VMEM_LIMIT),
        name="diff_attn",
    )(lam_p, norm_w, t16, n16, t16)


def _dsa_kernel(iqt_ref, ik_ref, iwt_ref, qt_ref, k_ref, vt_ref, o_ref,
                iqbd_ref, bd_ref, key_ref, ab_ref, m_ref, l_ref, acc_ref, *, tq, slopes, topk):
    g = DSA_HEADS
    q0 = pl.program_id(1) * tq
    scale = DSA_HD ** -0.5
    nfull = q0 // CH
    nch = nfull + 1
    iqbd_ref[...] = _block_diag_tile(iqt_ref[0, 0], IDX_HEADS)
    bd_ref[...] = _block_diag_tile(qt_ref[0, 0], g)
    slope_row = _slope_row(slopes, 1, tq)
    ab_ref[...] = slope_row * _iota((CH, g * tq), 0).astype(_F32)
    w = iwt_ref[0, 0][0:IDX_HEADS, :] * (IDX_HEADS ** -0.5 * IDX_HD ** -0.5)

    def score_chunk(j, masked):
        k0 = pl.multiple_of(j * CH, CH)
        lg = _dot(ik_ref[0, pl.ds(k0, CH), :], iqbd_ref[...])
        sc = jnp.maximum(lg[:, 0:tq], 0.0) * w[0:1]
        for h in range(1, IDX_HEADS):
            sc = sc + jnp.maximum(lg[:, h * tq:(h + 1) * tq], 0.0) * w[h:h + 1]
        sc = jnp.where(sc == 0.0, 0.0, sc)
        bits = pltpu.bitcast(sc, jnp.int32)
        key = bits ^ ((bits >> 31) & 0x7FFFFFFF)
        if masked:
            rel = _iota(key.shape, 0) - _iota(key.shape, 1)
            key = jnp.where(rel <= q0 - k0, key, _INT_MIN)
        key_ref[j] = key

    def score_body(j, c):
        score_chunk(j, False)
        return c

    lax.fori_loop(0, nfull, score_body, 0)
    score_chunk(nfull, True)

    def count_ge(cand):
        def cnt_body(j, acc8):
            ge = jnp.where(key_ref[j] >= cand, 1, 0)
            return acc8 + jnp.sum(ge.reshape(CH // 8, 8, tq), axis=0)
        acc8 = lax.fori_loop(0, nch, cnt_body, jnp.zeros((8, tq), jnp.int32))
        return jnp.sum(acc8, axis=0, keepdims=True)

    def bit_step(b, carry):
        thr, cnt_thr = carry
        cand = thr + jnp.left_shift(jnp.int32(1), 31 - b)
        cnt = count_ge(cand)
        ok = cnt >= topk
        return jnp.where(ok, cand, thr), jnp.where(ok, cnt, cnt_thr)

    thr, cnt_thr = lax.fori_loop(
        0, 32, bit_step,
        (jnp.full((1, tq), _INT_MIN, jnp.int32), jnp.full((1, tq), 2 ** 30, jnp.int32)))
    need_tie = jnp.max(jnp.where((cnt_thr > topk) & (thr > _INT_MIN), 1, 0)) > 0

    def attend(tie):
        m_ref[...] = jnp.full(m_ref.shape, NEG_INF, _F32)
        l_ref[...] = jnp.zeros(l_ref.shape, _F32)
        acc_ref[...] = jnp.zeros(acc_ref.shape, _F32)
        if tie:
            room = (topk - count_ge(thr + 1)).astype(_F32)
            lower = jnp.where(_iota((CH, CH), 0) > _iota((CH, CH), 1), 1.0, 0.0).astype(_BF16)
        else:
            thr_eff = jnp.maximum(thr, _INT_MIN + 1)

        def chunk(j, seen):
            k0 = pl.multiple_of(j * CH, CH)
            key = key_ref[j]
            if tie:
                eq = key == thr
                eqf = jnp.where(eq, 1.0, 0.0)
                rank = _dot(lower, eqf.astype(_BF16)) + seen
                sel = ((key > thr) | (eq & (rank < room))) & (key > _INT_MIN)
                seen = seen + jnp.sum(eqf, axis=0, keepdims=True)
            else:
                sel = key >= thr_eff
            s = _dot(k_ref[0, pl.ds(k0, CH), :], bd_ref[...])
            t = s * scale + ab_ref[...] + slope_row * (k0 - q0).astype(_F32)
            t = jnp.concatenate(
                [jnp.where(sel, t[:, h * tq:(h + 1) * tq], NEG_INF) for h in range(g)], axis=1)
            _softmax_chunk(t, vt_ref[0, j], m_ref, l_ref, acc_ref, tq)
            return seen

        lax.fori_loop(0, nch, chunk, jnp.zeros((1, tq), _F32))

    @pl.when(need_tie)
    def _():
        attend(True)

    @pl.when(jnp.logical_not(need_tie))
    def _():
        attend(False)

    linv = 1.0 / l_ref[...]
    outs = [acc_ref[h] * linv[:, h * tq:(h + 1) * tq] for h in range(g)]
    o_ref[0] = jnp.concatenate(outs, axis=0).T.astype(_BF16)


def _dsa_attention(n16, t16, t32, bn, s, slopes):
    tq = TQ
    ns, per = s // CH, CH // tq
    g = DSA_HEADS
    topk = min(TOPK_MAX, s // 4)
    kern = functools.partial(_dsa_kernel, tq=tq, slopes=slopes, topk=topk)
    return pl.pallas_call(
        kern,
        grid=(bn, s // tq),
        in_specs=[
            pl.BlockSpec((1, 1, 128, tq), lambda b, i: (b, i // per, 10, i % per)),
            pl.BlockSpec((1, s, 128), lambda b, i: (b, 0, 10)),
            pl.BlockSpec((1, 1, 16, tq), lambda b, i: (b, i // per, 16, i % per)),
            pl.BlockSpec((1, 1, 256, tq), lambda b, i: (b, i // per, 2, i % per)),
            pl.BlockSpec((1, s, 256), lambda b, i: (b, 0, 1)),
            pl.BlockSpec((1, ns, 256, CH), lambda b, i: (b, 0, 3, 0)),
        ],
        out_specs=pl.BlockSpec((1, tq, 256), lambda b, i: (b, i, 0)),
        out_shape=jax.ShapeDtypeStruct((bn, s, 256), _BF16),
        scratch_shapes=[
            pltpu.VMEM((128, IDX_HEADS * tq), _BF16), pltpu.VMEM((256, g * tq), _BF16),
            pltpu.VMEM((ns, CH, tq), jnp.int32), pltpu.VMEM((CH, g * tq), _F32),
            pltpu.VMEM((1, g * tq), _F32), pltpu.VMEM((1, g * tq), _F32),
            pltpu.VMEM((DSA_HEADS, DSA_HD, tq), _F32),
        ],
        compiler_params=pltpu.CompilerParams(
            dimension_semantics=("parallel", "arbitrary"), vmem_limit_bytes=V7X_VMEM_LIMIT),
        name="dsa_attn",
    )(t16, n16, t32, t16, n16, t16)


def _ret_kernel(q_ref, kt_ref, v_ref, g_ref, intra_ref, qdec_ref, kdect_ref, cd_ref, nw_ref, o_ref, s_ref, *, c):
    @pl.when(pl.program_id(1) == 0)
    def _():
        s_ref[...] = jnp.zeros(s_ref.shape, _F32)

    q = q_ref[0]
    v = v_ref[0]
    kt = kt_ref[0, 0] * (RET_QK ** -0.5)
    att = _dot(q, _block_diag_tile(kt.astype(_BF16), RET_HEADS)) * intra_ref[...]
    vt = jnp.concatenate([v] * RET_HEADS, axis=0)
    vbd = jnp.where((_iota(vt.shape, 0) // c) == (_iota(vt.shape, 1) // RET_V), vt, jnp.zeros_like(vt))
    st = s_ref[...]
    o = _dot(att.astype(_BF16), vbd) + _dot(q, st.astype(_BF16)) * qdec_ref[...]
    upd = _dot((kt * kdect_ref[...]).astype(_BF16), v)
    same_head = (_iota(upd.shape, 0) // RET_QK) == (_iota(upd.shape, 1) // RET_V)
    s_ref[...] = st * cd_ref[...] + jnp.where(same_head, upd, 0.0)

    mu = _group_mean(o, RET_V)
    d = o - mu
    var = _group_mean(d * d, RET_V)
    y = d * lax.rsqrt(var + LN_EPS) * nw_ref[...]
    gate = g_ref[0]
    o_ref[0] = (gate * jax.nn.sigmoid(gate) * y).astype(_BF16)


def _retention_consts(c):
    h = RET_HEADS
    log_g = np.log1p(-np.power(2.0, -5.0 - np.arange(h, dtype=np.float64)))
    pos = np.arange(c, dtype=np.float64)
    rel = pos[:, None] - pos[None, :]
    intra = np.where(rel >= 0, np.exp(log_g[:, None, None] * np.maximum(rel, 0.0)), 0.0)
    intra = np.transpose(intra, (1, 0, 2)).reshape(c, h * c)
    qdec = np.repeat(np.exp(log_g[:, None] * (pos[None, :] + 1.0)).T, RET_V, axis=1)
    kdect = np.repeat(np.exp(log_g[:, None] * (c - 1.0 - pos[None, :])), RET_QK, axis=0)
    cd = np.repeat(np.exp(log_g * c), RET_QK)[:, None] * np.ones((1, h * RET_V))
    return tuple(jnp.asarray(a, _F32) for a in (intra, qdec, kdect, cd))


def _retention(norm_w, n16, n32, t32, bn, s):
    c = RET_C
    per = CH // c
    intra, qdec, kdect, cd = _retention_consts(c)
    return pl.pallas_call(
        functools.partial(_ret_kernel, c=c),
        grid=(bn, s // c),
        in_specs=[
            pl.BlockSpec((1, c, 256), lambda b, i: (b, i, 2)),
            pl.BlockSpec((1, 1, 256, c), lambda b, i: (b, i // per, 0, i % per)),
            pl.BlockSpec((1, c, 256), lambda b, i: (b, i, 3)),
            pl.BlockSpec((1, c, 256), lambda b, i: (b, i, 0)),
            _const_spec(intra.shape), _const_spec(qdec.shape), _const_spec(kdect.shape),
            _const_spec(cd.shape), _const_spec(norm_w.shape),
        ],
        out_specs=pl.BlockSpec((1, c, 256), lambda b, i: (b, i, 0)),
        out_shape=jax.ShapeDtypeStruct((bn, s, 256), _BF16),
        scratch_shapes=[pltpu.VMEM((RET_HEADS * RET_QK, RET_HEADS * RET_V), _F32)],
        compiler_params=pltpu.CompilerParams(dimension_semantics=("parallel", "arbitrary")),
        name="retention",
    )(n16, t32, n16, n32, intra, qdec, kdect, cd, norm_w)


def _gla_kernel(q_ref, k_ref, a_ref, v_ref, vt_ref, g_ref, wa_ref, ba_ref, nw_ref, o_ref,
                st_ref, u_ref, oacc_ref, qs_ref, kk_ref, b_ref, qh_ref, dec_ref, *, ts, sub):
    nsub = ts // sub
    hi = lax.Precision.HIGHEST

    @pl.when(pl.program_id(1) == 0)
    def _():
        st_ref[...] = jnp.zeros(st_ref.shape, _F32)

    la = jax.nn.log_sigmoid(_dot(a_ref[0].astype(_BF16), wa_ref[...]) + ba_ref[...]) * (1.0 / GLA_GATE_TEMP)
    rr, cc = _iota((ts, ts), 0), _iota((ts, ts), 1)
    same = (rr // sub) == (cc // sub)
    b = _dot(jnp.where(same & (cc <= rr), 1.0, 0.0), la, precision=hi)
    bl = _dot(jnp.where(same, 1.0, 0.0), la, precision=hi)
    qs = q_ref[0] * (GLA_QK ** -0.5)
    kk = k_ref[0]
    kd = (kk * jnp.exp(bl - b)).astype(_BF16)
    qs_ref[...] = qs
    kk_ref[...] = kk
    b_ref[...] = b
    qh_ref[...] = (qs * jnp.exp(b)).astype(_BF16)
    dec_ref[...] = jnp.exp(bl)
    vt = vt_ref[0, 0]
    row_blk = _iota(kd.shape, 0) // sub
    for n in range(nsub):
        u_ref[n] = _dot(vt, jnp.where(row_blk == n, kd, jnp.zeros_like(kd)))

    st_keep = (_iota(st_ref.shape, 0) // GLA_V) == (_iota(st_ref.shape, 1) // GLA_QK)
    spread = jnp.where((_iota((128, 256), 0) // GLA_QK) == (_iota((128, 256), 1) // GLA_V), 1.0, 0.0).astype(_BF16)
    row16 = _iota((sub, 128), 0)

    def body(n, carry):
        r0 = pl.multiple_of(n * sub, sub)
        st = st_ref[...]
        o_cross = _dot_nt(qh_ref[pl.ds(r0, sub), :], st.astype(_BF16))
        q16 = qs_ref[pl.ds(r0, sub), :]
        k16 = kk_ref[pl.ds(r0, sub), :]
        b16 = b_ref[pl.ds(r0, sub), :]
        v16 = v_ref[0, pl.ds(r0, sub), :].astype(_F32)
        es = []
        for j in range(sub):
            e = q16 * k16[j:j + 1] * jnp.exp(jnp.minimum(b16 - b16[j:j + 1], 0.0))
            es.append(jnp.where(row16 >= j, e, 0.0))
        e_all = jnp.concatenate(es, axis=0)
        e_hi = e_all.astype(_BF16)
        e_lo = (e_all - e_hi.astype(_F32)).astype(_BF16)
        att = _dot(e_hi, spread) + _dot(e_lo, spread)
        o_diag = att[0:sub] * v16[0:1]
        for j in range(1, sub):
            o_diag = o_diag + att[j * sub:(j + 1) * sub] * v16[j:j + 1]
        oacc_ref[pl.ds(r0, sub), :] = o_cross + o_diag
        st_ref[...] = st * dec_ref[pl.ds(r0, 1), :] + jnp.where(st_keep, u_ref[n], 0.0)
        return carry

    lax.fori_loop(0, nsub, body, 0)

    o = oacc_ref[...]
    ms = _group_mean(o * o, GLA_V)
    y = o * lax.rsqrt(ms + LN_EPS) * nw_ref[...]
    gate = g_ref[0]
    o_ref[0] = (gate * jax.nn.sigmoid(gate) * y).astype(_BF16)


def _gla(wa, ba, norm_w, n16, n32, t16, bn, s):
    ts, sub = CH, GLA_SUB
    return pl.pallas_call(
        functools.partial(_gla_kernel, ts=ts, sub=sub),
        grid=(bn, s // ts),
        in_specs=[
            pl.BlockSpec((1, ts, 128), lambda b, i: (b, i, 4)),
            pl.BlockSpec((1, ts, 128), lambda b, i: (b, i, 5)),
            pl.BlockSpec((1, ts, 128), lambda b, i: (b, i, 6)),
            pl.BlockSpec((1, ts, 256), lambda b, i: (b, i, 4)),
            pl.BlockSpec((1, 1, 256, ts), lambda b, i: (b, i, 4, 0)),
            pl.BlockSpec((1, ts, 256), lambda b, i: (b, i, 1)),
            _const_spec(wa.shape), _const_spec(ba.shape), _const_spec(norm_w.shape),
        ],
        out_specs=pl.BlockSpec((1, ts, 256), lambda b, i: (b, i, 0)),
        out_shape=jax.ShapeDtypeStruct((bn, s, 256), _BF16),
        scratch_shapes=[
            pltpu.VMEM((GLA_HEADS * GLA_V, GLA_HEADS * GLA_QK), _F32),
            pltpu.VMEM((ts // sub, GLA_HEADS * GLA_V, GLA_HEADS * GLA_QK), _F32),
            pltpu.VMEM((ts, 256), _F32),
            pltpu.VMEM((ts, 128), _F32), pltpu.VMEM((ts, 128), _F32), pltpu.VMEM((ts, 128), _F32),
            pltpu.VMEM((ts, 128), _BF16), pltpu.VMEM((ts, 128), _F32),
        ],
        compiler_params=pltpu.CompilerParams(dimension_semantics=("parallel", "arbitrary")),
        name="gla",
    )(n32, n32, n32, n16, t16, n32, wa, ba, norm_w)


def _merge_kernel(x_ref, ya_ref, yb_ref, yc_ref, yd_ref, wg_ref, wbr_ref, wout_ref, lnw_ref, lnb_ref, h_ref):
    x = x_ref[...]
    xb = x.astype(_BF16)
    merged = None
    for n, y_ref in enumerate((ya_ref, yb_ref, yc_ref, yd_ref)):
        gate = jax.nn.sigmoid(_dot(xb, wg_ref[:, n * D_MODEL:(n + 1) * D_MODEL]))
        term = gate * _dot(y_ref[...], wbr_ref[n])
        merged = term if merged is None else merged + term
    mix = _dot(merged.astype(_BF16), wout_ref[...])
    h_ref[...] = _layer_norm(DEEPNORM_ALPHA * x + mix, lnw_ref[...], lnb_ref[...])


def _merge(x2, ys, wg, wbr, wout, lnw, lnb):
    t = x2.shape[0]
    ts = TS_DENSE
    tok = lambda w: pl.BlockSpec((ts, w), lambda i: (i, 0))
    return pl.pallas_call(
        _merge_kernel,
        grid=(t // ts,),
        in_specs=[tok(D_MODEL)] + [tok(BRANCH_W)] * 4 + [
            _const_spec(wg.shape), _const_spec(wbr.shape), _const_spec(wout.shape),
            _const_spec(lnw.shape), _const_spec(lnb.shape)],
        out_specs=tok(D_MODEL),
        out_shape=jax.ShapeDtypeStruct((t, D_MODEL), _F32),
        compiler_params=pltpu.CompilerParams(
            dimension_semantics=("parallel",), vmem_limit_bytes=V7X_VMEM_LIMIT),
        name="merge_ln",
    )(x2, *ys, wg, wbr, wout, lnw, lnb)


def _ffn_kernel(h_ref, p_ref, wup_ref, cw_ref, cb_ref, wdn_ref, wpg_ref, wpp_ref, lnw_ref, lnb_ref,
                o_ref, tail_ref, *, ts):
    @pl.when(pl.program_id(1) == 0)
    def _():
        tail_ref[...] = jnp.zeros(tail_ref.shape, _F32)

    h = h_ref[...]
    hb = h.astype(_BF16)
    row = _iota((ts, FF_COLS), 0)
    f = None
    for c0 in range(0, D_FF, FF_COLS):
        cols = slice(c0, c0 + FF_COLS)
        u = _dot(hb, wup_ref[:, cols])
        gt = _dot(hb, wup_ref[:, D_FF + c0:D_FF + c0 + FF_COLS])
        prev = tail_ref[:, cols]
        g1 = jnp.where(row == 0, prev[7:8], pltpu.roll(gt, 1, 0))
        g2 = jnp.where(row == 0, prev[6:7], jnp.where(row == 1, prev[7:8], pltpu.roll(gt, 2, 0)))
        tail_ref[:, cols] = gt[ts - 8:ts]
        gc = cb_ref[:, cols] + cw_ref[0:1, cols] * g2
        gc = gc + cw_ref[1:2, cols] * g1
        gc = gc + cw_ref[2:3, cols] * gt
        term = _dot((jax.nn.gelu(gc) * u).astype(_BF16), wdn_ref[cols, :])
        f = term if f is None else f + term
    e = jax.nn.sigmoid(_dot(hb, wpg_ref[...])) * _dot(p_ref[...].astype(_BF16), wpp_ref[...])
    o_ref[...] = _layer_norm(DEEPNORM_ALPHA * h + f + e, lnw_ref[...], lnb_ref[...])


def _ffn(h2, p2, wup, cw, cb, wdn, wpg, wpp, lnw, lnb, bn, s):
    ts = TS_DENSE
    ns = s // ts
    tok = lambda w: pl.BlockSpec((ts, w), lambda b, i: (b * ns + i, 0))
    return pl.pallas_call(
        functools.partial(_ffn_kernel, ts=ts),
        grid=(bn, ns),
        in_specs=[tok(D_MODEL), tok(P_DIM)] + [_const_spec(a.shape) for a in (wup, cw, cb, wdn, wpg, wpp, lnw, lnb)],
        out_specs=tok(D_MODEL),
        out_shape=jax.ShapeDtypeStruct((bn * s, D_MODEL), _F32),
        scratch_shapes=[pltpu.VMEM((8, D_FF), _F32)],
        compiler_params=pltpu.CompilerParams(
            dimension_semantics=("parallel", "arbitrary"), vmem_limit_bytes=V7X_VMEM_LIMIT),
        name="ffn_ple_ln",
    )(h2, p2, wup, cw, cb, wdn, wpg, wpp, lnw, lnb)


def _projection_weights(w):
    offs = np.concatenate([[0], np.cumsum(IN_SIZES)])
    col = {n: w[:, offs[i]:offs[i + 1]] for i, n in enumerate(IN_NAMES)}
    pad = lambda a, n: jnp.pad(a, ((0, 0), (0, n - a.shape[1])))
    wn32 = jnp.concatenate([col["c_g"], col["d_g"], col["d_q"], col["d_k"], pad(col["d_a"], 128)], axis=1)
    wn16 = jnp.concatenate([col["a_k"], col["b_k"], col["c_q"], col["c_v"], col["d_v"]]
                           + [col["b_ik"]] * IDX_HEADS, axis=1)
    wt16 = jnp.concatenate([col["a_q"], col["a_v"], col["b_q"], col["b_v"], col["d_v"], col["b_iq"]], axis=1).T
    wt32 = jnp.concatenate([col["c_k"], pad(col["b_iw"], 16)], axis=1).T
    return [a.astype(_BF16) for a in (wn32, wn16, wt16, wt32)], col["m_g"].astype(_BF16)


def kernel(x, p, w_in, a_lambda, a_norm_w, ret_norm_w, gla_w_a2, gla_b_a, gla_norm_w, w_branch, w_out,
           ln1_w, ln1_b, w_ffn_up, ffn_conv_w, ffn_conv_b, w_ffn_down, w_ple_gate, w_ple_proj, ln2_w, ln2_b):
    bn, s, _ = x.shape
    t = bn * s
    slopes = [2.0 ** (-(8.0 / N_SOFTMAX_HEADS) * i) for i in range(1, N_SOFTMAX_HEADS + 1)]
    slopes_a, slopes_b = tuple(slopes[0::2]), tuple(slopes[1::2])
    row = lambda v: v.astype(_F32).reshape(1, -1)
    x2 = x.reshape(t, D_MODEL)
    for i in range(DEPTH):
        proj_w, wg = _projection_weights(w_in[i])
        n32, n16, t16, t32 = _project(x2, *proj_w, bn, s)
        n32 = n32.reshape(bn, s, -1)
        n16 = n16.reshape(bn, s, -1)
        lam_init = 0.8 - 0.6 * math.exp(-0.3 * i)
        y_a = _diff_attention(a_lambda[i].astype(_F32), a_norm_w[i].astype(_F32).reshape(DA_V, 1),
                              n16, t16, bn, s, slopes_a, lam_init)
        y_b = _dsa_attention(n16, t16, t32, bn, s, slopes_b)
        y_c = _retention(row(ret_norm_w[i]), n16, n32, t32, bn, s)
        wa = jnp.pad(gla_w_a2[i], ((0, 128 - GLA_RANK), (0, 0))).astype(_BF16)
        y_d = _gla(wa, row(gla_b_a[i]), row(jnp.tile(gla_norm_w[i], GLA_HEADS)), n16, n32, t16, bn, s)
        ys = [y.reshape(t, BRANCH_W) for y in (y_a, y_b, y_c, y_d)]
        h2 = _merge(x2, ys, wg, w_branch[i].astype(_BF16), w_out[i].astype(_BF16), row(ln1_w[i]), row(ln1_b[i]))
        x2 = _ffn(h2, p[i].reshape(t, P_DIM), w_ffn_up[i].astype(_BF16), ffn_conv_w[i].astype(_F32),
                  row(ffn_conv_b[i]), w_ffn_down[i].astype(_BF16), w_ple_gate[i].astype(_BF16),
                  w_ple_proj[i].astype(_BF16), row(ln2_w[i]), row(ln2_b[i]), bn, s)
    return x2.reshape(bn, s, D_MODEL)
```

```python
import functools
import math

import numpy as np
import jax
import jax.numpy as jnp
from jax import lax
from jax.experimental import pallas as pl
from jax.experimental.pallas import tpu as pltpu

D_MODEL = 1024
DEPTH = 2
P_DIM = 256
N_BRANCH = 4
BRANCH_W = 256
DA_HEADS = 4
DA_QK = 32
DA_V = 64
DSA_HEADS = 4
DSA_HD = 64
IDX_HEADS = 4
IDX_HD = 32
TOPK_MAX = 256
RET_HEADS = 4
RET_QK = 64
RET_V = 64
GLA_HEADS = 4
GLA_QK = 32
GLA_V = 64
GLA_RANK = 16
GLA_GATE_TEMP = 16.0
D_FF = 2816
CONV_W = 3
N_SOFTMAX_HEADS = DA_HEADS + DSA_HEADS
LN_EPS = 1e-5
NEG_INF = -1e30
DEEPNORM_ALPHA = (2.0 * DEPTH) ** 0.25

IN_SIZES = (256, 256, 256, 256, 256, 256, 128, 32, 4, 256, 256, 256, 256, 128, 128, 256, 16, 256, 4096)
IN_NAMES = ("a_q", "a_k", "a_v", "b_q", "b_k", "b_v", "b_iq", "b_ik", "b_iw",
            "c_q", "c_k", "c_v", "c_g", "d_q", "d_k", "d_v", "d_a", "d_g", "m_g")

_BF16 = jnp.bfloat16
_F32 = jnp.float32
_INT_MIN = -2 ** 31
_I16_MIN = -2 ** 15
_LOG2E = math.log2(math.e)

CH = 256
TQ = 128
RET_C = 128
GLA_SUB = 16
COUNT_UNROLL = 4
TS_DENSE = 256
FF_COLS = 256
V7X_VMEM_LIMIT = 56 * 1024 * 1024


def _dot(a, b, precision=None):
    return jnp.dot(a, b, preferred_element_type=_F32, precision=precision)


def _dot_nt(a, b):
    return lax.dot_general(a, b, (((1,), (1,)), ((), ())), preferred_element_type=_F32)


def _iota(shape, dim):
    return lax.broadcasted_iota(jnp.int32, shape, dim)


def _block_diag_tile(m_t, ngroups):
    r, tq = m_t.shape
    tiled = jnp.concatenate([m_t] * ngroups, axis=1)
    keep = (_iota(tiled.shape, 0) // (r // ngroups)) == (_iota(tiled.shape, 1) // tq)
    return jnp.where(keep, tiled, jnp.zeros_like(tiled))


def _layer_norm(x, w, b):
    mu = jnp.mean(x, -1, keepdims=True)
    var = jnp.mean(jnp.square(x - mu), -1, keepdims=True)
    return (x - mu) * lax.rsqrt(var + LN_EPS) * w + b


def _group_mean(x, group):
    lane_g = _iota(x.shape, 1) // group
    out = jnp.zeros_like(x)
    for h in range(x.shape[1] // group):
        mk = lane_g == h
        mh = jnp.sum(jnp.where(mk, x, 0.0), axis=1, keepdims=True) * (1.0 / group)
        out = jnp.where(mk, mh, out)
    return out


def _const_spec(shape):
    nd = len(shape)
    return pl.BlockSpec(shape, lambda *_: (0,) * nd, pipeline_mode=pl.Buffered(1))


def _proj_kernel(x_ref, wn32_ref, wn16_ref, wt16_ref, wt32_ref, n32_ref, n16_ref, t16_ref, t32_ref,
                 wt16t_ref, wt32t_ref):
    @pl.when((pl.program_id(0) == 0) & (pl.program_id(1) == 0))
    def _():
        wt16t_ref[...] = wt16_ref[...].T
        wt32t_ref[...] = wt32_ref[...].T

    x = x_ref[...].astype(_BF16)
    n32_ref[...] = _dot(x, wn32_ref[...])
    n16_ref[...] = _dot(x, wn16_ref[...]).astype(_BF16)
    t16_ref[0, 0] = _dot_nt(wt16t_ref[...], x).astype(_BF16)
    t32_ref[0, 0] = _dot_nt(wt32t_ref[...], x)


def _project(x2, wn32, wn16, wt16, wt32, bn, s):
    ns = s // CH
    n32, n16, t16, t32 = wn32.shape[1], wn16.shape[1], wt16.shape[1], wt32.shape[1]
    return pl.pallas_call(
        _proj_kernel,
        grid=(bn, ns),
        scratch_shapes=[pltpu.VMEM((t16, D_MODEL), _BF16), pltpu.VMEM((t32, D_MODEL), _BF16)],
        in_specs=[
            pl.BlockSpec((CH, D_MODEL), lambda b, i: (b * ns + i, 0)),
            _const_spec(wn32.shape), _const_spec(wn16.shape), _const_spec(wt16.shape), _const_spec(wt32.shape),
        ],
        out_specs=[
            pl.BlockSpec((CH, n32), lambda b, i: (b * ns + i, 0)),
            pl.BlockSpec((CH, n16), lambda b, i: (b * ns + i, 0)),
            pl.BlockSpec((1, 1, t16, CH), lambda b, i: (b, i, 0, 0)),
            pl.BlockSpec((1, 1, t32, CH), lambda b, i: (b, i, 0, 0)),
        ],
        out_shape=[
            jax.ShapeDtypeStruct((bn * s, n32), _F32),
            jax.ShapeDtypeStruct((bn * s, n16), _BF16),
            jax.ShapeDtypeStruct((bn, ns, t16, CH), _BF16),
            jax.ShapeDtypeStruct((bn, ns, t32, CH), _F32),
        ],
        compiler_params=pltpu.CompilerParams(
            dimension_semantics=("arbitrary", "arbitrary"), vmem_limit_bytes=V7X_VMEM_LIMIT),
        name="proj",
    )(x2, wn32, wn16, wt16, wt32)


def _slope_row(slopes, reps, tq):
    return jnp.concatenate([jnp.full((1, tq), s * _LOG2E, _F32) for s in slopes for _ in range(reps)], axis=1)


def _alibi_rows(slopes):
    rows = _iota((CH, 128), 0).astype(_F32)
    return jnp.stack([rows * (s * _LOG2E) for s in slopes], axis=0)


def _attend(k_ref, vt_ref, bd_ref, ab_ref, slope_row, c1, q0, nfull, tq, w, mask_fn, aux0):
    g_tq = bd_ref.shape[1]

    def pv(j, p):
        vt_c = vt_ref[0, j]
        return [_dot(vt_c[h * 64:(h + 1) * 64, :], p[:, h * w:(h + 1) * w]) for h in range(4)]

    def qk(j):
        kc = k_ref[0, pl.ds(pl.multiple_of(j * CH, CH), CH), :]
        return [_dot(kc, bd_ref[:, c0:c0 + 256]) for c0 in range(0, g_tq, 256)]

    def softmax(s, j, m, l, aux, diag):
        crow = slope_row * (j * CH - q0).astype(_F32)
        amask, aux = mask_fn(j, aux, diag)
        ps, ms, ls, alphas = [], [], [], []
        for c0 in range(0, g_tq, 128):
            cols = slice(c0, c0 + 128)
            t = s[c0 // 256][:, c0 % 256:c0 % 256 + 128] * c1 + ab_ref[c0 // w]
            if amask is not None:
                t = t + amask[:, c0 % tq:c0 % tq + 128]
            m_new = jnp.maximum(m[:, cols], jnp.max(t, axis=0, keepdims=True) + crow[:, cols])
            alpha = jnp.exp2(m[:, cols] - m_new)
            p = jnp.exp2(t - (m_new - crow[:, cols]))
            ls.append(alpha * l[:, cols] + jnp.sum(p, axis=0, keepdims=True))
            ps.append(p.astype(_BF16))
            ms.append(m_new)
            alphas.append(alpha)
        cat = lambda xs: jnp.concatenate(xs, axis=1)
        return cat(ms), cat(ls), cat(alphas), cat(ps), aux

    def rescale(acc, pvs, alpha):
        return [alpha[:, h * w:(h + 1) * w] * (acc[h] + pvs[h]) for h in range(4)]

    def step(s, j, carry, diag=False):
        p, m, l, acc, aux = carry
        pvs = pv(jnp.maximum(j - 1, 0), p)
        m, l, alpha, p, aux = softmax(s, j, m, l, aux, diag)
        return p, m, l, rescale(acc, pvs, alpha), aux

    def pair(i, carry):
        s_a, s_b = qk(2 * i), qk(2 * i + 1)
        return step(s_b, 2 * i + 1, step(s_a, 2 * i, carry))

    init = (jnp.zeros((CH, g_tq), _BF16), jnp.full((1, g_tq), NEG_INF, _F32),
            jnp.zeros((1, g_tq), _F32), [jnp.zeros((64, w), _F32) for _ in range(4)], aux0)
    npair = nfull // 2
    carry = lax.fori_loop(0, npair, pair, init)
    carry = lax.fori_loop(2 * npair, nfull, lambda j, c: step(qk(j), j, c), carry)
    p, m, l, acc, aux = step(qk(nfull), nfull, carry, True)
    pvs = pv(nfull, p)
    return l, [acc[h] + pvs[h] for h in range(4)]


def _attn_a_kernel(lam_ref, nw_ref, qt_ref, k_ref, vt_ref, o_ref, bd_ref, ab_ref, *, tq, slopes, lam_init):
    g = 2 * DA_HEADS
    q0 = pl.program_id(1) * tq
    nfull = q0 // CH
    bd_ref[...] = _block_diag_tile(qt_ref[0, 0], g)
    ab_ref[...] = _alibi_rows(slopes)

    def mask(j, aux, diag):
        if not diag:
            return None, aux
        rel = _iota((CH, tq), 0) - _iota((CH, tq), 1)
        return jnp.where(rel <= q0 - j * CH, 0.0, NEG_INF), aux

    l, acc = _attend(k_ref, vt_ref, bd_ref, ab_ref, _slope_row(slopes, 2, tq), DA_QK ** -0.5 * _LOG2E,
                     q0, nfull, tq, 2 * tq, mask, jnp.zeros((1, tq), _F32))

    lp = lam_ref[...]
    lam = (jnp.exp(jnp.sum(lp[0:1] * lp[1:2], axis=1, keepdims=True))
           - jnp.exp(jnp.sum(lp[2:3] * lp[3:4], axis=1, keepdims=True)) + lam_init)
    linv = 1.0 / l
    outs = []
    for h in range(DA_HEADS):
        a = acc[h] * linv[:, h * 2 * tq:(h + 1) * 2 * tq]
        o = a[:, :tq] - lam * a[:, tq:]
        ms = jnp.mean(o * o, axis=0, keepdims=True)
        outs.append(o * lax.rsqrt(ms + LN_EPS) * nw_ref[...] * (1.0 - lam_init))
    o_ref[0] = jnp.concatenate(outs, axis=0).T.astype(_BF16)


def _diff_attention(lam_p, norm_w, n16, t16, bn, s, slopes, lam_init):
    tq = TQ
    ns, per = s // CH, CH // tq
    g = 2 * DA_HEADS
    kern = functools.partial(_attn_a_kernel, tq=tq, slopes=slopes, lam_init=lam_init)
    return pl.pallas_call(
        kern,
        grid=(bn, s // tq),
        in_specs=[
            _const_spec(lam_p.shape), _const_spec(norm_w.shape),
            pl.BlockSpec((1, 1, 256, tq), lambda b, i: (b, i // per, 0, i % per)),
            pl.BlockSpec((1, s, 256), lambda b, i: (b, 0, 0)),
            pl.BlockSpec((1, ns, 256, CH), lambda b, i: (b, 0, 1, 0)),
        ],
        out_specs=pl.BlockSpec((1, tq, 256), lambda b, i: (b, i, 0)),
        out_shape=jax.ShapeDtypeStruct((bn, s, 256), _BF16),
        scratch_shapes=[pltpu.VMEM((256, g * tq), _BF16), pltpu.VMEM((DA_HEADS, CH, 128), _F32)],
        compiler_params=pltpu.CompilerParams(
            dimension_semantics=("parallel", "arbitrary"), vmem_limit_bytes=V7X_VMEM_LIMIT),
        name="diff_attn",
    )(lam_p, norm_w, t16, n16, t16)


def _dsa_kernel(iqt_ref, ik_ref, iwt_ref, qt_ref, k_ref, vt_ref, o_ref,
                iqbd_ref, bd_ref, key_ref, khi_ref, klo_ref, ab_ref, *, tq, slopes, topk):
    g = DSA_HEADS
    q0 = pl.program_id(1) * tq
    nfull = q0 // CH
    nch = nfull + 1
    ngrp = (nch + COUNT_UNROLL - 1) // COUNT_UNROLL
    i16 = jnp.int16
    iqbd_ref[...] = _block_diag_tile(iqt_ref[0, 0], IDX_HEADS)
    bd_ref[...] = _block_diag_tile(qt_ref[0, 0], g)
    ab_ref[...] = _alibi_rows(slopes)
    w = iwt_ref[0, 0][0:IDX_HEADS, :] * (IDX_HEADS ** -0.5 * IDX_HD ** -0.5)

    def logits(j):
        return _dot(ik_ref[0, pl.ds(pl.multiple_of(j * CH, CH), CH), :], iqbd_ref[...])

    def score(lg, j, diag):
        half = CH // 2
        for r0 in (0, half):
            sc = jnp.maximum(lg[r0:r0 + half, 0:tq], 0.0) * w[0:1]
            for h in range(1, IDX_HEADS):
                sc = sc + jnp.maximum(lg[r0:r0 + half, h * tq:(h + 1) * tq], 0.0) * w[h:h + 1]
            sc = jnp.where(sc == 0.0, 0.0, sc)
            bits = pltpu.bitcast(sc, jnp.int32)
            key = bits ^ ((bits >> 31) & 0x7FFFFFFF)
            if diag:
                rel = _iota(key.shape, 0) - _iota(key.shape, 1)
                key = jnp.where(rel <= q0 - j * CH - r0, key, _INT_MIN)
            key_ref[j, r0:r0 + half, :] = key
            khi_ref[j, r0:r0 + half, :] = (key >> 16).astype(i16)
            klo_ref[j, r0:r0 + half, :] = (key ^ 0x8000).astype(i16)

    def score_pair(i, lgs):
        nxt = (logits(2 * i + 2), logits(jnp.minimum(2 * i + 3, nfull)))
        score(lgs[0], 2 * i, False)
        score(lgs[1], 2 * i + 1, False)
        return nxt

    npair = nfull // 2
    lgs = lax.fori_loop(0, npair, score_pair, (logits(0), logits(jnp.minimum(1, nfull))))
    score(lgs[0], 2 * npair, True)

    @pl.when(nfull > 2 * npair)
    def _():
        score(lgs[1], nfull, True)

    for u in range(1, COUNT_UNROLL):
        @pl.when(nfull + u < ngrp * COUNT_UNROLL)
        def _():
            khi_ref[nfull + u] = jnp.full((CH, tq), _I16_MIN, i16)
            klo_ref[nfull + u] = jnp.full((CH, tq), _I16_MIN, i16)

    def count16(ref, cand):
        cand = cand.astype(i16)

        def cnt_body(gi, acc):
            parts = []
            for u in range(COUNT_UNROLL):
                ge = jnp.where(ref[gi * COUNT_UNROLL + u] >= cand, i16(1), i16(0)).reshape(CH // 16, 16, tq)
                parts += [ge[r] for r in range(CH // 16)]
            while len(parts) > 1:
                parts = [parts[i] + parts[i + 1] for i in range(0, len(parts), 2)]
            return acc + parts[0]
        acc = lax.fori_loop(0, ngrp, cnt_body, jnp.zeros((16, tq), i16))
        return jnp.sum(acc.astype(jnp.int32), axis=0, keepdims=True)

    def search16(ref, want):
        def body(b, c):
            thr, cnt_thr = c
            cand = thr + jnp.left_shift(jnp.int32(1), 15 - b)
            cnt = count16(ref, cand)
            ok = cnt >= want
            return jnp.where(ok, cand, thr), jnp.where(ok, cnt, cnt_thr)

        return lax.fori_loop(
            0, 16, body, (jnp.full((1, tq), _I16_MIN, jnp.int32), jnp.full((1, tq), 2 ** 30, jnp.int32)))

    thr_hi, _ = search16(khi_ref, topk)
    above = jnp.where(thr_hi == -_I16_MIN - 1, 0, count16(khi_ref, thr_hi + 1))
    hi16 = thr_hi.astype(i16)

    def keep_body(j, c):
        klo_ref[j] = jnp.where(khi_ref[j] == hi16, klo_ref[j], i16(_I16_MIN))
        return c

    lax.fori_loop(0, ngrp * COUNT_UNROLL, keep_body, 0)
    thr_lo, cnt_lo = search16(klo_ref, topk - above)
    thr = thr_hi * 65536 + (thr_lo - _I16_MIN)
    cnt_thr = above + cnt_lo
    need_tie = jnp.max(jnp.where((cnt_thr > topk) & (thr > _INT_MIN), 1, 0)) > 0

    def count_ge(cand):
        def cnt_body(j, acc8):
            ge = jnp.where(key_ref[j] >= cand, 1, 0)
            return acc8 + jnp.sum(ge.reshape(CH // 8, 8, tq), axis=0)
        acc8 = lax.fori_loop(0, nch, cnt_body, jnp.zeros((8, tq), jnp.int32))
        return jnp.sum(acc8, axis=0, keepdims=True)

    def attend(tie):
        if tie:
            room = (topk - count_ge(thr + 1)).astype(_F32)
            lower = jnp.where(_iota((CH, CH), 0) > _iota((CH, CH), 1), 1.0, 0.0).astype(_BF16)
        else:
            thr_eff = jnp.maximum(thr, _INT_MIN + 1)

        def mask(j, seen, diag):
            key = key_ref[j]
            if tie:
                eq = key == thr
                eqf = jnp.where(eq, 1.0, 0.0)
                rank = _dot(lower, eqf.astype(_BF16)) + seen
                sel = ((key > thr) | (eq & (rank < room))) & (key > _INT_MIN)
                seen = seen + jnp.sum(eqf, axis=0, keepdims=True)
            else:
                sel = key >= thr_eff
            return jnp.where(sel, 0.0, NEG_INF), seen

        l, acc = _attend(k_ref, vt_ref, bd_ref, ab_ref, _slope_row(slopes, 1, tq), DSA_HD ** -0.5 * _LOG2E,
                         q0, nfull, tq, tq, mask, jnp.zeros((1, tq), _F32))
        linv = 1.0 / l
        outs = [acc[h] * linv[:, h * tq:(h + 1) * tq] for h in range(g)]
        o_ref[0] = jnp.concatenate(outs, axis=0).T.astype(_BF16)

    @pl.when(need_tie)
    def _():
        attend(True)

    @pl.when(jnp.logical_not(need_tie))
    def _():
        attend(False)


def _dsa_attention(n16, t16, t32, bn, s, slopes):
    tq = TQ
    ns, per = s // CH, CH // tq
    g = DSA_HEADS
    topk = min(TOPK_MAX, s // 4)
    assert ns % COUNT_UNROLL == 0
    kern = functools.partial(_dsa_kernel, tq=tq, slopes=slopes, topk=topk)
    return pl.pallas_call(
        kern,
        grid=(bn, s // tq),
        in_specs=[
            pl.BlockSpec((1, 1, 128, tq), lambda b, i: (b, i // per, 10, i % per)),
            pl.BlockSpec((1, s, 128), lambda b, i: (b, 0, 10)),
            pl.BlockSpec((1, 1, 16, tq), lambda b, i: (b, i // per, 16, i % per)),
            pl.BlockSpec((1, 1, 256, tq), lambda b, i: (b, i // per, 2, i % per)),
            pl.BlockSpec((1, s, 256), lambda b, i: (b, 0, 1)),
            pl.BlockSpec((1, ns, 256, CH), lambda b, i: (b, 0, 3, 0)),
        ],
        out_specs=pl.BlockSpec((1, tq, 256), lambda b, i: (b, i, 0)),
        out_shape=jax.ShapeDtypeStruct((bn, s, 256), _BF16),
        scratch_shapes=[
            pltpu.VMEM((128, IDX_HEADS * tq), _BF16), pltpu.VMEM((256, g * tq), _BF16),
            pltpu.VMEM((ns, CH, tq), jnp.int32), pltpu.VMEM((ns, CH, tq), jnp.int16),
            pltpu.VMEM((ns, CH, tq), jnp.int16), pltpu.VMEM((DSA_HEADS, CH, 128), _F32),
        ],
        compiler_params=pltpu.CompilerParams(
            dimension_semantics=("parallel", "arbitrary"), vmem_limit_bytes=V7X_VMEM_LIMIT),
        name="dsa_attn",
    )(t16, n16, t32, t16, n16, t16)


def _ret_kernel(q_ref, kt_ref, v_ref, g_ref, intra_ref, qdec_ref, kdect_ref, cd_ref, nw_ref, o_ref, s_ref, *, c):
    @pl.when(pl.program_id(1) == 0)
    def _():
        s_ref[...] = jnp.zeros(s_ref.shape, _F32)

    q = q_ref[0]
    v = v_ref[0]
    kt = kt_ref[0, 0] * (RET_QK ** -0.5)
    att = _dot(q, _block_diag_tile(kt.astype(_BF16), RET_HEADS)) * intra_ref[...]
    vt = jnp.concatenate([v] * RET_HEADS, axis=0)
    vbd = jnp.where((_iota(vt.shape, 0) // c) == (_iota(vt.shape, 1) // RET_V), vt, jnp.zeros_like(vt))
    st = s_ref[...]
    o = _dot(att.astype(_BF16), vbd) + _dot(q, st.astype(_BF16)) * qdec_ref[...]
    upd = _dot((kt * kdect_ref[...]).astype(_BF16), v)
    same_head = (_iota(upd.shape, 0) // RET_QK) == (_iota(upd.shape, 1) // RET_V)
    s_ref[...] = st * cd_ref[...] + jnp.where(same_head, upd, 0.0)

    mu = _group_mean(o, RET_V)
    d = o - mu
    var = _group_mean(d * d, RET_V)
    y = d * lax.rsqrt(var + LN_EPS) * nw_ref[...]
    gate = g_ref[0]
    o_ref[0] = (gate * jax.nn.sigmoid(gate) * y).astype(_BF16)


def _retention_consts(c):
    h = RET_HEADS
    log_g = np.log1p(-np.power(2.0, -5.0 - np.arange(h, dtype=np.float64)))
    pos = np.arange(c, dtype=np.float64)
    rel = pos[:, None] - pos[None, :]
    intra = np.where(rel >= 0, np.exp(log_g[:, None, None] * np.maximum(rel, 0.0)), 0.0)
    intra = np.transpose(intra, (1, 0, 2)).reshape(c, h * c)
    qdec = np.repeat(np.exp(log_g[:, None] * (pos[None, :] + 1.0)).T, RET_V, axis=1)
    kdect = np.repeat(np.exp(log_g[:, None] * (c - 1.0 - pos[None, :])), RET_QK, axis=0)
    cd = np.repeat(np.exp(log_g * c), RET_QK)[:, None] * np.ones((1, h * RET_V))
    return tuple(jnp.asarray(a, _F32) for a in (intra, qdec, kdect, cd))


def _retention(norm_w, n16, n32, t32, bn, s):
    c = RET_C
    per = CH // c
    intra, qdec, kdect, cd = _retention_consts(c)
    return pl.pallas_call(
        functools.partial(_ret_kernel, c=c),
        grid=(bn, s // c),
        in_specs=[
            pl.BlockSpec((1, c, 256), lambda b, i: (b, i, 2)),
            pl.BlockSpec((1, 1, 256, c), lambda b, i: (b, i // per, 0, i % per)),
            pl.BlockSpec((1, c, 256), lambda b, i: (b, i, 3)),
            pl.BlockSpec((1, c, 256), lambda b, i: (b, i, 0)),
            _const_spec(intra.shape), _const_spec(qdec.shape), _const_spec(kdect.shape),
            _const_spec(cd.shape), _const_spec(norm_w.shape),
        ],
        out_specs=pl.BlockSpec((1, c, 256), lambda b, i: (b, i, 0)),
        out_shape=jax.ShapeDtypeStruct((bn, s, 256), _BF16),
        scratch_shapes=[pltpu.VMEM((RET_HEADS * RET_QK, RET_HEADS * RET_V), _F32)],
        compiler_params=pltpu.CompilerParams(dimension_semantics=("parallel", "arbitrary")),
        name="retention",
    )(n16, t32, n16, n32, intra, qdec, kdect, cd, norm_w)


def _gla_kernel(q_ref, k_ref, a_ref, v_ref, vt_ref, g_ref, wa_ref, ba_ref, nw_ref, o_ref,
                st_ref, u_ref, oacc_ref, qs_ref, kk_ref, b_ref, qh_ref, dec_ref, *, ts, sub):
    nsub = ts // sub
    hi = lax.Precision.HIGHEST

    @pl.when(pl.program_id(1) == 0)
    def _():
        st_ref[...] = jnp.zeros(st_ref.shape, _F32)

    la = jax.nn.log_sigmoid(_dot(a_ref[0].astype(_BF16), wa_ref[...]) + ba_ref[...]) * (1.0 / GLA_GATE_TEMP)
    rr, cc = _iota((ts, ts), 0), _iota((ts, ts), 1)
    same = (rr // sub) == (cc // sub)
    b = _dot(jnp.where(same & (cc <= rr), 1.0, 0.0), la, precision=hi)
    bl = _dot(jnp.where(same, 1.0, 0.0), la, precision=hi)
    qs = q_ref[0] * (GLA_QK ** -0.5)
    kk = k_ref[0]
    kd = (kk * jnp.exp(bl - b)).astype(_BF16)
    qs_ref[...] = qs
    kk_ref[...] = kk
    b_ref[...] = b
    qh_ref[...] = (qs * jnp.exp(b)).astype(_BF16)
    dec_ref[...] = jnp.exp(bl)
    vt = vt_ref[0, 0]
    row_blk = _iota(kd.shape, 0) // sub
    for n in range(nsub):
        u_ref[n] = _dot(vt, jnp.where(row_blk == n, kd, jnp.zeros_like(kd)))

    st_keep = (_iota(st_ref.shape, 0) // GLA_V) == (_iota(st_ref.shape, 1) // GLA_QK)
    spread = jnp.where((_iota((128, 256), 0) // GLA_QK) == (_iota((128, 256), 1) // GLA_V), 1.0, 0.0).astype(_BF16)
    row16 = _iota((sub, 128), 0)

    def body(n, carry):
        r0 = pl.multiple_of(n * sub, sub)
        st = st_ref[...]
        o_cross = _dot_nt(qh_ref[pl.ds(r0, sub), :], st.astype(_BF16))
        q16 = qs_ref[pl.ds(r0, sub), :]
        k16 = kk_ref[pl.ds(r0, sub), :]
        b16 = b_ref[pl.ds(r0, sub), :]
        v16 = v_ref[0, pl.ds(r0, sub), :].astype(_F32)
        es = []
        for j in range(sub):
            e = q16 * k16[j:j + 1] * jnp.exp(jnp.minimum(b16 - b16[j:j + 1], 0.0))
            es.append(jnp.where(row16 >= j, e, 0.0))
        e_all = jnp.concatenate(es, axis=0)
        e_hi = e_all.astype(_BF16)
        e_lo = (e_all - e_hi.astype(_F32)).astype(_BF16)
        att = _dot(e_hi, spread) + _dot(e_lo, spread)
        o_diag = att[0:sub] * v16[0:1]
        for j in range(1, sub):
            o_diag = o_diag + att[j * sub:(j + 1) * sub] * v16[j:j + 1]
        oacc_ref[pl.ds(r0, sub), :] = o_cross + o_diag
        st_ref[...] = st * dec_ref[pl.ds(r0, 1), :] + jnp.where(st_keep, u_ref[n], 0.0)
        return carry

    lax.fori_loop(0, nsub, body, 0)

    o = oacc_ref[...]
    ms = _group_mean(o * o, GLA_V)
    y = o * lax.rsqrt(ms + LN_EPS) * nw_ref[...]
    gate = g_ref[0]
    o_ref[0] = (gate * jax.nn.sigmoid(gate) * y).astype(_BF16)


def _gla(wa, ba, norm_w, n16, n32, t16, bn, s):
    ts, sub = CH, GLA_SUB
    return pl.pallas_call(
        functools.partial(_gla_kernel, ts=ts, sub=sub),
        grid=(bn, s // ts),
        in_specs=[
            pl.BlockSpec((1, ts, 128), lambda b, i: (b, i, 4)),
            pl.BlockSpec((1, ts, 128), lambda b, i: (b, i, 5)),
            pl.BlockSpec((1, ts, 128), lambda b, i: (b, i, 6)),
            pl.BlockSpec((1, ts, 256), lambda b, i: (b, i, 4)),
            pl.BlockSpec((1, 1, 256, ts), lambda b, i: (b, i, 4, 0)),
            pl.BlockSpec((1, ts, 256), lambda b, i: (b, i, 1)),
            _const_spec(wa.shape), _const_spec(ba.shape), _const_spec(norm_w.shape),
        ],
        out_specs=pl.BlockSpec((1, ts, 256), lambda b, i: (b, i, 0)),
        out_shape=jax.ShapeDtypeStruct((bn, s, 256), _BF16),
        scratch_shapes=[
            pltpu.VMEM((GLA_HEADS * GLA_V, GLA_HEADS * GLA_QK), _F32),
            pltpu.VMEM((ts // sub, GLA_HEADS * GLA_V, GLA_HEADS * GLA_QK), _F32),
            pltpu.VMEM((ts, 256), _F32),
            pltpu.VMEM((ts, 128), _F32), pltpu.VMEM((ts, 128), _F32), pltpu.VMEM((ts, 128), _F32),
            pltpu.VMEM((ts, 128), _BF16), pltpu.VMEM((ts, 128), _F32),
        ],
        compiler_params=pltpu.CompilerParams(dimension_semantics=("parallel", "arbitrary")),
        name="gla",
    )(n32, n32, n32, n16, t16, n32, wa, ba, norm_w)


def _merge_kernel(x_ref, ya_ref, yb_ref, yc_ref, yd_ref, wg_ref, wbr_ref, wout_ref, lnw_ref, lnb_ref, h_ref):
    x = x_ref[...]
    xb = x.astype(_BF16)
    merged = None
    for n, y_ref in enumerate((ya_ref, yb_ref, yc_ref, yd_ref)):
        gate = jax.nn.sigmoid(_dot(xb, wg_ref[:, n * D_MODEL:(n + 1) * D_MODEL]))
        term = gate * _dot(y_ref[...], wbr_ref[n])
        merged = term if merged is None else merged + term
    mix = _dot(merged.astype(_BF16), wout_ref[...])
    h_ref[...] = _layer_norm(DEEPNORM_ALPHA * x + mix, lnw_ref[...], lnb_ref[...])


def _merge(x2, ys, wg, wbr, wout, lnw, lnb):
    t = x2.shape[0]
    ts = TS_DENSE
    tok = lambda w: pl.BlockSpec((ts, w), lambda i: (i, 0))
    return pl.pallas_call(
        _merge_kernel,
        grid=(t // ts,),
        in_specs=[tok(D_MODEL)] + [tok(BRANCH_W)] * 4 + [
            _const_spec(wg.shape), _const_spec(wbr.shape), _const_spec(wout.shape),
            _const_spec(lnw.shape), _const_spec(lnb.shape)],
        out_specs=tok(D_MODEL),
        out_shape=jax.ShapeDtypeStruct((t, D_MODEL), _F32),
        compiler_params=pltpu.CompilerParams(
            dimension_semantics=("parallel",), vmem_limit_bytes=V7X_VMEM_LIMIT),
        name="merge_ln",
    )(x2, *ys, wg, wbr, wout, lnw, lnb)


def _ffn_kernel(h_ref, p_ref, wup_ref, cw_ref, cb_ref, wdn_ref, wpg_ref, wpp_ref, lnw_ref, lnb_ref,
                o_ref, tail_ref, *, ts):
    @pl.when(pl.program_id(1) == 0)
    def _():
        tail_ref[...] = jnp.zeros(tail_ref.shape, _F32)

    h = h_ref[...]
    hb = h.astype(_BF16)
    row = _iota((ts, FF_COLS), 0)
    f = None
    for c0 in range(0, D_FF, FF_COLS):
        cols = slice(c0, c0 + FF_COLS)
        u = _dot(hb, wup_ref[:, cols])
        gt = _dot(hb, wup_ref[:, D_FF + c0:D_FF + c0 + FF_COLS])
        prev = tail_ref[:, cols]
        g1 = jnp.where(row == 0, prev[7:8], pltpu.roll(gt, 1, 0))
        g2 = jnp.where(row == 0, prev[6:7], jnp.where(row == 1, prev[7:8], pltpu.roll(gt, 2, 0)))
        tail_ref[:, cols] = gt[ts - 8:ts]
        gc = cb_ref[:, cols] + cw_ref[0:1, cols] * g2
        gc = gc + cw_ref[1:2, cols] * g1
        gc = gc + cw_ref[2:3, cols] * gt
        term = _dot((jax.nn.gelu(gc) * u).astype(_BF16), wdn_ref[cols, :])
        f = term if f is None else f + term
    e = jax.nn.sigmoid(_dot(hb, wpg_ref[...])) * _dot(p_ref[...].astype(_BF16), wpp_ref[...])
    o_ref[...] = _layer_norm(DEEPNORM_ALPHA * h + f + e, lnw_ref[...], lnb_ref[...])


def _ffn(h2, p2, wup, cw, cb, wdn, wpg, wpp, lnw, lnb, bn, s):
    ts = TS_DENSE
    ns = s // ts
    tok = lambda w: pl.BlockSpec((ts, w), lambda b, i: (b * ns + i, 0))
    return pl.pallas_call(
        functools.partial(_ffn_kernel, ts=ts),
        grid=(bn, ns),
        in_specs=[tok(D_MODEL), tok(P_DIM)] + [_const_spec(a.shape) for a in (wup, cw, cb, wdn, wpg, wpp, lnw, lnb)],
        out_specs=tok(D_MODEL),
        out_shape=jax.ShapeDtypeStruct((bn * s, D_MODEL), _F32),
        scratch_shapes=[pltpu.VMEM((8, D_FF), _F32)],
        compiler_params=pltpu.CompilerParams(
            dimension_semantics=("parallel", "arbitrary"), vmem_limit_bytes=V7X_VMEM_LIMIT),
        name="ffn_ple_ln",
    )(h2, p2, wup, cw, cb, wdn, wpg, wpp, lnw, lnb)


def _projection_weights(w):
    offs = np.concatenate([[0], np.cumsum(IN_SIZES)])
    col = {n: w[:, offs[i]:offs[i + 1]] for i, n in enumerate(IN_NAMES)}
    pad = lambda a, n: jnp.pad(a, ((0, 0), (0, n - a.shape[1])))
    wn32 = jnp.concatenate([col["c_g"], col["d_g"], col["d_q"], col["d_k"], pad(col["d_a"], 128)], axis=1)
    wn16 = jnp.concatenate([col["a_k"], col["b_k"], col["c_q"], col["c_v"], col["d_v"]]
                           + [col["b_ik"]] * IDX_HEADS, axis=1)
    wt16 = jnp.concatenate([col["a_q"], col["a_v"], col["b_q"], col["b_v"], col["d_v"], col["b_iq"]], axis=1)
    wt32 = jnp.concatenate([col["c_k"], pad(col["b_iw"], 16)], axis=1)
    return [a.astype(_BF16) for a in (wn32, wn16, wt16, wt32)], col["m_g"].astype(_BF16)


def kernel(x, p, w_in, a_lambda, a_norm_w, ret_norm_w, gla_w_a2, gla_b_a, gla_norm_w, w_branch, w_out,
           ln1_w, ln1_b, w_ffn_up, ffn_conv_w, ffn_conv_b, w_ffn_down, w_ple_gate, w_ple_proj, ln2_w, ln2_b):
    bn, s, _ = x.shape
    t = bn * s
    slopes = [2.0 ** (-(8.0 / N_SOFTMAX_HEADS) * i) for i in range(1, N_SOFTMAX_HEADS + 1)]
    slopes_a, slopes_b = tuple(slopes[0::2]), tuple(slopes[1::2])
    row = lambda v: v.astype(_F32).reshape(1, -1)
    x2 = x.reshape(t, D_MODEL)
    for i in range(DEPTH):
        proj_w, wg = _projection_weights(w_in[i])
        n32, n16, t16, t32 = _project(x2, *proj_w, bn, s)
        n32 = n32.reshape(bn, s, -1)
        n16 = n16.reshape(bn, s, -1)
        lam_init = 0.8 - 0.6 * math.exp(-0.3 * i)
        y_a = _diff_attention(a_lambda[i].astype(_F32), a_norm_w[i].astype(_F32).reshape(DA_V, 1),
                              n16, t16, bn, s, slopes_a, lam_init)
        y_b = _dsa_attention(n16, t16, t32, bn, s, slopes_b)
        y_c = _retention(row(ret_norm_w[i]), n16, n32, t32, bn, s)
        wa = jnp.pad(gla_w_a2[i], ((0, 128 - GLA_RANK), (0, 0))).astype(_BF16)
        y_d = _gla(wa, row(gla_b_a[i]), row(jnp.tile(gla_norm_w[i], GLA_HEADS)), n16, n32, t16, bn, s)
        ys = [y.reshape(t, BRANCH_W) for y in (y_a, y_b, y_c, y_d)]
        h2 = _merge(x2, ys, wg, w_branch[i].astype(_BF16), w_out[i].astype(_BF16), row(ln1_w[i]), row(ln1_b[i]))
        x2 = _ffn(h2, p[i].reshape(t, P_DIM), w_ffn_up[i].astype(_BF16), ffn_conv_w[i].astype(_F32),
                  row(ffn_conv_b[i]), w_ffn_down[i].astype(_BF16), w_ple_gate[i].astype(_BF16),
                  w_ple_proj[i].astype(_BF16), row(ln2_w[i]), row(ln2_b[i]), bn, s)
    return x2.reshape(bn, s, D_MODEL)
```

```python
import functools
import math

import numpy as np
import jax
import jax.numpy as jnp
from jax import lax
from jax.experimental import pallas as pl
from jax.experimental.pallas import tpu as pltpu

D_MODEL = 1024
DEPTH = 2
P_DIM = 256
N_BRANCH = 4
BRANCH_W = 256
DA_HEADS = 4
DA_QK = 32
DA_V = 64
DSA_HEADS = 4
DSA_HD = 64
IDX_HEADS = 4
IDX_HD = 32
TOPK_MAX = 256
RET_HEADS = 4
RET_QK = 64
RET_V = 64
GLA_HEADS = 4
GLA_QK = 32
GLA_V = 64
GLA_RANK = 16
GLA_GATE_TEMP = 16.0
D_FF = 2816
CONV_W = 3
N_SOFTMAX_HEADS = DA_HEADS + DSA_HEADS
LN_EPS = 1e-5
NEG_INF = -1e30
DEEPNORM_ALPHA = (2.0 * DEPTH) ** 0.25

IN_SIZES = (256, 256, 256, 256, 256, 256, 128, 32, 4, 256, 256, 256, 256, 128, 128, 256, 16, 256, 4096)
IN_NAMES = ("a_q", "a_k", "a_v", "b_q", "b_k", "b_v", "b_iq", "b_ik", "b_iw",
            "c_q", "c_k", "c_v", "c_g", "d_q", "d_k", "d_v", "d_a", "d_g", "m_g")

_BF16 = jnp.bfloat16
_F32 = jnp.float32
_INT_MIN = -2 ** 31
_LOG2E = math.log2(math.e)

CH = 256
TQ = 128
RET_C = 128
GLA_SUB = 16
COUNT_UNROLL = 4
TS_DENSE = 256
FF_COLS = 256
PREP_ROWS = 256
V7X_VMEM_LIMIT = 56 * 1024 * 1024


def _dot(a, b, precision=None):
    return jnp.dot(a, b, preferred_element_type=_F32, precision=precision)


def _dot_nt(a, b):
    return lax.dot_general(a, b, (((1,), (1,)), ((), ())), preferred_element_type=_F32)


def _iota(shape, dim):
    return lax.broadcasted_iota(jnp.int32, shape, dim)


def _block_diag_tile(m_t, ngroups):
    r, tq = m_t.shape
    tiled = jnp.concatenate([m_t] * ngroups, axis=1)
    keep = (_iota(tiled.shape, 0) // (r // ngroups)) == (_iota(tiled.shape, 1) // tq)
    return jnp.where(keep, tiled, jnp.zeros_like(tiled))


def _layer_norm(x, w, b):
    mu = jnp.mean(x, -1, keepdims=True)
    var = jnp.mean(jnp.square(x - mu), -1, keepdims=True)
    return (x - mu) * lax.rsqrt(var + LN_EPS) * w + b


def _group_mean(x, group):
    lane_g = _iota(x.shape, 1) // group
    out = jnp.zeros_like(x)
    for h in range(x.shape[1] // group):
        mk = lane_g == h
        mh = jnp.sum(jnp.where(mk, x, 0.0), axis=1, keepdims=True) * (1.0 / group)
        out = jnp.where(mk, mh, out)
    return out


def _const_spec(shape):
    nd = len(shape)
    return pl.BlockSpec(shape, lambda *_: (0,) * nd, pipeline_mode=pl.Buffered(1))


def _prep_kernel(w_ref, wn32_ref, wn16_ref, wt16_ref, wt32_ref, wg_ref):
    offs = [0] + np.cumsum(IN_SIZES).tolist()
    col = {n: w_ref[0, :, offs[i]:offs[i + 1]] for i, n in enumerate(IN_NAMES)}
    zeros = lambda n: jnp.zeros((w_ref.shape[1], n), _F32)
    cat = lambda xs: jnp.concatenate(xs, axis=1)
    wn32_ref[...] = cat([col["c_g"], col["d_g"], col["d_q"], col["d_k"], col["d_a"],
                         zeros(128 - GLA_RANK)]).astype(_BF16)
    wn16_ref[...] = cat([col["a_k"], col["b_k"], col["c_q"], col["c_v"], col["d_v"]]
                        + [col["b_ik"]] * IDX_HEADS).astype(_BF16)
    wt16_ref[...] = cat([col["a_q"], col["a_v"], col["b_q"], col["b_v"], col["d_v"], col["b_iq"]]).T.astype(_BF16)
    wt32_ref[...] = cat([col["c_k"], col["b_iw"], zeros(16 - IDX_HEADS)]).T.astype(_BF16)
    wg_ref[...] = col["m_g"].astype(_BF16)


def _prep_weights(w_in, layer):
    rb = PREP_ROWS
    n32, n16, t16, t32, ng = 896, 1408, 1408, 272, N_BRANCH * D_MODEL
    rows = lambda n: pl.BlockSpec((rb, n), lambda r: (r, 0))
    cols = lambda n: pl.BlockSpec((n, rb), lambda r: (0, r))
    return pl.pallas_call(
        _prep_kernel,
        grid=(D_MODEL // rb,),
        in_specs=[pl.BlockSpec((1, rb, w_in.shape[2]), lambda r: (layer, r, 0))],
        out_specs=[rows(n32), rows(n16), cols(t16), cols(t32), rows(ng)],
        out_shape=[jax.ShapeDtypeStruct((D_MODEL, n32), _BF16), jax.ShapeDtypeStruct((D_MODEL, n16), _BF16),
                   jax.ShapeDtypeStruct((t16, D_MODEL), _BF16), jax.ShapeDtypeStruct((t32, D_MODEL), _BF16),
                   jax.ShapeDtypeStruct((D_MODEL, ng), _BF16)],
        compiler_params=pltpu.CompilerParams(
            dimension_semantics=("parallel",), vmem_limit_bytes=V7X_VMEM_LIMIT),
        name="prep_weights",
    )(w_in)


def _proj_kernel(x_ref, wn32_ref, wn16_ref, wt16_ref, wt32_ref, n32_ref, n16_ref, t16_ref, t32_ref):
    x = x_ref[...].astype(_BF16)
    n32_ref[...] = _dot(x, wn32_ref[...])
    n16_ref[...] = _dot(x, wn16_ref[...]).astype(_BF16)
    t16_ref[0, 0] = _dot_nt(wt16_ref[...], x).astype(_BF16)
    t32_ref[0, 0] = _dot_nt(wt32_ref[...], x)


def _project(x2, wn32, wn16, wt16, wt32, bn, s):
    ns = s // CH
    n32, n16, t16, t32 = wn32.shape[1], wn16.shape[1], wt16.shape[0], wt32.shape[0]
    return pl.pallas_call(
        _proj_kernel,
        grid=(bn, ns),
        in_specs=[
            pl.BlockSpec((CH, D_MODEL), lambda b, i: (b * ns + i, 0)),
            _const_spec(wn32.shape), _const_spec(wn16.shape), _const_spec(wt16.shape), _const_spec(wt32.shape),
        ],
        out_specs=[
            pl.BlockSpec((CH, n32), lambda b, i: (b * ns + i, 0)),
            pl.BlockSpec((CH, n16), lambda b, i: (b * ns + i, 0)),
            pl.BlockSpec((1, 1, t16, CH), lambda b, i: (b, i, 0, 0)),
            pl.BlockSpec((1, 1, t32, CH), lambda b, i: (b, i, 0, 0)),
        ],
        out_shape=[
            jax.ShapeDtypeStruct((bn * s, n32), _F32),
            jax.ShapeDtypeStruct((bn * s, n16), _BF16),
            jax.ShapeDtypeStruct((bn, ns, t16, CH), _BF16),
            jax.ShapeDtypeStruct((bn, ns, t32, CH), _F32),
        ],
        compiler_params=pltpu.CompilerParams(
            dimension_semantics=("parallel", "parallel"), vmem_limit_bytes=V7X_VMEM_LIMIT),
        name="proj",
    )(x2, wn32, wn16, wt16, wt32)


def _slope_row(slopes, reps, tq):
    return jnp.concatenate([jnp.full((1, tq), s * _LOG2E, _F32) for s in slopes for _ in range(reps)], axis=1)


def _alibi_rows(slopes):
    rows = _iota((CH, 128), 0).astype(_F32)
    return jnp.stack([rows * (s * _LOG2E) for s in slopes], axis=0)


def _attend(k_ref, vt_ref, bd_ref, ab_ref, slope_row, c1, q0, nfull, tq, w, mask_fn, aux0):
    g_tq = bd_ref.shape[1]

    def pv(j, p):
        vt_c = vt_ref[0, j]
        return [_dot(vt_c[h * 64:(h + 1) * 64, :], p[:, h * w:(h + 1) * w]) for h in range(4)]

    def qk(j):
        kc = k_ref[0, pl.ds(pl.multiple_of(j * CH, CH), CH), :]
        return [_dot(kc, bd_ref[:, c0:c0 + 256]) for c0 in range(0, g_tq, 256)]

    def softmax(s, j, m, l, aux, diag):
        crow = slope_row * (j * CH - q0).astype(_F32)
        amask, aux = mask_fn(j, aux, diag)
        ps, ms, ls, alphas = [], [], [], []
        for c0 in range(0, g_tq, 128):
            cols = slice(c0, c0 + 128)
            t = s[c0 // 256][:, c0 % 256:c0 % 256 + 128] * c1 + ab_ref[c0 // w]
            if amask is not None:
                t = t + amask[:, c0 % tq:c0 % tq + 128]
            m_new = jnp.maximum(m[:, cols], jnp.max(t, axis=0, keepdims=True) + crow[:, cols])
            alpha = jnp.exp2(m[:, cols] - m_new)
            p = jnp.exp2(t - (m_new - crow[:, cols]))
            ls.append(alpha * l[:, cols] + jnp.sum(p, axis=0, keepdims=True))
            ps.append(p.astype(_BF16))
            ms.append(m_new)
            alphas.append(alpha)
        cat = lambda xs: jnp.concatenate(xs, axis=1)
        return cat(ms), cat(ls), cat(alphas), cat(ps), aux

    def rescale(acc, pvs, alpha):
        return [alpha[:, h * w:(h + 1) * w] * (acc[h] + pvs[h]) for h in range(4)]

    def step(s, j, carry, diag=False):
        p, m, l, acc, aux = carry
        pvs = pv(jnp.maximum(j - 1, 0), p)
        m, l, alpha, p, aux = softmax(s, j, m, l, aux, diag)
        return p, m, l, rescale(acc, pvs, alpha), aux

    def pair(i, carry):
        s_a, s_b = qk(2 * i), qk(2 * i + 1)
        return step(s_b, 2 * i + 1, step(s_a, 2 * i, carry))

    init = (jnp.zeros((CH, g_tq), _BF16), jnp.full((1, g_tq), NEG_INF, _F32),
            jnp.zeros((1, g_tq), _F32), [jnp.zeros((64, w), _F32) for _ in range(4)], aux0)
    npair = nfull // 2
    carry = lax.fori_loop(0, npair, pair, init)
    carry = lax.fori_loop(2 * npair, nfull, lambda j, c: step(qk(j), j, c), carry)
    p, m, l, acc, aux = step(qk(nfull), nfull, carry, True)
    pvs = pv(nfull, p)
    return l, [acc[h] + pvs[h] for h in range(4)]


def _attn_a_kernel(lam_ref, nw_ref, qt_ref, k_ref, vt_ref, o_ref, bd_ref, ab_ref, *, tq, slopes, lam_init):
    g = 2 * DA_HEADS
    q0 = pl.program_id(1) * tq
    nfull = q0 // CH
    bd_ref[...] = _block_diag_tile(qt_ref[0, 0], g)
    ab_ref[...] = _alibi_rows(slopes)

    def mask(j, aux, diag):
        if not diag:
            return None, aux
        rel = _iota((CH, tq), 0) - _iota((CH, tq), 1)
        return jnp.where(rel <= q0 - j * CH, 0.0, NEG_INF), aux

    l, acc = _attend(k_ref, vt_ref, bd_ref, ab_ref, _slope_row(slopes, 2, tq), DA_QK ** -0.5 * _LOG2E,
                     q0, nfull, tq, 2 * tq, mask, jnp.zeros((1, tq), _F32))

    lp = lam_ref[...]
    lam = (jnp.exp(jnp.sum(lp[0:1] * lp[1:2], axis=1, keepdims=True))
           - jnp.exp(jnp.sum(lp[2:3] * lp[3:4], axis=1, keepdims=True)) + lam_init)
    linv = 1.0 / l
    outs = []
    for h in range(DA_HEADS):
        a = acc[h] * linv[:, h * 2 * tq:(h + 1) * 2 * tq]
        o = a[:, :tq] - lam * a[:, tq:]
        ms = jnp.mean(o * o, axis=0, keepdims=True)
        outs.append(o * lax.rsqrt(ms + LN_EPS) * nw_ref[...] * (1.0 - lam_init))
    o_ref[0] = jnp.concatenate(outs, axis=0).T.astype(_BF16)


def _diff_attention(lam_p, norm_w, n16, t16, bn, s, slopes, lam_init):
    tq = TQ
    ns, per = s // CH, CH // tq
    g = 2 * DA_HEADS
    kern = functools.partial(_attn_a_kernel, tq=tq, slopes=slopes, lam_init=lam_init)
    return pl.pallas_call(
        kern,
        grid=(bn, s // tq),
        in_specs=[
            _const_spec(lam_p.shape), _const_spec(norm_w.shape),
            pl.BlockSpec((1, 1, 256, tq), lambda b, i: (b, i // per, 0, i % per)),
            pl.BlockSpec((1, s, 256), lambda b, i: (b, 0, 0)),
            pl.BlockSpec((1, ns, 256, CH), lambda b, i: (b, 0, 1, 0)),
        ],
        out_specs=pl.BlockSpec((1, tq, 256), lambda b, i: (b, i, 0)),
        out_shape=jax.ShapeDtypeStruct((bn, s, 256), _BF16),
        scratch_shapes=[pltpu.VMEM((256, g * tq), _BF16), pltpu.VMEM((DA_HEADS, CH, 128), _F32)],
        compiler_params=pltpu.CompilerParams(
            dimension_semantics=("parallel", "arbitrary"), vmem_limit_bytes=V7X_VMEM_LIMIT),
        name="diff_attn",
    )(lam_p, norm_w, t16, n16, t16)


def _bit_planes(rows):
    a = list(rows)
    j, m = 16, 0x0000FFFF
    while j:
        k = 0
        while k < 32:
            t = (a[k] ^ lax.shift_right_logical(a[k + j], jnp.int32(j))) & jnp.int32(m)
            a[k] = a[k] ^ t
            a[k + j] = a[k + j] ^ (t << j)
            k = (k + j + 1) & ~j
        j >>= 1
        if j:
            m = (m ^ (m << j)) & 0xFFFFFFFF
            m = m - (1 << 32) if m >= (1 << 31) else m
    return a[::-1]


def _dsa_kernel(iqt_ref, ik_ref, iwt_ref, qt_ref, k_ref, vt_ref, o_ref,
                iqbd_ref, bd_ref, key_ref, planes_ref, alive_ref, ab_ref, *, tq, slopes, topk):
    g = DSA_HEADS
    q0 = pl.program_id(1) * tq
    nfull = q0 // CH
    ngrp = (nfull + COUNT_UNROLL) // COUNT_UNROLL
    iqbd_ref[...] = _block_diag_tile(iqt_ref[0, 0], IDX_HEADS)
    bd_ref[...] = _block_diag_tile(qt_ref[0, 0], g)
    ab_ref[...] = _alibi_rows(slopes)
    w = iwt_ref[0, 0][0:IDX_HEADS, :] * (IDX_HEADS ** -0.5 * IDX_HD ** -0.5)

    def logits(j):
        return _dot(ik_ref[0, pl.ds(pl.multiple_of(j * CH, CH), CH), :], iqbd_ref[...])

    def score(lg, j, diag):
        half = CH // 2
        rows = []
        for r0 in (0, half):
            sc = jnp.maximum(lg[r0:r0 + half, 0:tq], 0.0) * w[0:1]
            for h in range(1, IDX_HEADS):
                sc = sc + jnp.maximum(lg[r0:r0 + half, h * tq:(h + 1) * tq], 0.0) * w[h:h + 1]
            sc = jnp.where(sc == 0.0, 0.0, sc)
            bits = pltpu.bitcast(sc, jnp.int32)
            key = bits ^ ((bits >> 31) & 0x7FFFFFFF)
            if diag:
                rel = _iota(key.shape, 0) - _iota(key.shape, 1)
                key = jnp.where(rel <= q0 - j * CH - r0, key, _INT_MIN)
            key_ref[j, r0:r0 + half, :] = key
            key3 = key.reshape(half // 8, 8, tq)
            rows += [key3[v] for v in range(half // 8)]
        planes = _bit_planes(rows)
        planes[31] = ~planes[31]
        planes_ref[j] = jnp.stack(planes, axis=0)
        if diag:
            lim = q0 - j * CH + _iota((8, tq), 1) - _iota((8, tq), 0)
            nbits = jnp.clip((lim >> 3) + 1, 0, 32)
            alive_ref[j] = jnp.where(nbits == 0, 0, jnp.left_shift(jnp.int32(-1), 32 - jnp.maximum(nbits, 1)))
        else:
            alive_ref[j] = jnp.full((8, tq), -1, jnp.int32)

    def score_pair(i, lgs):
        nxt = (logits(2 * i + 2), logits(jnp.minimum(2 * i + 3, nfull)))
        score(lgs[0], 2 * i, False)
        score(lgs[1], 2 * i + 1, False)
        return nxt

    npair = nfull // 2
    lgs = lax.fori_loop(0, npair, score_pair, (logits(0), logits(jnp.minimum(1, nfull))))
    score(lgs[0], 2 * npair, True)

    @pl.when(nfull > 2 * npair)
    def _():
        score(lgs[1], nfull, True)

    for u in range(1, COUNT_UNROLL):
        @pl.when(nfull + u < ngrp * COUNT_UNROLL)
        def _():
            planes_ref[nfull + u] = jnp.zeros((32, 8, tq), jnp.int32)
            alive_ref[nfull + u] = jnp.zeros((8, tq), jnp.int32)

    def sweep(b_upd, keep, b_cnt):
        def body(gi, acc8):
            for u in range(COUNT_UNROLL):
                j = gi * COUNT_UNROLL + u
                a = alive_ref[j]
                if b_upd is not None:
                    a = a & ~(planes_ref[j, b_upd] ^ keep)
                    alive_ref[j] = a
                acc8 = acc8 + lax.population_count(a if b_cnt is None else a & planes_ref[j, b_cnt])
            return acc8
        acc8 = lax.fori_loop(0, ngrp, body, jnp.zeros((8, tq), jnp.int32))
        return jnp.sum(acc8, axis=0, keepdims=True)

    def decide(b, ones, want, thr):
        take = ones >= want
        thr = jnp.where(take, thr | jnp.left_shift(jnp.int32(1), b), thr)
        return jnp.where(take, want, want - ones), thr, jnp.where(take, -1, 0)

    want, thr, keep = decide(31, sweep(None, None, 31), jnp.full((1, tq), topk, jnp.int32),
                             jnp.zeros((1, tq), jnp.int32))

    def bit_body(i, c):
        want, thr, keep = c
        b = 30 - i
        return decide(b, sweep(b + 1, keep, b), want, thr)

    want, thr, keep = lax.fori_loop(0, 31, bit_body, (want, thr, keep))
    ties = sweep(0, keep, None)
    thr = thr ^ _INT_MIN
    need_tie = jnp.max(jnp.where((ties > want) & (thr > _INT_MIN), 1, 0)) > 0

    def attend(tie):
        if tie:
            room = want.astype(_F32)
            lower = jnp.where(_iota((CH, CH), 0) > _iota((CH, CH), 1), 1.0, 0.0).astype(_BF16)
        else:
            thr_eff = jnp.maximum(thr, _INT_MIN + 1)

        def mask(j, seen, diag):
            key = key_ref[j]
            if tie:
                eq = key == thr
                eqf = jnp.where(eq, 1.0, 0.0)
                rank = _dot(lower, eqf.astype(_BF16)) + seen
                sel = ((key > thr) | (eq & (rank < room))) & (key > _INT_MIN)
                seen = seen + jnp.sum(eqf, axis=0, keepdims=True)
            else:
                sel = key >= thr_eff
            return jnp.where(sel, 0.0, NEG_INF), seen

        l, acc = _attend(k_ref, vt_ref, bd_ref, ab_ref, _slope_row(slopes, 1, tq), DSA_HD ** -0.5 * _LOG2E,
                         q0, nfull, tq, tq, mask, jnp.zeros((1, tq), _F32))
        linv = 1.0 / l
        outs = [acc[h] * linv[:, h * tq:(h + 1) * tq] for h in range(g)]
        o_ref[0] = jnp.concatenate(outs, axis=0).T.astype(_BF16)

    @pl.when(need_tie)
    def _():
        attend(True)

    @pl.when(jnp.logical_not(need_tie))
    def _():
        attend(False)


def _dsa_attention(n16, t16, t32, bn, s, slopes):
    tq = TQ
    ns, per = s // CH, CH // tq
    g = DSA_HEADS
    topk = min(TOPK_MAX, s // 4)
    assert ns % COUNT_UNROLL == 0
    kern = functools.partial(_dsa_kernel, tq=tq, slopes=slopes, topk=topk)
    return pl.pallas_call(
        kern,
        grid=(bn, s // tq),
        in_specs=[
            pl.BlockSpec((1, 1, 128, tq), lambda b, i: (b, i // per, 10, i % per)),
            pl.BlockSpec((1, s, 128), lambda b, i: (b, 0, 10)),
            pl.BlockSpec((1, 1, 16, tq), lambda b, i: (b, i // per, 16, i % per)),
            pl.BlockSpec((1, 1, 256, tq), lambda b, i: (b, i // per, 2, i % per)),
            pl.BlockSpec((1, s, 256), lambda b, i: (b, 0, 1)),
            pl.BlockSpec((1, ns, 256, CH), lambda b, i: (b, 0, 3, 0)),
        ],
        out_specs=pl.BlockSpec((1, tq, 256), lambda b, i: (b, i, 0)),
        out_shape=jax.ShapeDtypeStruct((bn, s, 256), _BF16),
        scratch_shapes=[
            pltpu.VMEM((128, IDX_HEADS * tq), _BF16), pltpu.VMEM((256, g * tq), _BF16),
            pltpu.VMEM((ns, CH, tq), jnp.int32), pltpu.VMEM((ns, 32, 8, tq), jnp.int32),
            pltpu.VMEM((ns, 8, tq), jnp.int32), pltpu.VMEM((DSA_HEADS, CH, 128), _F32),
        ],
        compiler_params=pltpu.CompilerParams(
            dimension_semantics=("parallel", "arbitrary"), vmem_limit_bytes=V7X_VMEM_LIMIT),
        name="dsa_attn",
    )(t16, n16, t32, t16, n16, t16)


def _ret_kernel(q_ref, kt_ref, v_ref, g_ref, intra_ref, qdec_ref, kdect_ref, cd_ref, nw_ref, o_ref, s_ref, *, c):
    @pl.when(pl.program_id(1) == 0)
    def _():
        s_ref[...] = jnp.zeros(s_ref.shape, _F32)

    q = q_ref[0]
    v = v_ref[0]
    kt = kt_ref[0, 0] * (RET_QK ** -0.5)
    att = _dot(q, _block_diag_tile(kt.astype(_BF16), RET_HEADS)) * intra_ref[...]
    vt = jnp.concatenate([v] * RET_HEADS, axis=0)
    vbd = jnp.where((_iota(vt.shape, 0) // c) == (_iota(vt.shape, 1) // RET_V), vt, jnp.zeros_like(vt))
    st = s_ref[...]
    o = _dot(att.astype(_BF16), vbd) + _dot(q, st.astype(_BF16)) * qdec_ref[...]
    upd = _dot((kt * kdect_ref[...]).astype(_BF16), v)
    same_head = (_iota(upd.shape, 0) // RET_QK) == (_iota(upd.shape, 1) // RET_V)
    s_ref[...] = st * cd_ref[...] + jnp.where(same_head, upd, 0.0)

    mu = _group_mean(o, RET_V)
    d = o - mu
    var = _group_mean(d * d, RET_V)
    y = d * lax.rsqrt(var + LN_EPS) * nw_ref[...]
    gate = g_ref[0]
    o_ref[0] = (gate * jax.nn.sigmoid(gate) * y).astype(_BF16)


def _retention_consts(c):
    h = RET_HEADS
    log_g = np.log1p(-np.power(2.0, -5.0 - np.arange(h, dtype=np.float64)))
    pos = np.arange(c, dtype=np.float64)
    rel = pos[:, None] - pos[None, :]
    intra = np.where(rel >= 0, np.exp(log_g[:, None, None] * np.maximum(rel, 0.0)), 0.0)
    intra = np.transpose(intra, (1, 0, 2)).reshape(c, h * c)
    qdec = np.repeat(np.exp(log_g[:, None] * (pos[None, :] + 1.0)).T, RET_V, axis=1)
    kdect = np.repeat(np.exp(log_g[:, None] * (c - 1.0 - pos[None, :])), RET_QK, axis=0)
    cd = np.repeat(np.exp(log_g * c), RET_QK)[:, None] * np.ones((1, h * RET_V))
    return tuple(jnp.asarray(a, _F32) for a in (intra, qdec, kdect, cd))


def _retention(norm_w, n16, n32, t32, bn, s):
    c = RET_C
    per = CH // c
    intra, qdec, kdect, cd = _retention_consts(c)
    return pl.pallas_call(
        functools.partial(_ret_kernel, c=c),
        grid=(bn, s // c),
        in_specs=[
            pl.BlockSpec((1, c, 256), lambda b, i: (b, i, 2)),
            pl.BlockSpec((1, 1, 256, c), lambda b, i: (b, i // per, 0, i % per)),
            pl.BlockSpec((1, c, 256), lambda b, i: (b, i, 3)),
            pl.BlockSpec((1, c, 256), lambda b, i: (b, i, 0)),
            _const_spec(intra.shape), _const_spec(qdec.shape), _const_spec(kdect.shape),
            _const_spec(cd.shape), _const_spec(norm_w.shape),
        ],
        out_specs=pl.BlockSpec((1, c, 256), lambda b, i: (b, i, 0)),
        out_shape=jax.ShapeDtypeStruct((bn, s, 256), _BF16),
        scratch_shapes=[pltpu.VMEM((RET_HEADS * RET_QK, RET_HEADS * RET_V), _F32)],
        compiler_params=pltpu.CompilerParams(dimension_semantics=("parallel", "arbitrary")),
        name="retention",
    )(n16, t32, n16, n32, intra, qdec, kdect, cd, norm_w)


def _gla_kernel(q_ref, k_ref, a_ref, v_ref, vt_ref, g_ref, wa_ref, ba_ref, nw_ref, o_ref,
                st_ref, u_ref, oacc_ref, qs_ref, kk_ref, b_ref, qh_ref, dec_ref, *, ts, sub):
    nsub = ts // sub
    hi = lax.Precision.HIGHEST

    @pl.when(pl.program_id(1) == 0)
    def _():
        st_ref[...] = jnp.zeros(st_ref.shape, _F32)

    la = jax.nn.log_sigmoid(_dot(a_ref[0].astype(_BF16), wa_ref[...]) + ba_ref[...]) * (1.0 / GLA_GATE_TEMP)
    rr, cc = _iota((ts, ts), 0), _iota((ts, ts), 1)
    same = (rr // sub) == (cc // sub)
    b = _dot(jnp.where(same & (cc <= rr), 1.0, 0.0), la, precision=hi)
    bl = _dot(jnp.where(same, 1.0, 0.0), la, precision=hi)
    qs = q_ref[0] * (GLA_QK ** -0.5)
    kk = k_ref[0]
    kd = (kk * jnp.exp(bl - b)).astype(_BF16)
    qs_ref[...] = qs
    kk_ref[...] = kk
    b_ref[...] = b
    qh_ref[...] = (qs * jnp.exp(b)).astype(_BF16)
    dec_ref[...] = jnp.exp(bl)
    vt = vt_ref[0, 0]
    row_blk = _iota(kd.shape, 0) // sub
    for n in range(nsub):
        u_ref[n] = _dot(vt, jnp.where(row_blk == n, kd, jnp.zeros_like(kd)))

    st_keep = (_iota(st_ref.shape, 0) // GLA_V) == (_iota(st_ref.shape, 1) // GLA_QK)
    spread = jnp.where((_iota((128, 256), 0) // GLA_QK) == (_iota((128, 256), 1) // GLA_V), 1.0, 0.0).astype(_BF16)
    row16 = _iota((sub, 128), 0)

    def body(n, carry):
        r0 = pl.multiple_of(n * sub, sub)
        st = st_ref[...]
        o_cross = _dot_nt(qh_ref[pl.ds(r0, sub), :], st.astype(_BF16))
        q16 = qs_ref[pl.ds(r0, sub), :]
        k16 = kk_ref[pl.ds(r0, sub), :]
        b16 = b_ref[pl.ds(r0, sub), :]
        v16 = v_ref[0, pl.ds(r0, sub), :].astype(_F32)
        es = []
        for j in range(sub):
            e = q16 * k16[j:j + 1] * jnp.exp(jnp.minimum(b16 - b16[j:j + 1], 0.0))
            es.append(jnp.where(row16 >= j, e, 0.0))
        e_all = jnp.concatenate(es, axis=0)
        e_hi = e_all.astype(_BF16)
        e_lo = (e_all - e_hi.astype(_F32)).astype(_BF16)
        att = _dot(e_hi, spread) + _dot(e_lo, spread)
        o_diag = att[0:sub] * v16[0:1]
        for j in range(1, sub):
            o_diag = o_diag + att[j * sub:(j + 1) * sub] * v16[j:j + 1]
        oacc_ref[pl.ds(r0, sub), :] = o_cross + o_diag
        st_ref[...] = st * dec_ref[pl.ds(r0, 1), :] + jnp.where(st_keep, u_ref[n], 0.0)
        return carry

    lax.fori_loop(0, nsub, body, 0)

    o = oacc_ref[...]
    ms = _group_mean(o * o, GLA_V)
    y = o * lax.rsqrt(ms + LN_EPS) * nw_ref[...]
    gate = g_ref[0]
    o_ref[0] = (gate * jax.nn.sigmoid(gate) * y).astype(_BF16)


def _gla(wa, ba, norm_w, n16, n32, t16, bn, s):
    ts, sub = CH, GLA_SUB
    return pl.pallas_call(
        functools.partial(_gla_kernel, ts=ts, sub=sub),
        grid=(bn, s // ts),
        in_specs=[
            pl.BlockSpec((1, ts, 128), lambda b, i: (b, i, 4)),
            pl.BlockSpec((1, ts, 128), lambda b, i: (b, i, 5)),
            pl.BlockSpec((1, ts, 128), lambda b, i: (b, i, 6)),
            pl.BlockSpec((1, ts, 256), lambda b, i: (b, i, 4)),
            pl.BlockSpec((1, 1, 256, ts), lambda b, i: (b, i, 4, 0)),
            pl.BlockSpec((1, ts, 256), lambda b, i: (b, i, 1)),
            _const_spec(wa.shape), _const_spec(ba.shape), _const_spec(norm_w.shape),
        ],
        out_specs=pl.BlockSpec((1, ts, 256), lambda b, i: (b, i, 0)),
        out_shape=jax.ShapeDtypeStruct((bn, s, 256), _BF16),
        scratch_shapes=[
            pltpu.VMEM((GLA_HEADS * GLA_V, GLA_HEADS * GLA_QK), _F32),
            pltpu.VMEM((ts // sub, GLA_HEADS * GLA_V, GLA_HEADS * GLA_QK), _F32),
            pltpu.VMEM((ts, 256), _F32),
            pltpu.VMEM((ts, 128), _F32), pltpu.VMEM((ts, 128), _F32), pltpu.VMEM((ts, 128), _F32),
            pltpu.VMEM((ts, 128), _BF16), pltpu.VMEM((ts, 128), _F32),
        ],
        compiler_params=pltpu.CompilerParams(dimension_semantics=("parallel", "arbitrary")),
        name="gla",
    )(n32, n32, n32, n16, t16, n32, wa, ba, norm_w)


def _merge_kernel(x_ref, ya_ref, yb_ref, yc_ref, yd_ref, wg_ref, wbr_ref, wout_ref, lnw_ref, lnb_ref, h_ref):
    x = x_ref[...]
    xb = x.astype(_BF16)
    merged = None
    for n, y_ref in enumerate((ya_ref, yb_ref, yc_ref, yd_ref)):
        gate = jax.nn.sigmoid(_dot(xb, wg_ref[:, n * D_MODEL:(n + 1) * D_MODEL]))
        term = gate * _dot(y_ref[...], wbr_ref[n])
        merged = term if merged is None else merged + term
    mix = _dot(merged.astype(_BF16), wout_ref[...])
    h_ref[...] = _layer_norm(DEEPNORM_ALPHA * x + mix, lnw_ref[...], lnb_ref[...])


def _merge(x2, ys, wg, wbr, wout, lnw, lnb):
    t = x2.shape[0]
    ts = TS_DENSE
    tok = lambda w: pl.BlockSpec((ts, w), lambda i: (i, 0))
    return pl.pallas_call(
        _merge_kernel,
        grid=(t // ts,),
        in_specs=[tok(D_MODEL)] + [tok(BRANCH_W)] * 4 + [
            _const_spec(wg.shape), _const_spec(wbr.shape), _const_spec(wout.shape),
            _const_spec(lnw.shape), _const_spec(lnb.shape)],
        out_specs=tok(D_MODEL),
        out_shape=jax.ShapeDtypeStruct((t, D_MODEL), _F32),
        compiler_params=pltpu.CompilerParams(
            dimension_semantics=("parallel",), vmem_limit_bytes=V7X_VMEM_LIMIT),
        name="merge_ln",
    )(x2, *ys, wg, wbr, wout, lnw, lnb)


def _ffn_kernel(h_ref, p_ref, wup_ref, cw_ref, cb_ref, wdn_ref, wpg_ref, wpp_ref, lnw_ref, lnb_ref,
                o_ref, tail_ref, *, ts):
    @pl.when(pl.program_id(1) == 0)
    def _():
        tail_ref[...] = jnp.zeros(tail_ref.shape, _F32)

    h = h_ref[...]
    hb = h.astype(_BF16)
    row = _iota((ts, FF_COLS), 0)
    f = None
    for c0 in range(0, D_FF, FF_COLS):
        cols = slice(c0, c0 + FF_COLS)
        u = _dot(hb, wup_ref[:, cols])
        gt = _dot(hb, wup_ref[:, D_FF + c0:D_FF + c0 + FF_COLS])
        prev = tail_ref[:, cols]
        g1 = jnp.where(row == 0, prev[7:8], pltpu.roll(gt, 1, 0))
        g2 = jnp.where(row == 0, prev[6:7], jnp.where(row == 1, prev[7:8], pltpu.roll(gt, 2, 0)))
        tail_ref[:, cols] = gt[ts - 8:ts]
        gc = cb_ref[:, cols] + cw_ref[0:1, cols] * g2
        gc = gc + cw_ref[1:2, cols] * g1
        gc = gc + cw_ref[2:3, cols] * gt
        term = _dot((jax.nn.gelu(gc) * u).astype(_BF16), wdn_ref[cols, :])
        f = term if f is None else f + term
    e = jax.nn.sigmoid(_dot(hb, wpg_ref[...])) * _dot(p_ref[...].astype(_BF16), wpp_ref[...])
    o_ref[...] = _layer_norm(DEEPNORM_ALPHA * h + f + e, lnw_ref[...], lnb_ref[...])


def _ffn(h2, p2, wup, cw, cb, wdn, wpg, wpp, lnw, lnb, bn, s):
    ts = TS_DENSE
    ns = s // ts
    tok = lambda w: pl.BlockSpec((ts, w), lambda b, i: (b * ns + i, 0))
    return pl.pallas_call(
        functools.partial(_ffn_kernel, ts=ts),
        grid=(bn, ns),
        in_specs=[tok(D_MODEL), tok(P_DIM)] + [_const_spec(a.shape) for a in (wup, cw, cb, wdn, wpg, wpp, lnw, lnb)],
        out_specs=tok(D_MODEL),
        out_shape=jax.ShapeDtypeStruct((bn * s, D_MODEL), _F32),
        scratch_shapes=[pltpu.VMEM((8, D_FF), _F32)],
        compiler_params=pltpu.CompilerParams(
            dimension_semantics=("parallel", "arbitrary"), vmem_limit_bytes=V7X_VMEM_LIMIT),
        name="ffn_ple_ln",
    )(h2, p2, wup, cw, cb, wdn, wpg, wpp, lnw, lnb)


def kernel(x, p, w_in, a_lambda, a_norm_w, ret_norm_w, gla_w_a2, gla_b_a, gla_norm_w, w_branch, w_out,
           ln1_w, ln1_b, w_ffn_up, ffn_conv_w, ffn_conv_b, w_ffn_down, w_ple_gate, w_ple_proj, ln2_w, ln2_b):
    bn, s, _ = x.shape
    t = bn * s
    slopes = [2.0 ** (-(8.0 / N_SOFTMAX_HEADS) * i) for i in range(1, N_SOFTMAX_HEADS + 1)]
    slopes_a, slopes_b = tuple(slopes[0::2]), tuple(slopes[1::2])
    row = lambda v: v.astype(_F32).reshape(1, -1)
    x2 = x.reshape(t, D_MODEL)
    for i in range(DEPTH):
        *proj_w, wg = _prep_weights(w_in, i)
        n32, n16, t16, t32 = _project(x2, *proj_w, bn, s)
        n32 = n32.reshape(bn, s, -1)
        n16 = n16.reshape(bn, s, -1)
        lam_init = 0.8 - 0.6 * math.exp(-0.3 * i)
        y_a = _diff_attention(a_lambda[i].astype(_F32), a_norm_w[i].astype(_F32).reshape(DA_V, 1),
                              n16, t16, bn, s, slopes_a, lam_init)
        y_b = _dsa_attention(n16, t16, t32, bn, s, slopes_b)
        y_c = _retention(row(ret_norm_w[i]), n16, n32, t32, bn, s)
        wa = jnp.pad(gla_w_a2[i], ((0, 128 - GLA_RANK), (0, 0))).astype(_BF16)
        y_d = _gla(wa, row(gla_b_a[i]), row(jnp.tile(gla_norm_w[i], GLA_HEADS)), n16, n32, t16, bn, s)
        ys = [y.reshape(t, BRANCH_W) for y in (y_a, y_b, y_c, y_d)]
        h2 = _merge(x2, ys, wg, w_branch[i].astype(_BF16), w_out[i].astype(_BF16), row(ln1_w[i]), row(ln1_b[i]))
        x2 = _ffn(h2, p[i].reshape(t, P_DIM), w_ffn_up[i].astype(_BF16), ffn_conv_w[i].astype(_F32),
                  row(ffn_conv_b[i]), w_ffn_down[i].astype(_BF16), w_ple_gate[i].astype(_BF16),
                  w_ple_proj[i].astype(_BF16), row(ln2_w[i]), row(ln2_b[i]), bn, s)
    return x2.reshape(bn, s, D_MODEL)
```

```python
import functools
import math

import numpy as np
import jax
import jax.numpy as jnp
from jax import lax
from jax.experimental import pallas as pl
from jax.experimental.pallas import tpu as pltpu

D_MODEL = 1024
DEPTH = 2
P_DIM = 256
N_BRANCH = 4
BRANCH_W = 256
DA_HEADS = 4
DA_QK = 32
DA_V = 64
DSA_HEADS = 4
DSA_HD = 64
IDX_HEADS = 4
IDX_HD = 32
TOPK_MAX = 256
RET_HEADS = 4
RET_QK = 64
RET_V = 64
GLA_HEADS = 4
GLA_QK = 32
GLA_V = 64
GLA_RANK = 16
GLA_GATE_TEMP = 16.0
D_FF = 2816
CONV_W = 3
N_SOFTMAX_HEADS = DA_HEADS + DSA_HEADS
LN_EPS = 1e-5
NEG_INF = -1e30
DEEPNORM_ALPHA = (2.0 * DEPTH) ** 0.25

IN_SIZES = (256, 256, 256, 256, 256, 256, 128, 32, 4, 256, 256, 256, 256, 128, 128, 256, 16, 256, 4096)
IN_NAMES = ("a_q", "a_k", "a_v", "b_q", "b_k", "b_v", "b_iq", "b_ik", "b_iw",
            "c_q", "c_k", "c_v", "c_g", "d_q", "d_k", "d_v", "d_a", "d_g", "m_g")

_BF16 = jnp.bfloat16
_F32 = jnp.float32
_INT_MIN = -2 ** 31
_LOG2E = math.log2(math.e)

CH = 256
TQ = 128
RET_C = 128
GLA_SUB = 16
COUNT_UNROLL = 4
TS_DENSE = 256
FF_COLS = 256
PREP_ROWS = 256
NEAR_CHUNKS = 4
SKIP_NATS = 64.0
V7X_VMEM_LIMIT = 56 * 1024 * 1024


def _dot(a, b, precision=None):
    return jnp.dot(a, b, preferred_element_type=_F32, precision=precision)


def _dot_nt(a, b):
    return lax.dot_general(a, b, (((1,), (1,)), ((), ())), preferred_element_type=_F32)


def _iota(shape, dim):
    return lax.broadcasted_iota(jnp.int32, shape, dim)


def _block_diag_tile(m_t, ngroups):
    r, tq = m_t.shape
    tiled = jnp.concatenate([m_t] * ngroups, axis=1)
    keep = (_iota(tiled.shape, 0) // (r // ngroups)) == (_iota(tiled.shape, 1) // tq)
    return jnp.where(keep, tiled, jnp.zeros_like(tiled))


def _layer_norm(x, w, b):
    mu = jnp.mean(x, -1, keepdims=True)
    var = jnp.mean(jnp.square(x - mu), -1, keepdims=True)
    return (x - mu) * lax.rsqrt(var + LN_EPS) * w + b


def _group_mean(x, group):
    lane_g = _iota(x.shape, 1) // group
    out = jnp.zeros_like(x)
    for h in range(x.shape[1] // group):
        mk = lane_g == h
        mh = jnp.sum(jnp.where(mk, x, 0.0), axis=1, keepdims=True) * (1.0 / group)
        out = jnp.where(mk, mh, out)
    return out


def _const_spec(shape):
    nd = len(shape)
    return pl.BlockSpec(shape, lambda *_: (0,) * nd, pipeline_mode=pl.Buffered(1))


def _prep_kernel(w_ref, wn32_ref, wn16_ref, wt16_ref, wt32_ref, wg_ref):
    offs = [0] + np.cumsum(IN_SIZES).tolist()
    col = {n: w_ref[0, :, offs[i]:offs[i + 1]] for i, n in enumerate(IN_NAMES)}
    zeros = lambda n: jnp.zeros((w_ref.shape[1], n), _F32)
    cat = lambda xs: jnp.concatenate(xs, axis=1)
    wn32_ref[...] = cat([col["c_g"], col["d_g"], col["d_q"], col["d_k"], col["d_a"],
                         zeros(128 - GLA_RANK)]).astype(_BF16)
    wn16_ref[...] = cat([col["a_k"], col["b_k"], col["c_q"], col["c_v"], col["d_v"]]
                        + [col["b_ik"]] * IDX_HEADS).astype(_BF16)
    wt16_ref[...] = cat([col["a_q"], col["a_v"], col["b_q"], col["b_v"], col["d_v"], col["b_iq"]]).T.astype(_BF16)
    wt32_ref[...] = cat([col["c_k"], col["b_iw"], zeros(16 - IDX_HEADS)]).T.astype(_BF16)
    wg_ref[...] = col["m_g"].astype(_BF16)


def _prep_weights(w_in, layer):
    rb = PREP_ROWS
    n32, n16, t16, t32, ng = 896, 1408, 1408, 272, N_BRANCH * D_MODEL
    rows = lambda n: pl.BlockSpec((rb, n), lambda r: (r, 0))
    cols = lambda n: pl.BlockSpec((n, rb), lambda r: (0, r))
    return pl.pallas_call(
        _prep_kernel,
        grid=(D_MODEL // rb,),
        in_specs=[pl.BlockSpec((1, rb, w_in.shape[2]), lambda r: (layer, r, 0))],
        out_specs=[rows(n32), rows(n16), cols(t16), cols(t32), rows(ng)],
        out_shape=[jax.ShapeDtypeStruct((D_MODEL, n32), _BF16), jax.ShapeDtypeStruct((D_MODEL, n16), _BF16),
                   jax.ShapeDtypeStruct((t16, D_MODEL), _BF16), jax.ShapeDtypeStruct((t32, D_MODEL), _BF16),
                   jax.ShapeDtypeStruct((D_MODEL, ng), _BF16)],
        compiler_params=pltpu.CompilerParams(
            dimension_semantics=("parallel",), vmem_limit_bytes=V7X_VMEM_LIMIT),
        name="prep_weights",
    )(w_in)


def _proj_kernel(x_ref, wn32_ref, wn16_ref, wt16_ref, wt32_ref, n32_ref, n16_ref, t16_ref, t32_ref):
    x = x_ref[...].astype(_BF16)
    n32_ref[...] = _dot(x, wn32_ref[...])
    n16_ref[...] = _dot(x, wn16_ref[...]).astype(_BF16)
    t16_ref[0, 0] = _dot_nt(wt16_ref[...], x).astype(_BF16)
    t32_ref[0, 0] = _dot_nt(wt32_ref[...], x)


def _project(x2, wn32, wn16, wt16, wt32, bn, s):
    ns = s // CH
    n32, n16, t16, t32 = wn32.shape[1], wn16.shape[1], wt16.shape[0], wt32.shape[0]
    return pl.pallas_call(
        _proj_kernel,
        grid=(bn, ns),
        in_specs=[
            pl.BlockSpec((CH, D_MODEL), lambda b, i: (b * ns + i, 0)),
            _const_spec(wn32.shape), _const_spec(wn16.shape), _const_spec(wt16.shape), _const_spec(wt32.shape),
        ],
        out_specs=[
            pl.BlockSpec((CH, n32), lambda b, i: (b * ns + i, 0)),
            pl.BlockSpec((CH, n16), lambda b, i: (b * ns + i, 0)),
            pl.BlockSpec((1, 1, t16, CH), lambda b, i: (b, i, 0, 0)),
            pl.BlockSpec((1, 1, t32, CH), lambda b, i: (b, i, 0, 0)),
        ],
        out_shape=[
            jax.ShapeDtypeStruct((bn * s, n32), _F32),
            jax.ShapeDtypeStruct((bn * s, n16), _BF16),
            jax.ShapeDtypeStruct((bn, ns, t16, CH), _BF16),
            jax.ShapeDtypeStruct((bn, ns, t32, CH), _F32),
        ],
        compiler_params=pltpu.CompilerParams(
            dimension_semantics=("parallel", "parallel"), vmem_limit_bytes=V7X_VMEM_LIMIT),
        name="proj",
    )(x2, wn32, wn16, wt16, wt32)


def _slope_row(slopes, reps, tq):
    return jnp.concatenate([jnp.full((1, tq), s * _LOG2E, _F32) for s in slopes for _ in range(reps)], axis=1)


def _alibi_rows(slopes):
    rows = _iota((CH, 128), 0).astype(_F32)
    return jnp.stack([rows * (s * _LOG2E) for s in slopes], axis=0)


def _key_norm_bound(k_ref, kn_ref, ngroups, nchunks):
    gt = jnp.where(_iota((16, 256), 0) == _iota((16, 256), 1) // (256 // ngroups), 1.0, 0.0).astype(_BF16)

    def body(j, best):
        kc = k_ref[0, pl.ds(pl.multiple_of(j * CH, CH), CH), :].astype(_F32)
        return jnp.maximum(best, _dot_nt(gt, (kc * kc).astype(_BF16)))

    best = lax.fori_loop(0, nchunks, body, jnp.zeros((16, CH), _F32))
    kn_ref[...] = jnp.broadcast_to(jnp.sqrt(jnp.max(best, axis=1, keepdims=True) * (1.0 + 2.0 ** -7)), kn_ref.shape)


def _first_chunks(qt, kn_ref, ngroups, scale, slopes, q0, nfull, nearest=0):
    q = qt.astype(_F32)
    tq = q.shape[1]
    qn = jnp.sqrt(jnp.max(jnp.sum((q * q).reshape(ngroups, 256 // ngroups, tq), axis=1), axis=1, keepdims=True))
    bound = scale * qn * kn_ref[0:ngroups, 0:1]
    per_head = ngroups // len(slopes)
    inv_slope = jnp.concatenate([jnp.full((per_head, 1), 1.0 / s, _F32) for s in slopes], axis=0)
    reach = ((2.0 * bound + SKIP_NATS) * inv_slope).astype(jnp.int32) + 2 + nearest
    last_far = q0 - CH + 1 - reach
    first = jnp.where(last_far < 0, 0, last_far // CH + 1)
    first = jnp.minimum(first, nfull)
    return [jnp.min(first[h * per_head:(h + 1) * per_head]) for h in range(len(slopes))]


def _attend(k_ref, vt_ref, bd_ref, ab_ref, slope_row, c1, q0, nfull, tq, w, mask_fn, aux0, first):
    g_tq = bd_ref.shape[1]
    nheads = 4

    def pv(j, p, heads):
        vt_c = vt_ref[0, j]
        return {h: _dot(vt_c[h * 64:(h + 1) * 64, :], p[:, h * w:(h + 1) * w]) for h in heads}

    def qk(j, heads):
        kc = k_ref[0, pl.ds(pl.multiple_of(j * CH, CH), CH), :]
        tiles = sorted({c0 // 256 for h in heads for c0 in range(h * w, (h + 1) * w, 128)})
        return {t: _dot(kc, bd_ref[:, t * 256:(t + 1) * 256]) for t in tiles}

    def softmax(s, j, m, l, aux, heads, diag):
        crow = slope_row * (j * CH - q0).astype(_F32)
        amask, aux = mask_fn(j, aux, diag)
        ps, ms, ls, alphas = [], [], [], []
        for c0 in range(0, g_tq, 128):
            cols = slice(c0, c0 + 128)
            if c0 // w not in heads:
                ps.append(jnp.zeros((CH, 128), _BF16))
                ms.append(m[:, cols])
                ls.append(l[:, cols])
                alphas.append(jnp.ones((1, 128), _F32))
                continue
            t = s[c0 // 256][:, c0 % 256:c0 % 256 + 128] * c1 + ab_ref[c0 // w]
            if amask is not None:
                t = t + amask[:, c0 % tq:c0 % tq + 128]
            m_new = jnp.maximum(m[:, cols], jnp.max(t, axis=0, keepdims=True) + crow[:, cols])
            alpha = jnp.exp2(m[:, cols] - m_new)
            p = jnp.exp2(t - (m_new - crow[:, cols]))
            ls.append(alpha * l[:, cols] + jnp.sum(p, axis=0, keepdims=True))
            ps.append(p.astype(_BF16))
            ms.append(m_new)
            alphas.append(alpha)
        cat = lambda xs: jnp.concatenate(xs, axis=1)
        return cat(ms), cat(ls), cat(alphas), cat(ps), aux

    def step(s, j, carry, heads, diag=False):
        p, m, l, acc, aux = carry
        pvs = pv(jnp.maximum(j - 1, 0), p, heads)
        m, l, alpha, p, aux = softmax(s, j, m, l, aux, heads, diag)
        acc = [alpha[:, h * w:(h + 1) * w] * (acc[h] + pvs[h]) if h in heads else acc[h] for h in range(nheads)]
        return p, m, l, acc, aux

    def run(lo, hi, heads, carry):
        unroll = 4 if len(heads) <= 2 else 2

        def group(i, c):
            j = lo + unroll * i
            ss = [qk(j + u, heads) for u in range(unroll)]
            for u in range(unroll):
                c = step(ss[u], j + u, c, heads)
            return c

        ngroup = jnp.maximum(hi - lo, 0) // unroll
        carry = lax.fori_loop(0, ngroup, group, carry)
        return lax.fori_loop(lo + unroll * ngroup, hi, lambda j, c: step(qk(j, heads), j, c, heads), carry)

    carry = (jnp.zeros((CH, g_tq), _BF16), jnp.full((1, g_tq), NEG_INF, _F32),
             jnp.zeros((1, g_tq), _F32), [jnp.zeros((64, w), _F32) for _ in range(nheads)], aux0)
    start = [first[0]]
    for h in range(1, nheads):
        start.append(jnp.minimum(first[h], start[-1]))
    for h in reversed(range(nheads)):
        heads = tuple(range(h, nheads))
        carry = run(start[h], start[h - 1] if h else nfull, heads, carry)
    heads = tuple(range(nheads))
    p, m, l, acc, aux = step(qk(nfull, heads), nfull, carry, heads, True)
    pvs = pv(nfull, p, heads)
    return l, [acc[h] + pvs[h] for h in range(nheads)]


def _attn_a_kernel(lam_ref, nw_ref, qt_ref, k_ref, vt_ref, o_ref, bd_ref, ab_ref, kn_ref, *, tq, slopes, lam_init):
    g = 2 * DA_HEADS
    q0 = pl.program_id(1) * tq
    nfull = q0 // CH

    @pl.when(pl.program_id(1) == 0)
    def _():
        _key_norm_bound(k_ref, kn_ref, g, k_ref.shape[1] // CH)

    bd_ref[...] = _block_diag_tile(qt_ref[0, 0], g)
    ab_ref[...] = _alibi_rows(slopes)

    def mask(j, aux, diag):
        if not diag:
            return None, aux
        rel = _iota((CH, tq), 0) - _iota((CH, tq), 1)
        return jnp.where(rel <= q0 - j * CH, 0.0, NEG_INF), aux

    l, acc = _attend(k_ref, vt_ref, bd_ref, ab_ref, _slope_row(slopes, 2, tq), DA_QK ** -0.5 * _LOG2E,
                     q0, nfull, tq, 2 * tq, mask, jnp.zeros((1, tq), _F32),
                     _first_chunks(qt_ref[0, 0], kn_ref, g, DA_QK ** -0.5, slopes, q0, nfull))

    lp = lam_ref[...]
    lam = (jnp.exp(jnp.sum(lp[0:1] * lp[1:2], axis=1, keepdims=True))
           - jnp.exp(jnp.sum(lp[2:3] * lp[3:4], axis=1, keepdims=True)) + lam_init)
    linv = 1.0 / l
    outs = []
    for h in range(DA_HEADS):
        a = acc[h] * linv[:, h * 2 * tq:(h + 1) * 2 * tq]
        o = a[:, :tq] - lam * a[:, tq:]
        ms = jnp.mean(o * o, axis=0, keepdims=True)
        outs.append(o * lax.rsqrt(ms + LN_EPS) * nw_ref[...] * (1.0 - lam_init))
    o_ref[0] = jnp.concatenate(outs, axis=0).T.astype(_BF16)


def _diff_attention(lam_p, norm_w, n16, t16, bn, s, slopes, lam_init):
    tq = TQ
    ns, per = s // CH, CH // tq
    g = 2 * DA_HEADS
    kern = functools.partial(_attn_a_kernel, tq=tq, slopes=slopes, lam_init=lam_init)
    return pl.pallas_call(
        kern,
        grid=(bn, s // tq),
        in_specs=[
            _const_spec(lam_p.shape), _const_spec(norm_w.shape),
            pl.BlockSpec((1, 1, 256, tq), lambda b, i: (b, i // per, 0, i % per)),
            pl.BlockSpec((1, s, 256), lambda b, i: (b, 0, 0)),
            pl.BlockSpec((1, ns, 256, CH), lambda b, i: (b, 0, 1, 0)),
        ],
        out_specs=pl.BlockSpec((1, tq, 256), lambda b, i: (b, i, 0)),
        out_shape=jax.ShapeDtypeStruct((bn, s, 256), _BF16),
        scratch_shapes=[pltpu.VMEM((256, g * tq), _BF16), pltpu.VMEM((DA_HEADS, CH, 128), _F32),
                        pltpu.VMEM((16, 128), _F32)],
        compiler_params=pltpu.CompilerParams(
            dimension_semantics=("parallel", "arbitrary"), vmem_limit_bytes=V7X_VMEM_LIMIT),
        name="diff_attn",
    )(lam_p, norm_w, t16, n16, t16)


def _bit_planes(rows):
    a = list(rows)
    j, m = 16, 0x0000FFFF
    while j:
        k = 0
        while k < 32:
            t = (a[k] ^ lax.shift_right_logical(a[k + j], jnp.int32(j))) & jnp.int32(m)
            a[k] = a[k] ^ t
            a[k + j] = a[k + j] ^ (t << j)
            k = (k + j + 1) & ~j
        j >>= 1
        if j:
            m = (m ^ (m << j)) & 0xFFFFFFFF
            m = m - (1 << 32) if m >= (1 << 31) else m
    return a[::-1]


def _dsa_kernel(iqt_ref, ik_ref, iwt_ref, qt_ref, k_ref, vt_ref, o_ref,
                iqbd_ref, bd_ref, key_ref, planes_ref, alive_ref, ab_ref, kn_ref, *, tq, slopes, topk):
    g = DSA_HEADS
    q0 = pl.program_id(1) * tq
    nfull = q0 // CH

    @pl.when(pl.program_id(1) == 0)
    def _():
        _key_norm_bound(k_ref, kn_ref, g, k_ref.shape[1] // CH)

    ngrp = (nfull + COUNT_UNROLL) // COUNT_UNROLL
    iqbd_ref[...] = _block_diag_tile(iqt_ref[0, 0], IDX_HEADS)
    bd_ref[...] = _block_diag_tile(qt_ref[0, 0], g)
    ab_ref[...] = _alibi_rows(slopes)
    w = iwt_ref[0, 0][0:IDX_HEADS, :] * (IDX_HEADS ** -0.5 * IDX_HD ** -0.5)

    def logits(j):
        return _dot(ik_ref[0, pl.ds(pl.multiple_of(j * CH, CH), CH), :], iqbd_ref[...])

    def score(lg, j, diag):
        half = CH // 2
        rows = []
        for r0 in (0, half):
            sc = jnp.maximum(lg[r0:r0 + half, 0:tq], 0.0) * w[0:1]
            for h in range(1, IDX_HEADS):
                sc = sc + jnp.maximum(lg[r0:r0 + half, h * tq:(h + 1) * tq], 0.0) * w[h:h + 1]
            sc = jnp.where(sc == 0.0, 0.0, sc)
            bits = pltpu.bitcast(sc, jnp.int32)
            key = bits ^ ((bits >> 31) & 0x7FFFFFFF)
            if diag:
                rel = _iota(key.shape, 0) - _iota(key.shape, 1)
                key = jnp.where(rel <= q0 - j * CH - r0, key, _INT_MIN)
            key_ref[j, r0:r0 + half, :] = key
            key3 = key.reshape(half // 8, 8, tq)
            rows += [key3[v] for v in range(half // 8)]
        planes = _bit_planes(rows)
        planes[31] = ~planes[31]
        planes_ref[j] = jnp.stack(planes, axis=0)
        if diag:
            lim = q0 - j * CH + _iota((8, tq), 1) - _iota((8, tq), 0)
            nbits = jnp.clip((lim >> 3) + 1, 0, 32)
            alive_ref[j] = jnp.where(nbits == 0, 0, jnp.left_shift(jnp.int32(-1), 32 - jnp.maximum(nbits, 1)))
        else:
            alive_ref[j] = jnp.full((8, tq), -1, jnp.int32)

    def score_pair(i, lgs):
        nxt = (logits(2 * i + 2), logits(jnp.minimum(2 * i + 3, nfull)))
        score(lgs[0], 2 * i, False)
        score(lgs[1], 2 * i + 1, False)
        return nxt

    npair = nfull // 2
    lgs = lax.fori_loop(0, npair, score_pair, (logits(0), logits(jnp.minimum(1, nfull))))
    score(lgs[0], 2 * npair, True)

    @pl.when(nfull > 2 * npair)
    def _():
        score(lgs[1], nfull, True)

    for u in range(1, COUNT_UNROLL):
        @pl.when(nfull + u < ngrp * COUNT_UNROLL)
        def _():
            planes_ref[nfull + u] = jnp.zeros((32, 8, tq), jnp.int32)
            alive_ref[nfull + u] = jnp.zeros((8, tq), jnp.int32)

    def sweep(b_upd, keep, b_cnt):
        def body(gi, acc8):
            for u in range(COUNT_UNROLL):
                j = gi * COUNT_UNROLL + u
                a = alive_ref[j]
                if b_upd is not None:
                    a = a & ~(planes_ref[j, b_upd] ^ keep)
                    alive_ref[j] = a
                acc8 = acc8 + lax.population_count(a if b_cnt is None else a & planes_ref[j, b_cnt])
            return acc8
        acc8 = lax.fori_loop(0, ngrp, body, jnp.zeros((8, tq), jnp.int32))
        return jnp.sum(acc8, axis=0, keepdims=True)

    def decide(b, ones, want, thr):
        take = ones >= want
        thr = jnp.where(take, thr | jnp.left_shift(jnp.int32(1), b), thr)
        return jnp.where(take, want, want - ones), thr, jnp.where(take, -1, 0)

    want, thr, keep = decide(31, sweep(None, None, 31), jnp.full((1, tq), topk, jnp.int32),
                             jnp.zeros((1, tq), jnp.int32))

    def bit_body(i, c):
        want, thr, keep = c
        b = 30 - i
        return decide(b, sweep(b + 1, keep, b), want, thr)

    want, thr, keep = lax.fori_loop(0, 31, bit_body, (want, thr, keep))
    ties = sweep(0, keep, None)
    thr = thr ^ _INT_MIN
    need_tie = jnp.max(jnp.where((ties > want) & (thr > _INT_MIN), 1, 0)) > 0

    def attend(tie):
        if tie:
            first = [jnp.int32(0)] * g
            room = want.astype(_F32)
            lower = jnp.where(_iota((CH, CH), 0) > _iota((CH, CH), 1), 1.0, 0.0).astype(_BF16)
        else:
            thr_eff = jnp.maximum(thr, _INT_MIN + 1)

            def near_body(j, best):
                pos = jnp.where(key_ref[j] >= thr_eff, _iota((CH, tq), 0) + j * CH, -1)
                return jnp.maximum(best, jnp.max(pos, axis=0, keepdims=True))

            best = lax.fori_loop(jnp.maximum(nfull - (NEAR_CHUNKS - 1), 0), nfull + 1, near_body,
                                 jnp.full((1, tq), -1, jnp.int32))
            gap = jnp.where(best >= 0, q0 + _iota((1, tq), 1) - best, 2 ** 24)
            first = _first_chunks(qt_ref[0, 0], kn_ref, g, DSA_HD ** -0.5, slopes, q0, nfull,
                                  jnp.max(gap, axis=1, keepdims=True))

        def mask(j, seen, diag):
            key = key_ref[j]
            if tie:
                eq = key == thr
                eqf = jnp.where(eq, 1.0, 0.0)
                rank = _dot(lower, eqf.astype(_BF16)) + seen
                sel = ((key > thr) | (eq & (rank < room))) & (key > _INT_MIN)
                seen = seen + jnp.sum(eqf, axis=0, keepdims=True)
            else:
                sel = key >= thr_eff
            return jnp.where(sel, 0.0, NEG_INF), seen

        l, acc = _attend(k_ref, vt_ref, bd_ref, ab_ref, _slope_row(slopes, 1, tq), DSA_HD ** -0.5 * _LOG2E,
                         q0, nfull, tq, tq, mask, jnp.zeros((1, tq), _F32), first)
        linv = 1.0 / l
        outs = [acc[h] * linv[:, h * tq:(h + 1) * tq] for h in range(g)]
        o_ref[0] = jnp.concatenate(outs, axis=0).T.astype(_BF16)

    @pl.when(need_tie)
    def _():
        attend(True)

    @pl.when(jnp.logical_not(need_tie))
    def _():
        attend(False)


def _dsa_attention(n16, t16, t32, bn, s, slopes):
    tq = TQ
    ns, per = s // CH, CH // tq
    g = DSA_HEADS
    topk = min(TOPK_MAX, s // 4)
    assert ns % COUNT_UNROLL == 0
    kern = functools.partial(_dsa_kernel, tq=tq, slopes=slopes, topk=topk)
    return pl.pallas_call(
        kern,
        grid=(bn, s // tq),
        in_specs=[
            pl.BlockSpec((1, 1, 128, tq), lambda b, i: (b, i // per, 10, i % per)),
            pl.BlockSpec((1, s, 128), lambda b, i: (b, 0, 10)),
            pl.BlockSpec((1, 1, 16, tq), lambda b, i: (b, i // per, 16, i % per)),
            pl.BlockSpec((1, 1, 256, tq), lambda b, i: (b, i // per, 2, i % per)),
            pl.BlockSpec((1, s, 256), lambda b, i: (b, 0, 1)),
            pl.BlockSpec((1, ns, 256, CH), lambda b, i: (b, 0, 3, 0)),
        ],
        out_specs=pl.BlockSpec((1, tq, 256), lambda b, i: (b, i, 0)),
        out_shape=jax.ShapeDtypeStruct((bn, s, 256), _BF16),
        scratch_shapes=[
            pltpu.VMEM((128, IDX_HEADS * tq), _BF16), pltpu.VMEM((256, g * tq), _BF16),
            pltpu.VMEM((ns, CH, tq), jnp.int32), pltpu.VMEM((ns, 32, 8, tq), jnp.int32),
            pltpu.VMEM((ns, 8, tq), jnp.int32), pltpu.VMEM((DSA_HEADS, CH, 128), _F32),
            pltpu.VMEM((16, 128), _F32),
        ],
        compiler_params=pltpu.CompilerParams(
            dimension_semantics=("parallel", "arbitrary"), vmem_limit_bytes=V7X_VMEM_LIMIT),
        name="dsa_attn",
    )(t16, n16, t32, t16, n16, t16)


def _ret_kernel(q_ref, kt_ref, v_ref, g_ref, intra_ref, qdec_ref, kdect_ref, cd_ref, nw_ref, o_ref, s_ref, *, c):
    @pl.when(pl.program_id(1) == 0)
    def _():
        s_ref[...] = jnp.zeros(s_ref.shape, _F32)

    q = q_ref[0]
    v = v_ref[0]
    kt = kt_ref[0, 0] * (RET_QK ** -0.5)
    att = _dot(q, _block_diag_tile(kt.astype(_BF16), RET_HEADS)) * intra_ref[...]
    vt = jnp.concatenate([v] * RET_HEADS, axis=0)
    vbd = jnp.where((_iota(vt.shape, 0) // c) == (_iota(vt.shape, 1) // RET_V), vt, jnp.zeros_like(vt))
    st = s_ref[...]
    o = _dot(att.astype(_BF16), vbd) + _dot(q, st.astype(_BF16)) * qdec_ref[...]
    upd = _dot((kt * kdect_ref[...]).astype(_BF16), v)
    same_head = (_iota(upd.shape, 0) // RET_QK) == (_iota(upd.shape, 1) // RET_V)
    s_ref[...] = st * cd_ref[...] + jnp.where(same_head, upd, 0.0)

    mu = _group_mean(o, RET_V)
    d = o - mu
    var = _group_mean(d * d, RET_V)
    y = d * lax.rsqrt(var + LN_EPS) * nw_ref[...]
    gate = g_ref[0]
    o_ref[0] = (gate * jax.nn.sigmoid(gate) * y).astype(_BF16)


def _retention_consts(c):
    h = RET_HEADS
    log_g = np.log1p(-np.power(2.0, -5.0 - np.arange(h, dtype=np.float64)))
    pos = np.arange(c, dtype=np.float64)
    rel = pos[:, None] - pos[None, :]
    intra = np.where(rel >= 0, np.exp(log_g[:, None, None] * np.maximum(rel, 0.0)), 0.0)
    intra = np.transpose(intra, (1, 0, 2)).reshape(c, h * c)
    qdec = np.repeat(np.exp(log_g[:, None] * (pos[None, :] + 1.0)).T, RET_V, axis=1)
    kdect = np.repeat(np.exp(log_g[:, None] * (c - 1.0 - pos[None, :])), RET_QK, axis=0)
    cd = np.repeat(np.exp(log_g * c), RET_QK)[:, None] * np.ones((1, h * RET_V))
    return tuple(jnp.asarray(a, _F32) for a in (intra, qdec, kdect, cd))


def _retention(norm_w, n16, n32, t32, bn, s):
    c = RET_C
    per = CH // c
    intra, qdec, kdect, cd = _retention_consts(c)
    return pl.pallas_call(
        functools.partial(_ret_kernel, c=c),
        grid=(bn, s // c),
        in_specs=[
            pl.BlockSpec((1, c, 256), lambda b, i: (b, i, 2)),
            pl.BlockSpec((1, 1, 256, c), lambda b, i: (b, i // per, 0, i % per)),
            pl.BlockSpec((1, c, 256), lambda b, i: (b, i, 3)),
            pl.BlockSpec((1, c, 256), lambda b, i: (b, i, 0)),
            _const_spec(intra.shape), _const_spec(qdec.shape), _const_spec(kdect.shape),
            _const_spec(cd.shape), _const_spec(norm_w.shape),
        ],
        out_specs=pl.BlockSpec((1, c, 256), lambda b, i: (b, i, 0)),
        out_shape=jax.ShapeDtypeStruct((bn, s, 256), _BF16),
        scratch_shapes=[pltpu.VMEM((RET_HEADS * RET_QK, RET_HEADS * RET_V), _F32)],
        compiler_params=pltpu.CompilerParams(dimension_semantics=("parallel", "arbitrary")),
        name="retention",
    )(n16, t32, n16, n32, intra, qdec, kdect, cd, norm_w)


def _gla_kernel(q_ref, k_ref, a_ref, v_ref, vt_ref, g_ref, wa_ref, ba_ref, nw_ref, o_ref,
                st_ref, u_ref, oacc_ref, qs_ref, kk_ref, b_ref, qh_ref, dec_ref, *, ts, sub):
    nsub = ts // sub
    hi = lax.Precision.HIGHEST

    @pl.when(pl.program_id(1) == 0)
    def _():
        st_ref[...] = jnp.zeros(st_ref.shape, _F32)

    la = jax.nn.log_sigmoid(_dot(a_ref[0].astype(_BF16), wa_ref[...]) + ba_ref[...]) * (1.0 / GLA_GATE_TEMP)
    rr, cc = _iota((ts, ts), 0), _iota((ts, ts), 1)
    same = (rr // sub) == (cc // sub)
    b = _dot(jnp.where(same & (cc <= rr), 1.0, 0.0), la, precision=hi)
    bl = _dot(jnp.where(same, 1.0, 0.0), la, precision=hi)
    qs = q_ref[0] * (GLA_QK ** -0.5)
    kk = k_ref[0]
    kd = (kk * jnp.exp(bl - b)).astype(_BF16)
    qs_ref[...] = qs
    kk_ref[...] = kk
    b_ref[...] = b
    qh_ref[...] = (qs * jnp.exp(b)).astype(_BF16)
    dec_ref[...] = jnp.exp(bl)
    vt = vt_ref[0, 0]
    row_blk = _iota(kd.shape, 0) // sub
    for n in range(nsub):
        u_ref[n] = _dot(vt, jnp.where(row_blk == n, kd, jnp.zeros_like(kd)))

    st_keep = (_iota(st_ref.shape, 0) // GLA_V) == (_iota(st_ref.shape, 1) // GLA_QK)
    spread = jnp.where((_iota((128, 256), 0) // GLA_QK) == (_iota((128, 256), 1) // GLA_V), 1.0, 0.0).astype(_BF16)
    row16 = _iota((sub, 128), 0)

    def body(n, carry):
        r0 = pl.multiple_of(n * sub, sub)
        st = st_ref[...]
        o_cross = _dot_nt(qh_ref[pl.ds(r0, sub), :], st.astype(_BF16))
        q16 = qs_ref[pl.ds(r0, sub), :]
        k16 = kk_ref[pl.ds(r0, sub), :]
        b16 = b_ref[pl.ds(r0, sub), :]
        v16 = v_ref[0, pl.ds(r0, sub), :].astype(_F32)
        es = []
        for j in range(sub):
            e = q16 * k16[j:j + 1] * jnp.exp(jnp.minimum(b16 - b16[j:j + 1], 0.0))
            es.append(jnp.where(row16 >= j, e, 0.0))
        e_all = jnp.concatenate(es, axis=0)
        e_hi = e_all.astype(_BF16)
        e_lo = (e_all - e_hi.astype(_F32)).astype(_BF16)
        att = _dot(e_hi, spread) + _dot(e_lo, spread)
        o_diag = att[0:sub] * v16[0:1]
        for j in range(1, sub):
            o_diag = o_diag + att[j * sub:(j + 1) * sub] * v16[j:j + 1]
        oacc_ref[pl.ds(r0, sub), :] = o_cross + o_diag
        st_ref[...] = st * dec_ref[pl.ds(r0, 1), :] + jnp.where(st_keep, u_ref[n], 0.0)
        return carry

    lax.fori_loop(0, nsub, body, 0)

    o = oacc_ref[...]
    ms = _group_mean(o * o, GLA_V)
    y = o * lax.rsqrt(ms + LN_EPS) * nw_ref[...]
    gate = g_ref[0]
    o_ref[0] = (gate * jax.nn.sigmoid(gate) * y).astype(_BF16)


def _gla(wa, ba, norm_w, n16, n32, t16, bn, s):
    ts, sub = CH, GLA_SUB
    return pl.pallas_call(
        functools.partial(_gla_kernel, ts=ts, sub=sub),
        grid=(bn, s // ts),
        in_specs=[
            pl.BlockSpec((1, ts, 128), lambda b, i: (b, i, 4)),
            pl.BlockSpec((1, ts, 128), lambda b, i: (b, i, 5)),
            pl.BlockSpec((1, ts, 128), lambda b, i: (b, i, 6)),
            pl.BlockSpec((1, ts, 256), lambda b, i: (b, i, 4)),
            pl.BlockSpec((1, 1, 256, ts), lambda b, i: (b, i, 4, 0)),
            pl.BlockSpec((1, ts, 256), lambda b, i: (b, i, 1)),
            _const_spec(wa.shape), _const_spec(ba.shape), _const_spec(norm_w.shape),
        ],
        out_specs=pl.BlockSpec((1, ts, 256), lambda b, i: (b, i, 0)),
        out_shape=jax.ShapeDtypeStruct((bn, s, 256), _BF16),
        scratch_shapes=[
            pltpu.VMEM((GLA_HEADS * GLA_V, GLA_HEADS * GLA_QK), _F32),
            pltpu.VMEM((ts // sub, GLA_HEADS * GLA_V, GLA_HEADS * GLA_QK), _F32),
            pltpu.VMEM((ts, 256), _F32),
            pltpu.VMEM((ts, 128), _F32), pltpu.VMEM((ts, 128), _F32), pltpu.VMEM((ts, 128), _F32),
            pltpu.VMEM((ts, 128), _BF16), pltpu.VMEM((ts, 128), _F32),
        ],
        compiler_params=pltpu.CompilerParams(dimension_semantics=("parallel", "arbitrary")),
        name="gla",
    )(n32, n32, n32, n16, t16, n32, wa, ba, norm_w)


def _merge_kernel(x_ref, ya_ref, yb_ref, yc_ref, yd_ref, wg_ref, wbr_ref, wout_ref, lnw_ref, lnb_ref, h_ref):
    x = x_ref[...]
    xb = x.astype(_BF16)
    merged = None
    for n, y_ref in enumerate((ya_ref, yb_ref, yc_ref, yd_ref)):
        gate = jax.nn.sigmoid(_dot(xb, wg_ref[:, n * D_MODEL:(n + 1) * D_MODEL]))
        term = gate * _dot(y_ref[...], wbr_ref[n])
        merged = term if merged is None else merged + term
    mix = _dot(merged.astype(_BF16), wout_ref[...])
    h_ref[...] = _layer_norm(DEEPNORM_ALPHA * x + mix, lnw_ref[...], lnb_ref[...])


def _merge(x2, ys, wg, wbr, wout, lnw, lnb):
    t = x2.shape[0]
    ts = TS_DENSE
    tok = lambda w: pl.BlockSpec((ts, w), lambda i: (i, 0))
    return pl.pallas_call(
        _merge_kernel,
        grid=(t // ts,),
        in_specs=[tok(D_MODEL)] + [tok(BRANCH_W)] * 4 + [
            _const_spec(wg.shape), _const_spec(wbr.shape), _const_spec(wout.shape),
            _const_spec(lnw.shape), _const_spec(lnb.shape)],
        out_specs=tok(D_MODEL),
        out_shape=jax.ShapeDtypeStruct((t, D_MODEL), _F32),
        compiler_params=pltpu.CompilerParams(
            dimension_semantics=("parallel",), vmem_limit_bytes=V7X_VMEM_LIMIT),
        name="merge_ln",
    )(x2, *ys, wg, wbr, wout, lnw, lnb)


def _ffn_kernel(h_ref, p_ref, wup_ref, cw_ref, cb_ref, wdn_ref, wpg_ref, wpp_ref, lnw_ref, lnb_ref,
                o_ref, tail_ref, *, ts):
    @pl.when(pl.program_id(1) == 0)
    def _():
        tail_ref[...] = jnp.zeros(tail_ref.shape, _F32)

    h = h_ref[...]
    hb = h.astype(_BF16)
    row = _iota((ts, FF_COLS), 0)
    f = None
    for c0 in range(0, D_FF, FF_COLS):
        cols = slice(c0, c0 + FF_COLS)
        u = _dot(hb, wup_ref[:, cols])
        gt = _dot(hb, wup_ref[:, D_FF + c0:D_FF + c0 + FF_COLS])
        prev = tail_ref[:, cols]
        g1 = jnp.where(row == 0, prev[7:8], pltpu.roll(gt, 1, 0))
        g2 = jnp.where(row == 0, prev[6:7], jnp.where(row == 1, prev[7:8], pltpu.roll(gt, 2, 0)))
        tail_ref[:, cols] = gt[ts - 8:ts]
        gc = cb_ref[:, cols] + cw_ref[0:1, cols] * g2
        gc = gc + cw_ref[1:2, cols] * g1
        gc = gc + cw_ref[2:3, cols] * gt
        term = _dot((jax.nn.gelu(gc) * u).astype(_BF16), wdn_ref[cols, :])
        f = term if f is None else f + term
    e = jax.nn.sigmoid(_dot(hb, wpg_ref[...])) * _dot(p_ref[...].astype(_BF16), wpp_ref[...])
    o_ref[...] = _layer_norm(DEEPNORM_ALPHA * h + f + e, lnw_ref[...], lnb_ref[...])


def _ffn(h2, p2, wup, cw, cb, wdn, wpg, wpp, lnw, lnb, bn, s):
    ts = TS_DENSE
    ns = s // ts
    tok = lambda w: pl.BlockSpec((ts, w), lambda b, i: (b * ns + i, 0))
    return pl.pallas_call(
        functools.partial(_ffn_kernel, ts=ts),
        grid=(bn, ns),
        in_specs=[tok(D_MODEL), tok(P_DIM)] + [_const_spec(a.shape) for a in (wup, cw, cb, wdn, wpg, wpp, lnw, lnb)],
        out_specs=tok(D_MODEL),
        out_shape=jax.ShapeDtypeStruct((bn * s, D_MODEL), _F32),
        scratch_shapes=[pltpu.VMEM((8, D_FF), _F32)],
        compiler_params=pltpu.CompilerParams(
            dimension_semantics=("parallel", "arbitrary"), vmem_limit_bytes=V7X_VMEM_LIMIT),
        name="ffn_ple_ln",
    )(h2, p2, wup, cw, cb, wdn, wpg, wpp, lnw, lnb)


def kernel(x, p, w_in, a_lambda, a_norm_w, ret_norm_w, gla_w_a2, gla_b_a, gla_norm_w, w_branch, w_out,
           ln1_w, ln1_b, w_ffn_up, ffn_conv_w, ffn_conv_b, w_ffn_down, w_ple_gate, w_ple_proj, ln2_w, ln2_b):
    bn, s, _ = x.shape
    t = bn * s
    slopes = [2.0 ** (-(8.0 / N_SOFTMAX_HEADS) * i) for i in range(1, N_SOFTMAX_HEADS + 1)]
    slopes_a, slopes_b = tuple(slopes[0::2]), tuple(slopes[1::2])
    row = lambda v: v.astype(_F32).reshape(1, -1)
    x2 = x.reshape(t, D_MODEL)
    for i in range(DEPTH):
        *proj_w, wg = _prep_weights(w_in, i)
        n32, n16, t16, t32 = _project(x2, *proj_w, bn, s)
        n32 = n32.reshape(bn, s, -1)
        n16 = n16.reshape(bn, s, -1)
        lam_init = 0.8 - 0.6 * math.exp(-0.3 * i)
        y_a = _diff_attention(a_lambda[i].astype(_F32), a_norm_w[i].astype(_F32).reshape(DA_V, 1),
                              n16, t16, bn, s, slopes_a, lam_init)
        y_b = _dsa_attention(n16, t16, t32, bn, s, slopes_b)
        y_c = _retention(row(ret_norm_w[i]), n16, n32, t32, bn, s)
        wa = jnp.pad(gla_w_a2[i], ((0, 128 - GLA_RANK), (0, 0))).astype(_BF16)
        y_d = _gla(wa, row(gla_b_a[i]), row(jnp.tile(gla_norm_w[i], GLA_HEADS)), n16, n32, t16, bn, s)
        ys = [y.reshape(t, BRANCH_W) for y in (y_a, y_b, y_c, y_d)]
        h2 = _merge(x2, ys, wg, w_branch[i].astype(_BF16), w_out[i].astype(_BF16), row(ln1_w[i]), row(ln1_b[i]))
        x2 = _ffn(h2, p[i].reshape(t, P_DIM), w_ffn_up[i].astype(_BF16), ffn_conv_w[i].astype(_F32),
                  row(ffn_conv_b[i]), w_ffn_down[i].astype(_BF16), w_ple_gate[i].astype(_BF16),
                  w_ple_proj[i].astype(_BF16), row(ln2_w[i]), row(ln2_b[i]), bn, s)
    return x2.reshape(bn, s, D_MODEL)
```

```python
import functools
import math

import numpy as np
import jax
import jax.numpy as jnp
from jax import lax
from jax.experimental import pallas as pl
from jax.experimental.pallas import tpu as pltpu

D_MODEL = 1024
DEPTH = 2
P_DIM = 256
N_BRANCH = 4
BRANCH_W = 256
DA_HEADS = 4
DA_QK = 32
DA_V = 64
DSA_HEADS = 4
DSA_HD = 64
IDX_HEADS = 4
IDX_HD = 32
TOPK_MAX = 256
RET_HEADS = 4
RET_QK = 64
RET_V = 64
GLA_HEADS = 4
GLA_QK = 32
GLA_V = 64
GLA_RANK = 16
GLA_GATE_TEMP = 16.0
D_FF = 2816
CONV_W = 3
N_SOFTMAX_HEADS = DA_HEADS + DSA_HEADS
LN_EPS = 1e-5
NEG_INF = -1e30
DEEPNORM_ALPHA = (2.0 * DEPTH) ** 0.25

IN_SIZES = (256, 256, 256, 256, 256, 256, 128, 32, 4, 256, 256, 256, 256, 128, 128, 256, 16, 256, 4096)
IN_NAMES = ("a_q", "a_k", "a_v", "b_q", "b_k", "b_v", "b_iq", "b_ik", "b_iw",
            "c_q", "c_k", "c_v", "c_g", "d_q", "d_k", "d_v", "d_a", "d_g", "m_g")

_BF16 = jnp.bfloat16
_F32 = jnp.float32
_INT_MIN = -2 ** 31
_LOG2E = math.log2(math.e)

CH = 256
TQ = 256
RET_C = 128
GLA_SUB = 16
COUNT_UNROLL = 4
TS_DENSE = 512
FF_COLS = 256
PREP_ROWS = 256
NEAR_CHUNKS = 4
SKIP_NATS = 64.0
V7X_VMEM_LIMIT = 56 * 1024 * 1024


def _dot(a, b, precision=None):
    return jnp.dot(a, b, preferred_element_type=_F32, precision=precision)


def _dot_nt(a, b):
    return lax.dot_general(a, b, (((1,), (1,)), ((), ())), preferred_element_type=_F32)


def _iota(shape, dim):
    return lax.broadcasted_iota(jnp.int32, shape, dim)


def _block_diag_tile(m_t, ngroups):
    r, tq = m_t.shape
    tiled = jnp.concatenate([m_t] * ngroups, axis=1)
    keep = (_iota(tiled.shape, 0) // (r // ngroups)) == (_iota(tiled.shape, 1) // tq)
    return jnp.where(keep, tiled, jnp.zeros_like(tiled))


def _layer_norm(x, w, b):
    mu = jnp.mean(x, -1, keepdims=True)
    var = jnp.mean(jnp.square(x - mu), -1, keepdims=True)
    return (x - mu) * lax.rsqrt(var + LN_EPS) * w + b


def _group_mean(x, group):
    lane_g = _iota(x.shape, 1) // group
    out = jnp.zeros_like(x)
    for h in range(x.shape[1] // group):
        mk = lane_g == h
        mh = jnp.sum(jnp.where(mk, x, 0.0), axis=1, keepdims=True) * (1.0 / group)
        out = jnp.where(mk, mh, out)
    return out


def _const_spec(shape):
    nd = len(shape)
    return pl.BlockSpec(shape, lambda *_: (0,) * nd, pipeline_mode=pl.Buffered(1))


def _prep_kernel(w_ref, wn32_ref, wn16_ref, wt16_ref, wt32_ref, wg_ref):
    offs = [0] + np.cumsum(IN_SIZES).tolist()
    col = {n: w_ref[0, :, offs[i]:offs[i + 1]] for i, n in enumerate(IN_NAMES)}
    zeros = lambda n: jnp.zeros((w_ref.shape[1], n), _F32)
    cat = lambda xs: jnp.concatenate(xs, axis=1)
    wn32_ref[...] = cat([col["c_g"], col["d_g"], col["d_q"], col["d_k"], col["d_a"],
                         zeros(128 - GLA_RANK)]).astype(_BF16)
    wn16_ref[...] = cat([col["a_k"], col["b_k"], col["c_q"], col["c_v"], col["d_v"]]
                        + [col["b_ik"]] * IDX_HEADS).astype(_BF16)
    wt16_ref[...] = cat([col["a_q"], col["a_v"], col["b_q"], col["b_v"], col["d_v"], col["b_iq"]]).T.astype(_BF16)
    wt32_ref[...] = cat([col["c_k"], col["b_iw"], zeros(16 - IDX_HEADS)]).T.astype(_BF16)
    wg_ref[...] = col["m_g"].astype(_BF16)


def _prep_weights(w_in, layer):
    rb = PREP_ROWS
    n32, n16, t16, t32, ng = 896, 1408, 1408, 272, N_BRANCH * D_MODEL
    rows = lambda n: pl.BlockSpec((rb, n), lambda r: (r, 0))
    cols = lambda n: pl.BlockSpec((n, rb), lambda r: (0, r))
    return pl.pallas_call(
        _prep_kernel,
        grid=(D_MODEL // rb,),
        in_specs=[pl.BlockSpec((1, rb, w_in.shape[2]), lambda r: (layer, r, 0))],
        out_specs=[rows(n32), rows(n16), cols(t16), cols(t32), rows(ng)],
        out_shape=[jax.ShapeDtypeStruct((D_MODEL, n32), _BF16), jax.ShapeDtypeStruct((D_MODEL, n16), _BF16),
                   jax.ShapeDtypeStruct((t16, D_MODEL), _BF16), jax.ShapeDtypeStruct((t32, D_MODEL), _BF16),
                   jax.ShapeDtypeStruct((D_MODEL, ng), _BF16)],
        compiler_params=pltpu.CompilerParams(
            dimension_semantics=("parallel",), vmem_limit_bytes=V7X_VMEM_LIMIT),
        name="prep_weights",
    )(w_in)


def _proj_kernel(x_ref, wn32_ref, wn16_ref, wt16_ref, wt32_ref, n32_ref, n16_ref, t16_ref, t32_ref):
    x = x_ref[...].astype(_BF16)
    n32_ref[...] = _dot(x, wn32_ref[...])
    n16_ref[...] = _dot(x, wn16_ref[...]).astype(_BF16)
    t16_ref[0, 0] = _dot_nt(wt16_ref[...], x).astype(_BF16)
    t32_ref[0, 0] = _dot_nt(wt32_ref[...], x)


def _project(x2, wn32, wn16, wt16, wt32, bn, s):
    ns = s // CH
    n32, n16, t16, t32 = wn32.shape[1], wn16.shape[1], wt16.shape[0], wt32.shape[0]
    return pl.pallas_call(
        _proj_kernel,
        grid=(bn, ns),
        in_specs=[
            pl.BlockSpec((CH, D_MODEL), lambda b, i: (b * ns + i, 0)),
            _const_spec(wn32.shape), _const_spec(wn16.shape), _const_spec(wt16.shape), _const_spec(wt32.shape),
        ],
        out_specs=[
            pl.BlockSpec((CH, n32), lambda b, i: (b * ns + i, 0)),
            pl.BlockSpec((CH, n16), lambda b, i: (b * ns + i, 0)),
            pl.BlockSpec((1, 1, t16, CH), lambda b, i: (b, i, 0, 0)),
            pl.BlockSpec((1, 1, t32, CH), lambda b, i: (b, i, 0, 0)),
        ],
        out_shape=[
            jax.ShapeDtypeStruct((bn * s, n32), _F32),
            jax.ShapeDtypeStruct((bn * s, n16), _BF16),
            jax.ShapeDtypeStruct((bn, ns, t16, CH), _BF16),
            jax.ShapeDtypeStruct((bn, ns, t32, CH), _F32),
        ],
        compiler_params=pltpu.CompilerParams(
            dimension_semantics=("parallel", "parallel"), vmem_limit_bytes=V7X_VMEM_LIMIT),
        name="proj",
    )(x2, wn32, wn16, wt16, wt32)


def _slope_row(slopes, reps, tq):
    return jnp.concatenate([jnp.full((1, tq), s * _LOG2E, _F32) for s in slopes for _ in range(reps)], axis=1)


def _alibi_rows(slopes):
    rows = _iota((CH, 128), 0).astype(_F32)
    return jnp.stack([rows * (s * _LOG2E) for s in slopes], axis=0)


def _key_norm_bound(k_ref, kn_ref, ngroups, nchunks):
    gt = jnp.where(_iota((16, 256), 0) == _iota((16, 256), 1) // (256 // ngroups), 1.0, 0.0).astype(_BF16)

    def body(j, best):
        kc = k_ref[0, pl.ds(pl.multiple_of(j * CH, CH), CH), :].astype(_F32)
        return jnp.maximum(best, _dot_nt(gt, (kc * kc).astype(_BF16)))

    best = lax.fori_loop(0, nchunks, body, jnp.zeros((16, CH), _F32))
    kn_ref[...] = jnp.broadcast_to(jnp.sqrt(jnp.max(best, axis=1, keepdims=True) * (1.0 + 2.0 ** -7)), kn_ref.shape)


def _first_chunks(qt, kn_ref, ngroups, scale, slopes, q0, nfull, nearest=0):
    q = qt.astype(_F32)
    tq = q.shape[1]
    qn = jnp.sqrt(jnp.max(jnp.sum((q * q).reshape(ngroups, 256 // ngroups, tq), axis=1), axis=1, keepdims=True))
    bound = scale * qn * kn_ref[0:ngroups, 0:1]
    per_head = ngroups // len(slopes)
    inv_slope = jnp.concatenate([jnp.full((per_head, 1), 1.0 / s, _F32) for s in slopes], axis=0)
    reach = ((2.0 * bound + SKIP_NATS) * inv_slope).astype(jnp.int32) + 2 + nearest
    last_far = q0 - CH + 1 - reach
    first = jnp.where(last_far < 0, 0, last_far // CH + 1)
    first = jnp.minimum(first, nfull)
    return [jnp.min(first[h * per_head:(h + 1) * per_head]) for h in range(len(slopes))]


def _attend(k_ref, vt_ref, bd_ref, ab_ref, slope_row, c1, q0, nfull, tq, w, mask_fn, aux0, first):
    g_tq = bd_ref.shape[1]
    nheads = 4

    def pv(j, p, heads):
        vt_c = vt_ref[0, j]
        return {h: _dot(vt_c[h * 64:(h + 1) * 64, :], p[:, h * w:(h + 1) * w]) for h in heads}

    def qk(j, heads):
        kc = k_ref[0, pl.ds(pl.multiple_of(j * CH, CH), CH), :]
        tiles = sorted({c0 // 256 for h in heads for c0 in range(h * w, (h + 1) * w, 128)})
        return {t: _dot(kc, bd_ref[:, t * 256:(t + 1) * 256]) for t in tiles}

    def softmax(s, j, m, l, aux, heads, diag):
        crow = slope_row * (j * CH - q0).astype(_F32)
        amask, aux = mask_fn(j, aux, diag)
        ps, ms, ls, alphas = [], [], [], []
        for c0 in range(0, g_tq, 128):
            cols = slice(c0, c0 + 128)
            if c0 // w not in heads:
                ps.append(jnp.zeros((CH, 128), _BF16))
                ms.append(m[:, cols])
                ls.append(l[:, cols])
                alphas.append(jnp.ones((1, 128), _F32))
                continue
            t = s[c0 // 256][:, c0 % 256:c0 % 256 + 128] * c1 + ab_ref[c0 // w]
            if amask is not None:
                t = t + amask[:, c0 % tq:c0 % tq + 128]
            m_new = jnp.maximum(m[:, cols], jnp.max(t, axis=0, keepdims=True) + crow[:, cols])
            alpha = jnp.exp2(m[:, cols] - m_new)
            p = jnp.exp2(t - (m_new - crow[:, cols]))
            ls.append(alpha * l[:, cols] + jnp.sum(p, axis=0, keepdims=True))
            ps.append(p.astype(_BF16))
            ms.append(m_new)
            alphas.append(alpha)
        cat = lambda xs: jnp.concatenate(xs, axis=1)
        return cat(ms), cat(ls), cat(alphas), cat(ps), aux

    def step(s, j, carry, heads, diag=False):
        p, m, l, acc, aux = carry
        pvs = pv(jnp.maximum(j - 1, 0), p, heads)
        m, l, alpha, p, aux = softmax(s, j, m, l, aux, heads, diag)
        acc = [alpha[:, h * w:(h + 1) * w] * (acc[h] + pvs[h]) if h in heads else acc[h] for h in range(nheads)]
        return p, m, l, acc, aux

    def unroll_of(heads):
        return 4 if len(heads) <= 2 else 2

    def run(lo, hi, heads, carry):
        unroll = unroll_of(heads)

        def group(i, c):
            j = lo + unroll * i
            ss = [qk(j + u, heads) for u in range(unroll)]
            for u in range(unroll):
                c = step(ss[u], j + u, c, heads)
            return c

        ngroup = jnp.maximum(hi - lo, 0) // unroll
        carry = lax.fori_loop(0, ngroup, group, carry)
        return lax.fori_loop(lo + unroll * ngroup, hi, lambda j, c: step(qk(j, heads), j, c, heads), carry)

    carry = (jnp.zeros((CH, g_tq), _BF16), jnp.full((1, g_tq), NEG_INF, _F32),
             jnp.zeros((1, g_tq), _F32), [jnp.zeros((64, w), _F32) for _ in range(nheads)], aux0)
    lows, hi = [], nfull
    for h in range(nheads):
        u = unroll_of(range(h, nheads))
        lo = jnp.maximum(hi - u * ((hi - jnp.minimum(first[h], hi) + u - 1) // u), 0)
        lows.append(lo)
        hi = lo
    for h in reversed(range(nheads)):
        carry = run(lows[h], lows[h - 1] if h else nfull, tuple(range(h, nheads)), carry)
    heads = tuple(range(nheads))
    p, m, l, acc, aux = step(qk(nfull, heads), nfull, carry, heads, True)
    pvs = pv(nfull, p, heads)
    return l, [acc[h] + pvs[h] for h in range(nheads)]


def _attn_a_kernel(lam_ref, nw_ref, qt_ref, k_ref, vt_ref, o_ref, bd_ref, ab_ref, kn_ref, *, tq, slopes, lam_init):
    g = 2 * DA_HEADS
    q0 = pl.program_id(1) * tq
    nfull = q0 // CH

    @pl.when(pl.program_id(1) == 0)
    def _():
        _key_norm_bound(k_ref, kn_ref, g, k_ref.shape[1] // CH)

    bd_ref[...] = _block_diag_tile(qt_ref[0, 0], g)
    ab_ref[...] = _alibi_rows(slopes)

    def mask(j, aux, diag):
        if not diag:
            return None, aux
        rel = _iota((CH, tq), 0) - _iota((CH, tq), 1)
        return jnp.where(rel <= q0 - j * CH, 0.0, NEG_INF), aux

    l, acc = _attend(k_ref, vt_ref, bd_ref, ab_ref, _slope_row(slopes, 2, tq), DA_QK ** -0.5 * _LOG2E,
                     q0, nfull, tq, 2 * tq, mask, jnp.zeros((1, tq), _F32),
                     _first_chunks(qt_ref[0, 0], kn_ref, g, DA_QK ** -0.5, slopes, q0, nfull))

    lp = lam_ref[...]
    lam = (jnp.exp(jnp.sum(lp[0:1] * lp[1:2], axis=1, keepdims=True))
           - jnp.exp(jnp.sum(lp[2:3] * lp[3:4], axis=1, keepdims=True)) + lam_init)
    linv = 1.0 / l
    outs = []
    for h in range(DA_HEADS):
        a = acc[h] * linv[:, h * 2 * tq:(h + 1) * 2 * tq]
        o = a[:, :tq] - lam * a[:, tq:]
        ms = jnp.mean(o * o, axis=0, keepdims=True)
        outs.append(o * lax.rsqrt(ms + LN_EPS) * nw_ref[...] * (1.0 - lam_init))
    o_ref[0] = jnp.concatenate(outs, axis=0).T.astype(_BF16)


def _diff_attention(lam_p, norm_w, n16, t16, bn, s, slopes, lam_init):
    tq = TQ
    ns, per = s // CH, CH // tq
    g = 2 * DA_HEADS
    kern = functools.partial(_attn_a_kernel, tq=tq, slopes=slopes, lam_init=lam_init)
    return pl.pallas_call(
        kern,
        grid=(bn, s // tq),
        in_specs=[
            _const_spec(lam_p.shape), _const_spec(norm_w.shape),
            pl.BlockSpec((1, 1, 256, tq), lambda b, i: (b, i // per, 0, i % per)),
            pl.BlockSpec((1, s, 256), lambda b, i: (b, 0, 0)),
            pl.BlockSpec((1, ns, 256, CH), lambda b, i: (b, 0, 1, 0)),
        ],
        out_specs=pl.BlockSpec((1, tq, 256), lambda b, i: (b, i, 0)),
        out_shape=jax.ShapeDtypeStruct((bn, s, 256), _BF16),
        scratch_shapes=[pltpu.VMEM((256, g * tq), _BF16), pltpu.VMEM((DA_HEADS, CH, 128), _F32),
                        pltpu.VMEM((16, 128), _F32)],
        compiler_params=pltpu.CompilerParams(
            dimension_semantics=("parallel", "arbitrary"), vmem_limit_bytes=V7X_VMEM_LIMIT),
        name="diff_attn",
    )(lam_p, norm_w, t16, n16, t16)


def _bit_planes(rows):
    a = list(rows)
    j, m = 16, 0x0000FFFF
    while j:
        k = 0
        while k < 32:
            t = (a[k] ^ lax.shift_right_logical(a[k + j], jnp.int32(j))) & jnp.int32(m)
            a[k] = a[k] ^ t
            a[k + j] = a[k + j] ^ (t << j)
            k = (k + j + 1) & ~j
        j >>= 1
        if j:
            m = (m ^ (m << j)) & 0xFFFFFFFF
            m = m - (1 << 32) if m >= (1 << 31) else m
    return a[::-1]


def _dsa_kernel(iqt_ref, ik_ref, iwt_ref, qt_ref, k_ref, vt_ref, o_ref,
                iqbd_ref, bd_ref, key_ref, planes_ref, alive_ref, ab_ref, kn_ref, *, tq, slopes, topk):
    g = DSA_HEADS
    q0 = pl.program_id(1) * tq
    nfull = q0 // CH

    @pl.when(pl.program_id(1) == 0)
    def _():
        _key_norm_bound(k_ref, kn_ref, g, k_ref.shape[1] // CH)

    ngrp = (nfull + COUNT_UNROLL) // COUNT_UNROLL
    iqbd_ref[...] = _block_diag_tile(iqt_ref[0, 0], IDX_HEADS)
    bd_ref[...] = _block_diag_tile(qt_ref[0, 0], g)
    ab_ref[...] = _alibi_rows(slopes)
    w = iwt_ref[0, 0][0:IDX_HEADS, :] * (IDX_HEADS ** -0.5 * IDX_HD ** -0.5)

    def logits(j):
        return _dot(ik_ref[0, pl.ds(pl.multiple_of(j * CH, CH), CH), :], iqbd_ref[...])

    def score(lg, j, diag):
        half = CH // 2
        rows = []
        for r0 in (0, half):
            sc = jnp.maximum(lg[r0:r0 + half, 0:tq], 0.0) * w[0:1]
            for h in range(1, IDX_HEADS):
                sc = sc + jnp.maximum(lg[r0:r0 + half, h * tq:(h + 1) * tq], 0.0) * w[h:h + 1]
            sc = jnp.where(sc == 0.0, 0.0, sc)
            bits = pltpu.bitcast(sc, jnp.int32)
            key = bits ^ ((bits >> 31) & 0x7FFFFFFF)
            if diag:
                rel = _iota(key.shape, 0) - _iota(key.shape, 1)
                key = jnp.where(rel <= q0 - j * CH - r0, key, _INT_MIN)
            key_ref[j, r0:r0 + half, :] = key
            key3 = key.reshape(half // 8, 8, tq)
            rows += [key3[v] for v in range(half // 8)]
        planes = _bit_planes(rows)
        planes[31] = ~planes[31]
        planes_ref[j] = jnp.stack(planes, axis=0)
        if diag:
            lim = q0 - j * CH + _iota((8, tq), 1) - _iota((8, tq), 0)
            nbits = jnp.clip((lim >> 3) + 1, 0, 32)
            alive_ref[j] = jnp.where(nbits == 0, 0, jnp.left_shift(jnp.int32(-1), 32 - jnp.maximum(nbits, 1)))
        else:
            alive_ref[j] = jnp.full((8, tq), -1, jnp.int32)

    def score_pair(i, lgs):
        nxt = (logits(2 * i + 2), logits(jnp.minimum(2 * i + 3, nfull)))
        score(lgs[0], 2 * i, False)
        score(lgs[1], 2 * i + 1, False)
        return nxt

    npair = nfull // 2
    lgs = lax.fori_loop(0, npair, score_pair, (logits(0), logits(jnp.minimum(1, nfull))))
    score(lgs[0], 2 * npair, True)

    @pl.when(nfull > 2 * npair)
    def _():
        score(lgs[1], nfull, True)

    for u in range(1, COUNT_UNROLL):
        @pl.when(nfull + u < ngrp * COUNT_UNROLL)
        def _():
            planes_ref[nfull + u] = jnp.zeros((32, 8, tq), jnp.int32)
            alive_ref[nfull + u] = jnp.zeros((8, tq), jnp.int32)

    def sweep(b_upd, keep, b_cnt):
        def body(gi, acc8):
            for u in range(COUNT_UNROLL):
                j = gi * COUNT_UNROLL + u
                a = alive_ref[j]
                if b_upd is not None:
                    a = a & ~(planes_ref[j, b_upd] ^ keep)
                    alive_ref[j] = a
                acc8 = acc8 + lax.population_count(a if b_cnt is None else a & planes_ref[j, b_cnt])
            return acc8
        acc8 = lax.fori_loop(0, ngrp, body, jnp.zeros((8, tq), jnp.int32))
        return jnp.sum(acc8, axis=0, keepdims=True)

    def decide(b, ones, want, thr):
        take = ones >= want
        thr = jnp.where(take, thr | jnp.left_shift(jnp.int32(1), b), thr)
        return jnp.where(take, want, want - ones), thr, jnp.where(take, -1, 0)

    want, thr, keep = decide(31, sweep(None, None, 31), jnp.full((1, tq), topk, jnp.int32),
                             jnp.zeros((1, tq), jnp.int32))

    def bit_body(i, c):
        want, thr, keep = c
        b = 30 - i
        return decide(b, sweep(b + 1, keep, b), want, thr)

    want, thr, keep = lax.fori_loop(0, 31, bit_body, (want, thr, keep))
    ties = sweep(0, keep, None)
    thr = thr ^ _INT_MIN
    need_tie = jnp.max(jnp.where((ties > want) & (thr > _INT_MIN), 1, 0)) > 0

    def attend(tie):
        if tie:
            first = [jnp.int32(0)] * g
            room = want.astype(_F32)
            lower = jnp.where(_iota((CH, CH), 0) > _iota((CH, CH), 1), 1.0, 0.0).astype(_BF16)
        else:
            thr_eff = jnp.maximum(thr, _INT_MIN + 1)

            def near_body(j, best):
                pos = jnp.where(key_ref[j] >= thr_eff, _iota((CH, tq), 0) + j * CH, -1)
                return jnp.maximum(best, jnp.max(pos, axis=0, keepdims=True))

            best = lax.fori_loop(jnp.maximum(nfull - (NEAR_CHUNKS - 1), 0), nfull + 1, near_body,
                                 jnp.full((1, tq), -1, jnp.int32))
            gap = jnp.where(best >= 0, q0 + _iota((1, tq), 1) - best, 2 ** 24)
            first = _first_chunks(qt_ref[0, 0], kn_ref, g, DSA_HD ** -0.5, slopes, q0, nfull,
                                  jnp.max(gap, axis=1, keepdims=True))

        def mask(j, seen, diag):
            key = key_ref[j]
            if tie:
                eq = key == thr
                eqf = jnp.where(eq, 1.0, 0.0)
                rank = _dot(lower, eqf.astype(_BF16)) + seen
                sel = ((key > thr) | (eq & (rank < room))) & (key > _INT_MIN)
                seen = seen + jnp.sum(eqf, axis=0, keepdims=True)
            else:
                sel = key >= thr_eff
            return jnp.where(sel, 0.0, NEG_INF), seen

        l, acc = _attend(k_ref, vt_ref, bd_ref, ab_ref, _slope_row(slopes, 1, tq), DSA_HD ** -0.5 * _LOG2E,
                         q0, nfull, tq, tq, mask, jnp.zeros((1, tq), _F32), first)
        linv = 1.0 / l
        outs = [acc[h] * linv[:, h * tq:(h + 1) * tq] for h in range(g)]
        o_ref[0] = jnp.concatenate(outs, axis=0).T.astype(_BF16)

    @pl.when(need_tie)
    def _():
        attend(True)

    @pl.when(jnp.logical_not(need_tie))
    def _():
        attend(False)


def _dsa_attention(n16, t16, t32, bn, s, slopes):
    tq = TQ
    ns, per = s // CH, CH // tq
    g = DSA_HEADS
    topk = min(TOPK_MAX, s // 4)
    assert ns % COUNT_UNROLL == 0
    kern = functools.partial(_dsa_kernel, tq=tq, slopes=slopes, topk=topk)
    return pl.pallas_call(
        kern,
        grid=(bn, s // tq),
        in_specs=[
            pl.BlockSpec((1, 1, 128, tq), lambda b, i: (b, i // per, 10, i % per)),
            pl.BlockSpec((1, s, 128), lambda b, i: (b, 0, 10)),
            pl.BlockSpec((1, 1, 16, tq), lambda b, i: (b, i // per, 16, i % per)),
            pl.BlockSpec((1, 1, 256, tq), lambda b, i: (b, i // per, 2, i % per)),
            pl.BlockSpec((1, s, 256), lambda b, i: (b, 0, 1)),
            pl.BlockSpec((1, ns, 256, CH), lambda b, i: (b, 0, 3, 0)),
        ],
        out_specs=pl.BlockSpec((1, tq, 256), lambda b, i: (b, i, 0)),
        out_shape=jax.ShapeDtypeStruct((bn, s, 256), _BF16),
        scratch_shapes=[
            pltpu.VMEM((128, IDX_HEADS * tq), _BF16), pltpu.VMEM((256, g * tq), _BF16),
            pltpu.VMEM((ns, CH, tq), jnp.int32), pltpu.VMEM((ns, 32, 8, tq), jnp.int32),
            pltpu.VMEM((ns, 8, tq), jnp.int32), pltpu.VMEM((DSA_HEADS, CH, 128), _F32),
            pltpu.VMEM((16, 128), _F32),
        ],
        compiler_params=pltpu.CompilerParams(
            dimension_semantics=("parallel", "arbitrary"), vmem_limit_bytes=V7X_VMEM_LIMIT),
        name="dsa_attn",
    )(t16, n16, t32, t16, n16, t16)


def _ret_kernel(q_ref, kt_ref, v_ref, g_ref, intra_ref, qdec_ref, kdect_ref, cd_ref, nw_ref, o_ref, s_ref, *, c):
    @pl.when(pl.program_id(1) == 0)
    def _():
        s_ref[...] = jnp.zeros(s_ref.shape, _F32)

    q = q_ref[0]
    v = v_ref[0]
    kt = kt_ref[0, 0] * (RET_QK ** -0.5)
    att = _dot(q, _block_diag_tile(kt.astype(_BF16), RET_HEADS)) * intra_ref[...]
    vt = jnp.concatenate([v] * RET_HEADS, axis=0)
    vbd = jnp.where((_iota(vt.shape, 0) // c) == (_iota(vt.shape, 1) // RET_V), vt, jnp.zeros_like(vt))
    st = s_ref[...]
    o = _dot(att.astype(_BF16), vbd) + _dot(q, st.astype(_BF16)) * qdec_ref[...]
    upd = _dot((kt * kdect_ref[...]).astype(_BF16), v)
    same_head = (_iota(upd.shape, 0) // RET_QK) == (_iota(upd.shape, 1) // RET_V)
    s_ref[...] = st * cd_ref[...] + jnp.where(same_head, upd, 0.0)

    mu = _group_mean(o, RET_V)
    d = o - mu
    var = _group_mean(d * d, RET_V)
    y = d * lax.rsqrt(var + LN_EPS) * nw_ref[...]
    gate = g_ref[0]
    o_ref[0] = (gate * jax.nn.sigmoid(gate) * y).astype(_BF16)


def _retention_consts(c):
    h = RET_HEADS
    log_g = np.log1p(-np.power(2.0, -5.0 - np.arange(h, dtype=np.float64)))
    pos = np.arange(c, dtype=np.float64)
    rel = pos[:, None] - pos[None, :]
    intra = np.where(rel >= 0, np.exp(log_g[:, None, None] * np.maximum(rel, 0.0)), 0.0)
    intra = np.transpose(intra, (1, 0, 2)).reshape(c, h * c)
    qdec = np.repeat(np.exp(log_g[:, None] * (pos[None, :] + 1.0)).T, RET_V, axis=1)
    kdect = np.repeat(np.exp(log_g[:, None] * (c - 1.0 - pos[None, :])), RET_QK, axis=0)
    cd = np.repeat(np.exp(log_g * c), RET_QK)[:, None] * np.ones((1, h * RET_V))
    return tuple(jnp.asarray(a, _F32) for a in (intra, qdec, kdect, cd))


def _retention(norm_w, n16, n32, t32, bn, s):
    c = RET_C
    per = CH // c
    intra, qdec, kdect, cd = _retention_consts(c)
    return pl.pallas_call(
        functools.partial(_ret_kernel, c=c),
        grid=(bn, s // c),
        in_specs=[
            pl.BlockSpec((1, c, 256), lambda b, i: (b, i, 2)),
            pl.BlockSpec((1, 1, 256, c), lambda b, i: (b, i // per, 0, i % per)),
            pl.BlockSpec((1, c, 256), lambda b, i: (b, i, 3)),
            pl.BlockSpec((1, c, 256), lambda b, i: (b, i, 0)),
            _const_spec(intra.shape), _const_spec(qdec.shape), _const_spec(kdect.shape),
            _const_spec(cd.shape), _const_spec(norm_w.shape),
        ],
        out_specs=pl.BlockSpec((1, c, 256), lambda b, i: (b, i, 0)),
        out_shape=jax.ShapeDtypeStruct((bn, s, 256), _BF16),
        scratch_shapes=[pltpu.VMEM((RET_HEADS * RET_QK, RET_HEADS * RET_V), _F32)],
        compiler_params=pltpu.CompilerParams(dimension_semantics=("parallel", "arbitrary")),
        name="retention",
    )(n16, t32, n16, n32, intra, qdec, kdect, cd, norm_w)


def _gla_kernel(q_ref, k_ref, a_ref, v_ref, vt_ref, g_ref, wa_ref, ba_ref, nw_ref, o_ref,
                st_ref, u_ref, oacc_ref, qs_ref, kk_ref, b_ref, qh_ref, dec_ref, *, ts, sub):
    nsub = ts // sub
    assert sub == 16

    @pl.when(pl.program_id(1) == 0)
    def _():
        st_ref[...] = jnp.zeros(st_ref.shape, _F32)

    la = jax.nn.log_sigmoid(_dot(a_ref[0].astype(_BF16), wa_ref[...]) + ba_ref[...]) * (1.0 / GLA_GATE_TEMP)
    in_blk = _iota(la.shape, 0) % sub
    b = la
    for sh in (1, 2, 4, 8):
        b = b + jnp.where(in_blk >= sh, pltpu.roll(b, sh, 0), 0.0)
    bl = jnp.where(in_blk == sub - 1, b, 0.0)
    for sh in (1, 2, 4, 8):
        bl = bl + jnp.where(in_blk < sub - sh, pltpu.roll(bl, ts - sh, 0), 0.0)
    qs = q_ref[0] * (GLA_QK ** -0.5)
    kk = k_ref[0]
    kd = (kk * jnp.exp(bl - b)).astype(_BF16)
    qs_ref[...] = qs
    kk_ref[...] = kk
    b_ref[...] = b
    qh_ref[...] = (qs * jnp.exp(b)).astype(_BF16)
    dec_ref[...] = jnp.exp(bl)
    vt = vt_ref[0, 0]
    row_blk = _iota(kd.shape, 0) // sub
    kd_wide = jnp.concatenate([jnp.where(row_blk == n, kd, jnp.zeros_like(kd)) for n in range(nsub)], axis=1)
    u_all = _dot(vt, kd_wide)
    for n in range(nsub):
        u_ref[n] = u_all[:, n * 128:(n + 1) * 128]

    st_keep = (_iota(st_ref.shape, 0) // GLA_V) == (_iota(st_ref.shape, 1) // GLA_QK)
    spread = jnp.where((_iota((128, 256), 0) // GLA_QK) == (_iota((128, 256), 1) // GLA_V), 1.0, 0.0).astype(_BF16)
    row16 = _iota((sub, 128), 0)

    def body(n, carry):
        r0 = pl.multiple_of(n * sub, sub)
        st = st_ref[...]
        o_cross = _dot_nt(qh_ref[pl.ds(r0, sub), :], st.astype(_BF16))
        q16 = qs_ref[pl.ds(r0, sub), :]
        k16 = kk_ref[pl.ds(r0, sub), :]
        b16 = b_ref[pl.ds(r0, sub), :]
        v16 = v_ref[0, pl.ds(r0, sub), :].astype(_F32)
        es = []
        for j in range(sub):
            e = q16 * k16[j:j + 1] * jnp.exp(jnp.minimum(b16 - b16[j:j + 1], 0.0))
            es.append(jnp.where(row16 >= j, e, 0.0))
        e_all = jnp.concatenate(es, axis=0)
        e_hi = e_all.astype(_BF16)
        e_lo = (e_all - e_hi.astype(_F32)).astype(_BF16)
        att = _dot(e_hi, spread) + _dot(e_lo, spread)
        o_diag = att[0:sub] * v16[0:1]
        for j in range(1, sub):
            o_diag = o_diag + att[j * sub:(j + 1) * sub] * v16[j:j + 1]
        oacc_ref[pl.ds(r0, sub), :] = o_cross + o_diag
        st_ref[...] = st * dec_ref[pl.ds(r0, 1), :] + jnp.where(st_keep, u_ref[n], 0.0)
        return carry

    lax.fori_loop(0, nsub, body, 0, unroll=2)

    o = oacc_ref[...]
    ms = _group_mean(o * o, GLA_V)
    y = o * lax.rsqrt(ms + LN_EPS) * nw_ref[...]
    gate = g_ref[0]
    o_ref[0] = (gate * jax.nn.sigmoid(gate) * y).astype(_BF16)


def _gla(wa, ba, norm_w, n16, n32, t16, bn, s):
    ts, sub = CH, GLA_SUB
    return pl.pallas_call(
        functools.partial(_gla_kernel, ts=ts, sub=sub),
        grid=(bn, s // ts),
        in_specs=[
            pl.BlockSpec((1, ts, 128), lambda b, i: (b, i, 4)),
            pl.BlockSpec((1, ts, 128), lambda b, i: (b, i, 5)),
            pl.BlockSpec((1, ts, 128), lambda b, i: (b, i, 6)),
            pl.BlockSpec((1, ts, 256), lambda b, i: (b, i, 4)),
            pl.BlockSpec((1, 1, 256, ts), lambda b, i: (b, i, 4, 0)),
            pl.BlockSpec((1, ts, 256), lambda b, i: (b, i, 1)),
            _const_spec(wa.shape), _const_spec(ba.shape), _const_spec(norm_w.shape),
        ],
        out_specs=pl.BlockSpec((1, ts, 256), lambda b, i: (b, i, 0)),
        out_shape=jax.ShapeDtypeStruct((bn, s, 256), _BF16),
        scratch_shapes=[
            pltpu.VMEM((GLA_HEADS * GLA_V, GLA_HEADS * GLA_QK), _F32),
            pltpu.VMEM((ts // sub, GLA_HEADS * GLA_V, GLA_HEADS * GLA_QK), _F32),
            pltpu.VMEM((ts, 256), _F32),
            pltpu.VMEM((ts, 128), _F32), pltpu.VMEM((ts, 128), _F32), pltpu.VMEM((ts, 128), _F32),
            pltpu.VMEM((ts, 128), _BF16), pltpu.VMEM((ts, 128), _F32),
        ],
        compiler_params=pltpu.CompilerParams(dimension_semantics=("parallel", "arbitrary")),
        name="gla",
    )(n32, n32, n32, n16, t16, n32, wa, ba, norm_w)


def _merge_kernel(x_ref, ya_ref, yb_ref, yc_ref, yd_ref, wg_ref, wbr_ref, wout_ref, lnw_ref, lnb_ref, h_ref):
    x = x_ref[...]
    xb = x.astype(_BF16)
    merged = None
    for n, y_ref in enumerate((ya_ref, yb_ref, yc_ref, yd_ref)):
        gate = jax.nn.sigmoid(_dot(xb, wg_ref[:, n * D_MODEL:(n + 1) * D_MODEL]))
        term = gate * _dot(y_ref[...], wbr_ref[n])
        merged = term if merged is None else merged + term
    mix = _dot(merged.astype(_BF16), wout_ref[...])
    h_ref[...] = _layer_norm(DEEPNORM_ALPHA * x + mix, lnw_ref[...], lnb_ref[...])


def _merge(x2, ys, wg, wbr, wout, lnw, lnb):
    t = x2.shape[0]
    ts = TS_DENSE
    tok = lambda w: pl.BlockSpec((ts, w), lambda i: (i, 0))
    return pl.pallas_call(
        _merge_kernel,
        grid=(t // ts,),
        in_specs=[tok(D_MODEL)] + [tok(BRANCH_W)] * 4 + [
            _const_spec(wg.shape), _const_spec(wbr.shape), _const_spec(wout.shape),
            _const_spec(lnw.shape), _const_spec(lnb.shape)],
        out_specs=tok(D_MODEL),
        out_shape=jax.ShapeDtypeStruct((t, D_MODEL), _F32),
        compiler_params=pltpu.CompilerParams(
            dimension_semantics=("parallel",), vmem_limit_bytes=V7X_VMEM_LIMIT),
        name="merge_ln",
    )(x2, *ys, wg, wbr, wout, lnw, lnb)


def _ffn_kernel(h_ref, p_ref, wup_ref, cw_ref, cb_ref, wdn_ref, wpg_ref, wpp_ref, lnw_ref, lnb_ref,
                o_ref, tail_ref, *, ts):
    @pl.when(pl.program_id(1) == 0)
    def _():
        tail_ref[...] = jnp.zeros(tail_ref.shape, _F32)

    h = h_ref[...]
    hb = h.astype(_BF16)
    row = _iota((ts, FF_COLS), 0)
    f = None
    for c0 in range(0, D_FF, FF_COLS):
        cols = slice(c0, c0 + FF_COLS)
        u = _dot(hb, wup_ref[:, cols])
        gt = _dot(hb, wup_ref[:, D_FF + c0:D_FF + c0 + FF_COLS])
        prev = tail_ref[:, cols]
        g1 = jnp.where(row == 0, prev[7:8], pltpu.roll(gt, 1, 0))
        g2 = jnp.where(row == 0, prev[6:7], jnp.where(row == 1, prev[7:8], pltpu.roll(gt, 2, 0)))
        tail_ref[:, cols] = gt[ts - 8:ts]
        gc = cb_ref[:, cols] + cw_ref[0:1, cols] * g2
        gc = gc + cw_ref[1:2, cols] * g1
        gc = gc + cw_ref[2:3, cols] * gt
        term = _dot((jax.nn.gelu(gc) * u).astype(_BF16), wdn_ref[cols, :])
        f = term if f is None else f + term
    e = jax.nn.sigmoid(_dot(hb, wpg_ref[...])) * _dot(p_ref[...].astype(_BF16), wpp_ref[...])
    o_ref[...] = _layer_norm(DEEPNORM_ALPHA * h + f + e, lnw_ref[...], lnb_ref[...])


def _ffn(h2, p2, wup, cw, cb, wdn, wpg, wpp, lnw, lnb, bn, s):
    ts = TS_DENSE
    ns = s // ts
    tok = lambda w: pl.BlockSpec((ts, w), lambda b, i: (b * ns + i, 0))
    return pl.pallas_call(
        functools.partial(_ffn_kernel, ts=ts),
        grid=(bn, ns),
        in_specs=[tok(D_MODEL), tok(P_DIM)] + [_const_spec(a.shape) for a in (wup, cw, cb, wdn, wpg, wpp, lnw, lnb)],
        out_specs=tok(D_MODEL),
        out_shape=jax.ShapeDtypeStruct((bn * s, D_MODEL), _F32),
        scratch_shapes=[pltpu.VMEM((8, D_FF), _F32)],
        compiler_params=pltpu.CompilerParams(
            dimension_semantics=("parallel", "arbitrary"), vmem_limit_bytes=V7X_VMEM_LIMIT),
        name="ffn_ple_ln",
    )(h2, p2, wup, cw, cb, wdn, wpg, wpp, lnw, lnb)


def kernel(x, p, w_in, a_lambda, a_norm_w, ret_norm_w, gla_w_a2, gla_b_a, gla_norm_w, w_branch, w_out,
           ln1_w, ln1_b, w_ffn_up, ffn_conv_w, ffn_conv_b, w_ffn_down, w_ple_gate, w_ple_proj, ln2_w, ln2_b):
    bn, s, _ = x.shape
    t = bn * s
    slopes = [2.0 ** (-(8.0 / N_SOFTMAX_HEADS) * i) for i in range(1, N_SOFTMAX_HEADS + 1)]
    slopes_a, slopes_b = tuple(slopes[0::2]), tuple(slopes[1::2])
    row = lambda v: v.astype(_F32).reshape(1, -1)
    x2 = x.reshape(t, D_MODEL)
    for i in range(DEPTH):
        *proj_w, wg = _prep_weights(w_in, i)
        n32, n16, t16, t32 = _project(x2, *proj_w, bn, s)
        n32 = n32.reshape(bn, s, -1)
        n16 = n16.reshape(bn, s, -1)
        lam_init = 0.8 - 0.6 * math.exp(-0.3 * i)
        y_a = _diff_attention(a_lambda[i].astype(_F32), a_norm_w[i].astype(_F32).reshape(DA_V, 1),
                              n16, t16, bn, s, slopes_a, lam_init)
        y_b = _dsa_attention(n16, t16, t32, bn, s, slopes_b)
        y_c = _retention(row(ret_norm_w[i]), n16, n32, t32, bn, s)
        wa = jnp.pad(gla_w_a2[i], ((0, 128 - GLA_RANK), (0, 0))).astype(_BF16)
        y_d = _gla(wa, row(gla_b_a[i]), row(jnp.tile(gla_norm_w[i], GLA_HEADS)), n16, n32, t16, bn, s)
        ys = [y.reshape(t, BRANCH_W) for y in (y_a, y_b, y_c, y_d)]
        h2 = _merge(x2, ys, wg, w_branch[i].astype(_BF16), w_out[i].astype(_BF16), row(ln1_w[i]), row(ln1_b[i]))
        x2 = _ffn(h2, p[i].reshape(t, P_DIM), w_ffn_up[i].astype(_BF16), ffn_conv_w[i].astype(_F32),
                  row(ffn_conv_b[i]), w_ffn_down[i].astype(_BF16), w_ple_gate[i].astype(_BF16),
                  w_ple_proj[i].astype(_BF16), row(ln2_w[i]), row(ln2_b[i]), bn, s)
    return x2.reshape(bn, s, D_MODEL)
```

```python
import functools
import math

import numpy as np
import jax
import jax.numpy as jnp
from jax import lax
from jax.experimental import pallas as pl
from jax.experimental.pallas import tpu as pltpu

D_MODEL = 1024
DEPTH = 2
P_DIM = 256
N_BRANCH = 4
BRANCH_W = 256
DA_HEADS = 4
DA_QK = 32
DA_V = 64
DSA_HEADS = 4
DSA_HD = 64
IDX_HEADS = 4
IDX_HD = 32
TOPK_MAX = 256
RET_HEADS = 4
RET_QK = 64
RET_V = 64
GLA_HEADS = 4
GLA_QK = 32
GLA_V = 64
GLA_RANK = 16
GLA_GATE_TEMP = 16.0
D_FF = 2816
CONV_W = 3
N_SOFTMAX_HEADS = DA_HEADS + DSA_HEADS
LN_EPS = 1e-5
NEG_INF = -1e30
DEEPNORM_ALPHA = (2.0 * DEPTH) ** 0.25

IN_SIZES = (256, 256, 256, 256, 256, 256, 128, 32, 4, 256, 256, 256, 256, 128, 128, 256, 16, 256, 4096)
IN_NAMES = ("a_q", "a_k", "a_v", "b_q", "b_k", "b_v", "b_iq", "b_ik", "b_iw",
            "c_q", "c_k", "c_v", "c_g", "d_q", "d_k", "d_v", "d_a", "d_g", "m_g")

_BF16 = jnp.bfloat16
_F32 = jnp.float32
_INT_MIN = -2 ** 31
_LOG2E = math.log2(math.e)

CH = 256
TQ = 256
RET_C = 128
GLA_SUB = 16
COUNT_UNROLL = 4
SCORE_UNROLL = 4
ATTEND_UNROLL = 4
TS_DENSE = 512
FF_COLS = 256
PREP_ROWS = 256
NEAR_CHUNKS = 4
SKIP_NATS = 64.0
V7X_VMEM_LIMIT = 56 * 1024 * 1024


def _dot(a, b, precision=None):
    return jnp.dot(a, b, preferred_element_type=_F32, precision=precision)


def _dot_nt(a, b):
    return lax.dot_general(a, b, (((1,), (1,)), ((), ())), preferred_element_type=_F32)


def _iota(shape, dim):
    return lax.broadcasted_iota(jnp.int32, shape, dim)


def _block_diag_tile(m_t, ngroups):
    r, tq = m_t.shape
    tiled = jnp.concatenate([m_t] * ngroups, axis=1)
    keep = (_iota(tiled.shape, 0) // (r // ngroups)) == (_iota(tiled.shape, 1) // tq)
    return jnp.where(keep, tiled, jnp.zeros_like(tiled))


def _layer_norm(x, w, b):
    mu = jnp.mean(x, -1, keepdims=True)
    var = jnp.mean(jnp.square(x - mu), -1, keepdims=True)
    return (x - mu) * lax.rsqrt(var + LN_EPS) * w + b


def _group_mean(x, group):
    lane_g = _iota(x.shape, 1) // group
    out = jnp.zeros_like(x)
    for h in range(x.shape[1] // group):
        mk = lane_g == h
        mh = jnp.sum(jnp.where(mk, x, 0.0), axis=1, keepdims=True) * (1.0 / group)
        out = jnp.where(mk, mh, out)
    return out


def _const_spec(shape):
    nd = len(shape)
    return pl.BlockSpec(shape, lambda *_: (0,) * nd, pipeline_mode=pl.Buffered(1))


def _prep_kernel(w_ref, wn32_ref, wn16_ref, wt16_ref, wt32_ref, wg_ref):
    offs = [0] + np.cumsum(IN_SIZES).tolist()
    col = {n: w_ref[0, :, offs[i]:offs[i + 1]] for i, n in enumerate(IN_NAMES)}
    zeros = lambda n: jnp.zeros((w_ref.shape[1], n), _F32)
    cat = lambda xs: jnp.concatenate(xs, axis=1)
    wn32_ref[...] = cat([col["c_g"], col["d_g"], col["d_q"], col["d_k"], col["d_a"],
                         zeros(128 - GLA_RANK)]).astype(_BF16)
    wn16_ref[...] = cat([col["a_k"], col["b_k"], col["c_q"], col["c_v"], col["d_v"]]
                        + [col["b_ik"]] * IDX_HEADS).astype(_BF16)
    wt16_ref[...] = cat([col["a_q"], col["a_v"], col["b_q"], col["b_v"], col["d_v"], col["b_iq"]]).T.astype(_BF16)
    wt32_ref[...] = cat([col["c_k"], col["b_iw"], zeros(16 - IDX_HEADS)]).T.astype(_BF16)
    wg_ref[...] = col["m_g"].astype(_BF16)


def _prep_weights(w_in, layer):
    rb = PREP_ROWS
    n32, n16, t16, t32, ng = 896, 1408, 1408, 272, N_BRANCH * D_MODEL
    rows = lambda n: pl.BlockSpec((rb, n), lambda r: (r, 0))
    cols = lambda n: pl.BlockSpec((n, rb), lambda r: (0, r))
    return pl.pallas_call(
        _prep_kernel,
        grid=(D_MODEL // rb,),
        in_specs=[pl.BlockSpec((1, rb, w_in.shape[2]), lambda r: (layer, r, 0))],
        out_specs=[rows(n32), rows(n16), cols(t16), cols(t32), rows(ng)],
        out_shape=[jax.ShapeDtypeStruct((D_MODEL, n32), _BF16), jax.ShapeDtypeStruct((D_MODEL, n16), _BF16),
                   jax.ShapeDtypeStruct((t16, D_MODEL), _BF16), jax.ShapeDtypeStruct((t32, D_MODEL), _BF16),
                   jax.ShapeDtypeStruct((D_MODEL, ng), _BF16)],
        compiler_params=pltpu.CompilerParams(
            dimension_semantics=("parallel",), vmem_limit_bytes=V7X_VMEM_LIMIT),
        name="prep_weights",
    )(w_in)


def _proj_kernel(x_ref, wn32_ref, wn16_ref, wt16_ref, wt32_ref, n32_ref, n16_ref, t16_ref, t32_ref):
    x = x_ref[...].astype(_BF16)
    n32_ref[...] = _dot(x, wn32_ref[...])
    n16_ref[...] = _dot(x, wn16_ref[...]).astype(_BF16)
    t16_ref[0, 0] = _dot_nt(wt16_ref[...], x).astype(_BF16)
    t32_ref[0, 0] = _dot_nt(wt32_ref[...], x)


def _project(x2, wn32, wn16, wt16, wt32, bn, s):
    ns = s // CH
    n32, n16, t16, t32 = wn32.shape[1], wn16.shape[1], wt16.shape[0], wt32.shape[0]
    return pl.pallas_call(
        _proj_kernel,
        grid=(bn, ns),
        in_specs=[
            pl.BlockSpec((CH, D_MODEL), lambda b, i: (b * ns + i, 0)),
            _const_spec(wn32.shape), _const_spec(wn16.shape), _const_spec(wt16.shape), _const_spec(wt32.shape),
        ],
        out_specs=[
            pl.BlockSpec((CH, n32), lambda b, i: (b * ns + i, 0)),
            pl.BlockSpec((CH, n16), lambda b, i: (b * ns + i, 0)),
            pl.BlockSpec((1, 1, t16, CH), lambda b, i: (b, i, 0, 0)),
            pl.BlockSpec((1, 1, t32, CH), lambda b, i: (b, i, 0, 0)),
        ],
        out_shape=[
            jax.ShapeDtypeStruct((bn * s, n32), _F32),
            jax.ShapeDtypeStruct((bn * s, n16), _BF16),
            jax.ShapeDtypeStruct((bn, ns, t16, CH), _BF16),
            jax.ShapeDtypeStruct((bn, ns, t32, CH), _F32),
        ],
        compiler_params=pltpu.CompilerParams(
            dimension_semantics=("parallel", "parallel"), vmem_limit_bytes=V7X_VMEM_LIMIT),
        name="proj",
    )(x2, wn32, wn16, wt16, wt32)


def _slope_row(slopes, reps, tq):
    return jnp.concatenate([jnp.full((1, tq), s * _LOG2E, _F32) for s in slopes for _ in range(reps)], axis=1)


def _alibi_rows(slopes):
    rows = _iota((CH, 128), 0).astype(_F32)
    return jnp.stack([rows * (s * _LOG2E) for s in slopes], axis=0)


def _key_norm_bound(k_ref, kn_ref, ngroups, nchunks):
    gt = jnp.where(_iota((16, 256), 0) == _iota((16, 256), 1) // (256 // ngroups), 1.0, 0.0).astype(_BF16)

    def body(j, best):
        kc = k_ref[0, pl.ds(pl.multiple_of(j * CH, CH), CH), :].astype(_F32)
        return jnp.maximum(best, _dot_nt(gt, (kc * kc).astype(_BF16)))

    best = lax.fori_loop(0, nchunks, body, jnp.zeros((16, CH), _F32))
    kn_ref[...] = jnp.broadcast_to(jnp.sqrt(jnp.max(best, axis=1, keepdims=True) * (1.0 + 2.0 ** -7)), kn_ref.shape)


def _first_chunks(qt, kn_ref, ngroups, scale, slopes, q0, nfull, nearest=0):
    q = qt.astype(_F32)
    tq = q.shape[1]
    qn = jnp.sqrt(jnp.max(jnp.sum((q * q).reshape(ngroups, 256 // ngroups, tq), axis=1), axis=1, keepdims=True))
    bound = scale * qn * kn_ref[0:ngroups, 0:1]
    per_head = ngroups // len(slopes)
    inv_slope = jnp.concatenate([jnp.full((per_head, 1), 1.0 / s, _F32) for s in slopes], axis=0)
    reach = ((2.0 * bound + SKIP_NATS) * inv_slope).astype(jnp.int32) + 2 + nearest
    last_far = q0 - CH + 1 - reach
    first = jnp.where(last_far < 0, 0, last_far // CH + 1)
    first = jnp.minimum(first, nfull)
    return [jnp.min(first[h * per_head:(h + 1) * per_head]) for h in range(len(slopes))]


def _attend(k_ref, vt_ref, bd_ref, ab_ref, slope_row, c1, q0, nfull, tq, w, mask_fn, aux0, first):
    g_tq = bd_ref.shape[1]
    nheads = 4

    def pv(j, p, heads):
        vt_c = vt_ref[0, j]
        return {h: _dot(vt_c[h * 64:(h + 1) * 64, :], p[:, h * w:(h + 1) * w]) for h in heads}

    def qk(j, heads):
        kc = k_ref[0, pl.ds(pl.multiple_of(j * CH, CH), CH), :]
        tiles = sorted({c0 // 256 for h in heads for c0 in range(h * w, (h + 1) * w, 128)})
        return {t: _dot(kc, bd_ref[:, t * 256:(t + 1) * 256]) for t in tiles}

    def softmax(s, j, m, l, aux, heads, diag):
        crow = slope_row * (j * CH - q0).astype(_F32)
        amask, aux = mask_fn(j, aux, diag)
        ps, ms, ls, alphas = [], [], [], []
        for c0 in range(0, g_tq, 128):
            cols = slice(c0, c0 + 128)
            if c0 // w not in heads:
                ps.append(jnp.zeros((CH, 128), _BF16))
                ms.append(m[:, cols])
                ls.append(l[:, cols])
                alphas.append(jnp.ones((1, 128), _F32))
                continue
            t = s[c0 // 256][:, c0 % 256:c0 % 256 + 128] * c1 + ab_ref[c0 // w]
            if amask is not None:
                t = t + amask[:, c0 % tq:c0 % tq + 128]
            m_new = jnp.maximum(m[:, cols], jnp.max(t, axis=0, keepdims=True) + crow[:, cols])
            alpha = jnp.exp2(m[:, cols] - m_new)
            p = jnp.exp2(t - (m_new - crow[:, cols]))
            ls.append(alpha * l[:, cols] + jnp.sum(p, axis=0, keepdims=True))
            ps.append(p.astype(_BF16))
            ms.append(m_new)
            alphas.append(alpha)
        cat = lambda xs: jnp.concatenate(xs, axis=1)
        return cat(ms), cat(ls), cat(alphas), cat(ps), aux

    def step(s, j, carry, heads, diag=False):
        m, l, acc, aux = carry
        m, l, alpha, p, aux = softmax(s, j, m, l, aux, heads, diag)
        pvs = pv(j, p, heads)
        acc = [alpha[:, h * w:(h + 1) * w] * acc[h] + pvs[h] if h in heads else acc[h] for h in range(nheads)]
        return m, l, acc, aux

    def run(lo, hi, heads, carry):
        unroll = ATTEND_UNROLL

        def group(i, c):
            j = lo + unroll * i
            ss = [qk(j + u, heads) for u in range(unroll)]
            for u in range(unroll):
                c = step(ss[u], j + u, c, heads)
            return c

        ngroup = jnp.maximum(hi - lo, 0) // unroll
        carry = lax.fori_loop(0, ngroup, group, carry)
        return lax.fori_loop(lo + unroll * ngroup, hi, lambda j, c: step(qk(j, heads), j, c, heads), carry)

    carry = (jnp.full((1, g_tq), NEG_INF, _F32), jnp.zeros((1, g_tq), _F32),
             [jnp.zeros((64, w), _F32) for _ in range(nheads)], aux0)
    lows, hi = [], nfull
    for h in range(nheads):
        u = ATTEND_UNROLL
        lo = jnp.maximum(hi - u * ((hi - jnp.minimum(first[h], hi) + u - 1) // u), 0)
        lows.append(lo)
        hi = lo
    for h in reversed(range(nheads)):
        carry = run(lows[h], lows[h - 1] if h else nfull, tuple(range(h, nheads)), carry)
    heads = tuple(range(nheads))
    m, l, acc, aux = step(qk(nfull, heads), nfull, carry, heads, True)
    return l, acc


def _attn_a_kernel(lam_ref, nw_ref, qt_ref, k_ref, vt_ref, o_ref, bd_ref, ab_ref, kn_ref, *, tq, slopes, lam_init):
    g = 2 * DA_HEADS
    q0 = pl.program_id(1) * tq
    nfull = q0 // CH

    @pl.when(pl.program_id(1) == 0)
    def _():
        _key_norm_bound(k_ref, kn_ref, g, k_ref.shape[1] // CH)

    bd_ref[...] = _block_diag_tile(qt_ref[0, 0], g)
    ab_ref[...] = _alibi_rows(slopes)

    def mask(j, aux, diag):
        if not diag:
            return None, aux
        rel = _iota((CH, tq), 0) - _iota((CH, tq), 1)
        return jnp.where(rel <= q0 - j * CH, 0.0, NEG_INF), aux

    l, acc = _attend(k_ref, vt_ref, bd_ref, ab_ref, _slope_row(slopes, 2, tq), DA_QK ** -0.5 * _LOG2E,
                     q0, nfull, tq, 2 * tq, mask, jnp.zeros((1, tq), _F32),
                     _first_chunks(qt_ref[0, 0], kn_ref, g, DA_QK ** -0.5, slopes, q0, nfull))

    lp = lam_ref[...]
    lam = (jnp.exp(jnp.sum(lp[0:1] * lp[1:2], axis=1, keepdims=True))
           - jnp.exp(jnp.sum(lp[2:3] * lp[3:4], axis=1, keepdims=True)) + lam_init)
    linv = 1.0 / l
    outs = []
    for h in range(DA_HEADS):
        a = acc[h] * linv[:, h * 2 * tq:(h + 1) * 2 * tq]
        o = a[:, :tq] - lam * a[:, tq:]
        ms = jnp.mean(o * o, axis=0, keepdims=True)
        outs.append(o * lax.rsqrt(ms + LN_EPS) * nw_ref[...] * (1.0 - lam_init))
    o_ref[0] = jnp.concatenate(outs, axis=0).T.astype(_BF16)


def _diff_attention(lam_p, norm_w, n16, t16, bn, s, slopes, lam_init):
    tq = TQ
    ns, per = s // CH, CH // tq
    g = 2 * DA_HEADS
    kern = functools.partial(_attn_a_kernel, tq=tq, slopes=slopes, lam_init=lam_init)
    return pl.pallas_call(
        kern,
        grid=(bn, s // tq),
        in_specs=[
            _const_spec(lam_p.shape), _const_spec(norm_w.shape),
            pl.BlockSpec((1, 1, 256, tq), lambda b, i: (b, i // per, 0, i % per)),
            pl.BlockSpec((1, s, 256), lambda b, i: (b, 0, 0)),
            pl.BlockSpec((1, ns, 256, CH), lambda b, i: (b, 0, 1, 0)),
        ],
        out_specs=pl.BlockSpec((1, tq, 256), lambda b, i: (b, i, 0)),
        out_shape=jax.ShapeDtypeStruct((bn, s, 256), _BF16),
        scratch_shapes=[pltpu.VMEM((256, g * tq), _BF16), pltpu.VMEM((DA_HEADS, CH, 128), _F32),
                        pltpu.VMEM((16, 128), _F32)],
        compiler_params=pltpu.CompilerParams(
            dimension_semantics=("parallel", "arbitrary"), vmem_limit_bytes=V7X_VMEM_LIMIT),
        name="diff_attn",
    )(lam_p, norm_w, t16, n16, t16)


def _bit_planes(rows):
    a = list(rows)
    j, m = 16, 0x0000FFFF
    while j:
        k = 0
        while k < 32:
            t = (a[k] ^ lax.shift_right_logical(a[k + j], jnp.int32(j))) & jnp.int32(m)
            a[k] = a[k] ^ t
            a[k + j] = a[k + j] ^ (t << j)
            k = (k + j + 1) & ~j
        j >>= 1
        if j:
            m = (m ^ (m << j)) & 0xFFFFFFFF
            m = m - (1 << 32) if m >= (1 << 31) else m
    return a[::-1]


def _dsa_kernel(iqt_ref, ik_ref, iwt_ref, qt_ref, k_ref, vt_ref, o_ref,
                iqbd_ref, bd_ref, key_ref, planes_ref, alive_ref, ab_ref, kn_ref, *, tq, slopes, topk):
    g = DSA_HEADS
    q0 = pl.program_id(1) * tq
    nfull = q0 // CH

    @pl.when(pl.program_id(1) == 0)
    def _():
        _key_norm_bound(k_ref, kn_ref, g, k_ref.shape[1] // CH)

    ngrp = (nfull + COUNT_UNROLL) // COUNT_UNROLL
    iqbd_ref[...] = _block_diag_tile(iqt_ref[0, 0], IDX_HEADS)
    bd_ref[...] = _block_diag_tile(qt_ref[0, 0], g)
    ab_ref[...] = _alibi_rows(slopes)
    w = iwt_ref[0, 0][0:IDX_HEADS, :] * (IDX_HEADS ** -0.5 * IDX_HD ** -0.5)

    def logits(j):
        return _dot(ik_ref[0, pl.ds(pl.multiple_of(j * CH, CH), CH), :], iqbd_ref[...])

    def score(lg, j, diag):
        half = CH // 2
        rows = []
        for r0 in (0, half):
            sc = jnp.maximum(lg[r0:r0 + half, 0:tq], 0.0) * w[0:1]
            for h in range(1, IDX_HEADS):
                sc = sc + jnp.maximum(lg[r0:r0 + half, h * tq:(h + 1) * tq], 0.0) * w[h:h + 1]
            sc = jnp.where(sc == 0.0, 0.0, sc)
            bits = pltpu.bitcast(sc, jnp.int32)
            key = bits ^ ((bits >> 31) & 0x7FFFFFFF)
            if diag:
                rel = _iota(key.shape, 0) - _iota(key.shape, 1)
                key = jnp.where(rel <= q0 - j * CH - r0, key, _INT_MIN)
            key_ref[j, r0:r0 + half, :] = key
            key3 = key.reshape(half // 8, 8, tq)
            rows += [key3[v] for v in range(half // 8)]
        planes = _bit_planes(rows)
        planes[31] = ~planes[31]
        planes_ref[j] = jnp.stack(planes, axis=0)
        if diag:
            lim = q0 - j * CH + _iota((8, tq), 1) - _iota((8, tq), 0)
            nbits = jnp.clip((lim >> 3) + 1, 0, 32)
            alive_ref[j] = jnp.where(nbits == 0, 0, jnp.left_shift(jnp.int32(-1), 32 - jnp.maximum(nbits, 1)))
        else:
            alive_ref[j] = jnp.full((8, tq), -1, jnp.int32)

    def score_group(i, c):
        lgs = [logits(SCORE_UNROLL * i + u) for u in range(SCORE_UNROLL)]
        for u in range(SCORE_UNROLL):
            score(lgs[u], SCORE_UNROLL * i + u, False)
        return c

    def score_one(j, c):
        score(logits(j), j, False)
        return c

    ngroup = nfull // SCORE_UNROLL
    lax.fori_loop(0, ngroup, score_group, 0)
    lax.fori_loop(SCORE_UNROLL * ngroup, nfull, score_one, 0)
    score(logits(nfull), nfull, True)

    for u in range(1, COUNT_UNROLL):
        @pl.when(nfull + u < ngrp * COUNT_UNROLL)
        def _():
            planes_ref[nfull + u] = jnp.zeros((32, 8, tq), jnp.int32)
            alive_ref[nfull + u] = jnp.zeros((8, tq), jnp.int32)

    def sweep(b_upd, keep, b_cnt):
        def body(gi, acc8):
            for u in range(COUNT_UNROLL):
                j = gi * COUNT_UNROLL + u
                a = alive_ref[j]
                if b_upd is not None:
                    a = a & ~(planes_ref[j, b_upd] ^ keep)
                    alive_ref[j] = a
                acc8 = acc8 + lax.population_count(a if b_cnt is None else a & planes_ref[j, b_cnt])
            return acc8
        acc8 = lax.fori_loop(0, ngrp, body, jnp.zeros((8, tq), jnp.int32))
        return jnp.sum(acc8, axis=0, keepdims=True)

    def decide(b, ones, want, thr):
        take = ones >= want
        thr = jnp.where(take, thr | jnp.left_shift(jnp.int32(1), b), thr)
        return jnp.where(take, want, want - ones), thr, jnp.where(take, -1, 0)

    want, thr, keep = decide(31, sweep(None, None, 31), jnp.full((1, tq), topk, jnp.int32),
                             jnp.zeros((1, tq), jnp.int32))

    def bit_body(i, c):
        want, thr, keep = c
        b = 30 - i
        return decide(b, sweep(b + 1, keep, b), want, thr)

    want, thr, keep = lax.fori_loop(0, 31, bit_body, (want, thr, keep))
    ties = sweep(0, keep, None)
    thr = thr ^ _INT_MIN
    need_tie = jnp.max(jnp.where((ties > want) & (thr > _INT_MIN), 1, 0)) > 0

    def attend(tie):
        if tie:
            first = [jnp.int32(0)] * g
            room = want.astype(_F32)
            lower = jnp.where(_iota((CH, CH), 0) > _iota((CH, CH), 1), 1.0, 0.0).astype(_BF16)
        else:
            thr_eff = jnp.maximum(thr, _INT_MIN + 1)

            def near_body(j, best):
                pos = jnp.where(key_ref[j] >= thr_eff, _iota((CH, tq), 0) + j * CH, -1)
                return jnp.maximum(best, jnp.max(pos, axis=0, keepdims=True))

            best = lax.fori_loop(jnp.maximum(nfull - (NEAR_CHUNKS - 1), 0), nfull + 1, near_body,
                                 jnp.full((1, tq), -1, jnp.int32))
            gap = jnp.where(best >= 0, q0 + _iota((1, tq), 1) - best, 2 ** 24)
            first = _first_chunks(qt_ref[0, 0], kn_ref, g, DSA_HD ** -0.5, slopes, q0, nfull,
                                  jnp.max(gap, axis=1, keepdims=True))

        def mask(j, seen, diag):
            key = key_ref[j]
            if tie:
                eq = key == thr
                eqf = jnp.where(eq, 1.0, 0.0)
                rank = _dot(lower, eqf.astype(_BF16)) + seen
                sel = ((key > thr) | (eq & (rank < room))) & (key > _INT_MIN)
                seen = seen + jnp.sum(eqf, axis=0, keepdims=True)
            else:
                sel = key >= thr_eff
            return jnp.where(sel, 0.0, NEG_INF), seen

        l, acc = _attend(k_ref, vt_ref, bd_ref, ab_ref, _slope_row(slopes, 1, tq), DSA_HD ** -0.5 * _LOG2E,
                         q0, nfull, tq, tq, mask, jnp.zeros((1, tq), _F32), first)
        linv = 1.0 / l
        outs = [acc[h] * linv[:, h * tq:(h + 1) * tq] for h in range(g)]
        o_ref[0] = jnp.concatenate(outs, axis=0).T.astype(_BF16)

    @pl.when(need_tie)
    def _():
        attend(True)

    @pl.when(jnp.logical_not(need_tie))
    def _():
        attend(False)


def _dsa_attention(n16, t16, t32, bn, s, slopes):
    tq = TQ
    ns, per = s // CH, CH // tq
    g = DSA_HEADS
    topk = min(TOPK_MAX, s // 4)
    assert ns % COUNT_UNROLL == 0
    kern = functools.partial(_dsa_kernel, tq=tq, slopes=slopes, topk=topk)
    return pl.pallas_call(
        kern,
        grid=(bn, s // tq),
        in_specs=[
            pl.BlockSpec((1, 1, 128, tq), lambda b, i: (b, i // per, 10, i % per)),
            pl.BlockSpec((1, s, 128), lambda b, i: (b, 0, 10)),
            pl.BlockSpec((1, 1, 16, tq), lambda b, i: (b, i // per, 16, i % per)),
            pl.BlockSpec((1, 1, 256, tq), lambda b, i: (b, i // per, 2, i % per)),
            pl.BlockSpec((1, s, 256), lambda b, i: (b, 0, 1)),
            pl.BlockSpec((1, ns, 256, CH), lambda b, i: (b, 0, 3, 0)),
        ],
        out_specs=pl.BlockSpec((1, tq, 256), lambda b, i: (b, i, 0)),
        out_shape=jax.ShapeDtypeStruct((bn, s, 256), _BF16),
        scratch_shapes=[
            pltpu.VMEM((128, IDX_HEADS * tq), _BF16), pltpu.VMEM((256, g * tq), _BF16),
            pltpu.VMEM((ns, CH, tq), jnp.int32), pltpu.VMEM((ns, 32, 8, tq), jnp.int32),
            pltpu.VMEM((ns, 8, tq), jnp.int32), pltpu.VMEM((DSA_HEADS, CH, 128), _F32),
            pltpu.VMEM((16, 128), _F32),
        ],
        compiler_params=pltpu.CompilerParams(
            dimension_semantics=("parallel", "arbitrary"), vmem_limit_bytes=V7X_VMEM_LIMIT),
        name="dsa_attn",
    )(t16, n16, t32, t16, n16, t16)


def _ret_kernel(q_ref, kt_ref, v_ref, g_ref, intra_ref, qdec_ref, kdect_ref, cd_ref, nw_ref, o_ref, s_ref, *, c):
    @pl.when(pl.program_id(1) == 0)
    def _():
        s_ref[...] = jnp.zeros(s_ref.shape, _F32)

    q = q_ref[0]
    v = v_ref[0]
    kt = kt_ref[0, 0] * (RET_QK ** -0.5)
    att = _dot(q, _block_diag_tile(kt.astype(_BF16), RET_HEADS)) * intra_ref[...]
    vt = jnp.concatenate([v] * RET_HEADS, axis=0)
    vbd = jnp.where((_iota(vt.shape, 0) // c) == (_iota(vt.shape, 1) // RET_V), vt, jnp.zeros_like(vt))
    st = s_ref[...]
    o = _dot(att.astype(_BF16), vbd) + _dot(q, st.astype(_BF16)) * qdec_ref[...]
    upd = _dot((kt * kdect_ref[...]).astype(_BF16), v)
    same_head = (_iota(upd.shape, 0) // RET_QK) == (_iota(upd.shape, 1) // RET_V)
    s_ref[...] = st * cd_ref[...] + jnp.where(same_head, upd, 0.0)

    mu = _group_mean(o, RET_V)
    d = o - mu
    var = _group_mean(d * d, RET_V)
    y = d * lax.rsqrt(var + LN_EPS) * nw_ref[...]
    gate = g_ref[0]
    o_ref[0] = (gate * jax.nn.sigmoid(gate) * y).astype(_BF16)


def _retention_consts(c):
    h = RET_HEADS
    log_g = np.log1p(-np.power(2.0, -5.0 - np.arange(h, dtype=np.float64)))
    pos = np.arange(c, dtype=np.float64)
    rel = pos[:, None] - pos[None, :]
    intra = np.where(rel >= 0, np.exp(log_g[:, None, None] * np.maximum(rel, 0.0)), 0.0)
    intra = np.transpose(intra, (1, 0, 2)).reshape(c, h * c)
    qdec = np.repeat(np.exp(log_g[:, None] * (pos[None, :] + 1.0)).T, RET_V, axis=1)
    kdect = np.repeat(np.exp(log_g[:, None] * (c - 1.0 - pos[None, :])), RET_QK, axis=0)
    cd = np.repeat(np.exp(log_g * c), RET_QK)[:, None] * np.ones((1, h * RET_V))
    return tuple(jnp.asarray(a, _F32) for a in (intra, qdec, kdect, cd))


def _retention(norm_w, n16, n32, t32, bn, s):
    c = RET_C
    per = CH // c
    intra, qdec, kdect, cd = _retention_consts(c)
    return pl.pallas_call(
        functools.partial(_ret_kernel, c=c),
        grid=(bn, s // c),
        in_specs=[
            pl.BlockSpec((1, c, 256), lambda b, i: (b, i, 2)),
            pl.BlockSpec((1, 1, 256, c), lambda b, i: (b, i // per, 0, i % per)),
            pl.BlockSpec((1, c, 256), lambda b, i: (b, i, 3)),
            pl.BlockSpec((1, c, 256), lambda b, i: (b, i, 0)),
            _const_spec(intra.shape), _const_spec(qdec.shape), _const_spec(kdect.shape),
            _const_spec(cd.shape), _const_spec(norm_w.shape),
        ],
        out_specs=pl.BlockSpec((1, c, 256), lambda b, i: (b, i, 0)),
        out_shape=jax.ShapeDtypeStruct((bn, s, 256), _BF16),
        scratch_shapes=[pltpu.VMEM((RET_HEADS * RET_QK, RET_HEADS * RET_V), _F32)],
        compiler_params=pltpu.CompilerParams(dimension_semantics=("parallel", "arbitrary")),
        name="retention",
    )(n16, t32, n16, n32, intra, qdec, kdect, cd, norm_w)


def _gla_kernel(q_ref, k_ref, a_ref, v_ref, vt_ref, g_ref, wa_ref, ba_ref, nw_ref, o_ref,
                st_ref, u_ref, oacc_ref, qs_ref, kk_ref, b_ref, qh_ref, dec_ref, *, ts, sub):
    nsub = ts // sub
    assert sub == 16

    @pl.when(pl.program_id(1) == 0)
    def _():
        st_ref[...] = jnp.zeros(st_ref.shape, _F32)

    la = jax.nn.log_sigmoid(_dot(a_ref[0].astype(_BF16), wa_ref[...]) + ba_ref[...]) * (1.0 / GLA_GATE_TEMP)
    in_blk = _iota(la.shape, 0) % sub
    b = la
    for sh in (1, 2, 4, 8):
        b = b + jnp.where(in_blk >= sh, pltpu.roll(b, sh, 0), 0.0)
    bl = jnp.where(in_blk == sub - 1, b, 0.0)
    for sh in (1, 2, 4, 8):
        bl = bl + jnp.where(in_blk < sub - sh, pltpu.roll(bl, ts - sh, 0), 0.0)
    qs = q_ref[0] * (GLA_QK ** -0.5)
    kk = k_ref[0]
    kd = (kk * jnp.exp(bl - b)).astype(_BF16)
    qs_ref[...] = qs
    kk_ref[...] = kk
    b_ref[...] = b
    qh_ref[...] = (qs * jnp.exp(b)).astype(_BF16)
    dec_ref[...] = jnp.exp(bl)
    vt = vt_ref[0, 0]
    row_blk = _iota(kd.shape, 0) // sub
    kd_wide = jnp.concatenate([jnp.where(row_blk == n, kd, jnp.zeros_like(kd)) for n in range(nsub)], axis=1)
    u_all = _dot(vt, kd_wide)
    for n in range(nsub):
        u_ref[n] = u_all[:, n * 128:(n + 1) * 128]

    st_keep = (_iota(st_ref.shape, 0) // GLA_V) == (_iota(st_ref.shape, 1) // GLA_QK)
    spread = jnp.where((_iota((128, 256), 0) // GLA_QK) == (_iota((128, 256), 1) // GLA_V), 1.0, 0.0).astype(_BF16)
    row16 = _iota((sub, 128), 0)

    def body(n, carry):
        r0 = pl.multiple_of(n * sub, sub)
        st = st_ref[...]
        o_cross = _dot_nt(qh_ref[pl.ds(r0, sub), :], st.astype(_BF16))
        q16 = qs_ref[pl.ds(r0, sub), :]
        k16 = kk_ref[pl.ds(r0, sub), :]
        b16 = b_ref[pl.ds(r0, sub), :]
        v16 = v_ref[0, pl.ds(r0, sub), :].astype(_F32)
        es = []
        for j in range(sub):
            e = q16 * k16[j:j + 1] * jnp.exp(jnp.minimum(b16 - b16[j:j + 1], 0.0))
            es.append(jnp.where(row16 >= j, e, 0.0))
        e_all = jnp.concatenate(es, axis=0)
        e_hi = e_all.astype(_BF16)
        e_lo = (e_all - e_hi.astype(_F32)).astype(_BF16)
        att = _dot(e_hi, spread) + _dot(e_lo, spread)
        o_diag = att[0:sub] * v16[0:1]
        for j in range(1, sub):
            o_diag = o_diag + att[j * sub:(j + 1) * sub] * v16[j:j + 1]
        oacc_ref[pl.ds(r0, sub), :] = o_cross + o_diag
        st_ref[...] = st * dec_ref[pl.ds(r0, 1), :] + jnp.where(st_keep, u_ref[n], 0.0)
        return carry

    lax.fori_loop(0, nsub, body, 0, unroll=2)

    o = oacc_ref[...]
    ms = _group_mean(o * o, GLA_V)
    y = o * lax.rsqrt(ms + LN_EPS) * nw_ref[...]
    gate = g_ref[0]
    o_ref[0] = (gate * jax.nn.sigmoid(gate) * y).astype(_BF16)


def _gla(wa, ba, norm_w, n16, n32, t16, bn, s):
    ts, sub = CH, GLA_SUB
    return pl.pallas_call(
        functools.partial(_gla_kernel, ts=ts, sub=sub),
        grid=(bn, s // ts),
        in_specs=[
            pl.BlockSpec((1, ts, 128), lambda b, i: (b, i, 4)),
            pl.BlockSpec((1, ts, 128), lambda b, i: (b, i, 5)),
            pl.BlockSpec((1, ts, 128), lambda b, i: (b, i, 6)),
            pl.BlockSpec((1, ts, 256), lambda b, i: (b, i, 4)),
            pl.BlockSpec((1, 1, 256, ts), lambda b, i: (b, i, 4, 0)),
            pl.BlockSpec((1, ts, 256), lambda b, i: (b, i, 1)),
            _const_spec(wa.shape), _const_spec(ba.shape), _const_spec(norm_w.shape),
        ],
        out_specs=pl.BlockSpec((1, ts, 256), lambda b, i: (b, i, 0)),
        out_shape=jax.ShapeDtypeStruct((bn, s, 256), _BF16),
        scratch_shapes=[
            pltpu.VMEM((GLA_HEADS * GLA_V, GLA_HEADS * GLA_QK), _F32),
            pltpu.VMEM((ts // sub, GLA_HEADS * GLA_V, GLA_HEADS * GLA_QK), _F32),
            pltpu.VMEM((ts, 256), _F32),
            pltpu.VMEM((ts, 128), _F32), pltpu.VMEM((ts, 128), _F32), pltpu.VMEM((ts, 128), _F32),
            pltpu.VMEM((ts, 128), _BF16), pltpu.VMEM((ts, 128), _F32),
        ],
        compiler_params=pltpu.CompilerParams(dimension_semantics=("parallel", "arbitrary")),
        name="gla",
    )(n32, n32, n32, n16, t16, n32, wa, ba, norm_w)


def _merge_kernel(x_ref, ya_ref, yb_ref, yc_ref, yd_ref, wg_ref, wbr_ref, wout_ref, lnw_ref, lnb_ref, h_ref):
    x = x_ref[...]
    xb = x.astype(_BF16)
    merged = None
    for n, y_ref in enumerate((ya_ref, yb_ref, yc_ref, yd_ref)):
        gate = jax.nn.sigmoid(_dot(xb, wg_ref[:, n * D_MODEL:(n + 1) * D_MODEL]))
        term = gate * _dot(y_ref[...], wbr_ref[n])
        merged = term if merged is None else merged + term
    mix = _dot(merged.astype(_BF16), wout_ref[...])
    h_ref[...] = _layer_norm(DEEPNORM_ALPHA * x + mix, lnw_ref[...], lnb_ref[...])


def _merge(x2, ys, wg, wbr, wout, lnw, lnb):
    t = x2.shape[0]
    ts = TS_DENSE
    tok = lambda w: pl.BlockSpec((ts, w), lambda i: (i, 0))
    return pl.pallas_call(
        _merge_kernel,
        grid=(t // ts,),
        in_specs=[tok(D_MODEL)] + [tok(BRANCH_W)] * 4 + [
            _const_spec(wg.shape), _const_spec(wbr.shape), _const_spec(wout.shape),
            _const_spec(lnw.shape), _const_spec(lnb.shape)],
        out_specs=tok(D_MODEL),
        out_shape=jax.ShapeDtypeStruct((t, D_MODEL), _F32),
        compiler_params=pltpu.CompilerParams(
            dimension_semantics=("parallel",), vmem_limit_bytes=V7X_VMEM_LIMIT),
        name="merge_ln",
    )(x2, *ys, wg, wbr, wout, lnw, lnb)


def _ffn_kernel(h_ref, p_ref, wup_ref, cw_ref, cb_ref, wdn_ref, wpg_ref, wpp_ref, lnw_ref, lnb_ref,
                o_ref, tail_ref, *, ts):
    @pl.when(pl.program_id(1) == 0)
    def _():
        tail_ref[...] = jnp.zeros(tail_ref.shape, _F32)

    h = h_ref[...]
    hb = h.astype(_BF16)
    row = _iota((ts, FF_COLS), 0)
    f = None
    for c0 in range(0, D_FF, FF_COLS):
        cols = slice(c0, c0 + FF_COLS)
        u = _dot(hb, wup_ref[:, cols])
        gt = _dot(hb, wup_ref[:, D_FF + c0:D_FF + c0 + FF_COLS])
        prev = tail_ref[:, cols]
        g1 = jnp.where(row == 0, prev[7:8], pltpu.roll(gt, 1, 0))
        g2 = jnp.where(row == 0, prev[6:7], jnp.where(row == 1, prev[7:8], pltpu.roll(gt, 2, 0)))
        tail_ref[:, cols] = gt[ts - 8:ts]
        gc = cb_ref[:, cols] + cw_ref[0:1, cols] * g2
        gc = gc + cw_ref[1:2, cols] * g1
        gc = gc + cw_ref[2:3, cols] * gt
        term = _dot((jax.nn.gelu(gc) * u).astype(_BF16), wdn_ref[cols, :])
        f = term if f is None else f + term
    e = jax.nn.sigmoid(_dot(hb, wpg_ref[...])) * _dot(p_ref[...].astype(_BF16), wpp_ref[...])
    o_ref[...] = _layer_norm(DEEPNORM_ALPHA * h + f + e, lnw_ref[...], lnb_ref[...])


def _ffn(h2, p2, wup, cw, cb, wdn, wpg, wpp, lnw, lnb, bn, s):
    ts = TS_DENSE
    ns = s // ts
    tok = lambda w: pl.BlockSpec((ts, w), lambda b, i: (b * ns + i, 0))
    return pl.pallas_call(
        functools.partial(_ffn_kernel, ts=ts),
        grid=(bn, ns),
        in_specs=[tok(D_MODEL), tok(P_DIM)] + [_const_spec(a.shape) for a in (wup, cw, cb, wdn, wpg, wpp, lnw, lnb)],
        out_specs=tok(D_MODEL),
        out_shape=jax.ShapeDtypeStruct((bn * s, D_MODEL), _F32),
        scratch_shapes=[pltpu.VMEM((8, D_FF), _F32)],
        compiler_params=pltpu.CompilerParams(
            dimension_semantics=("parallel", "arbitrary"), vmem_limit_bytes=V7X_VMEM_LIMIT),
        name="ffn_ple_ln",
    )(h2, p2, wup, cw, cb, wdn, wpg, wpp, lnw, lnb)


def kernel(x, p, w_in, a_lambda, a_norm_w, ret_norm_w, gla_w_a2, gla_b_a, gla_norm_w, w_branch, w_out,
           ln1_w, ln1_b, w_ffn_up, ffn_conv_w, ffn_conv_b, w_ffn_down, w_ple_gate, w_ple_proj, ln2_w, ln2_b):
    bn, s, _ = x.shape
    t = bn * s
    slopes = [2.0 ** (-(8.0 / N_SOFTMAX_HEADS) * i) for i in range(1, N_SOFTMAX_HEADS + 1)]
    slopes_a, slopes_b = tuple(slopes[0::2]), tuple(slopes[1::2])
    row = lambda v: v.astype(_F32).reshape(1, -1)
    x2 = x.reshape(t, D_MODEL)
    for i in range(DEPTH):
        *proj_w, wg = _prep_weights(w_in, i)
        n32, n16, t16, t32 = _project(x2, *proj_w, bn, s)
        n32 = n32.reshape(bn, s, -1)
        n16 = n16.reshape(bn, s, -1)
        lam_init = 0.8 - 0.6 * math.exp(-0.3 * i)
        y_a = _diff_attention(a_lambda[i].astype(_F32), a_norm_w[i].astype(_F32).reshape(DA_V, 1),
                              n16, t16, bn, s, slopes_a, lam_init)
        y_b = _dsa_attention(n16, t16, t32, bn, s, slopes_b)
        y_c = _retention(row(ret_norm_w[i]), n16, n32, t32, bn, s)
        wa = jnp.pad(gla_w_a2[i], ((0, 128 - GLA_RANK), (0, 0))).astype(_BF16)
        y_d = _gla(wa, row(gla_b_a[i]), row(jnp.tile(gla_norm_w[i], GLA_HEADS)), n16, n32, t16, bn, s)
        ys = [y.reshape(t, BRANCH_W) for y in (y_a, y_b, y_c, y_d)]
        h2 = _merge(x2, ys, wg, w_branch[i].astype(_BF16), w_out[i].astype(_BF16), row(ln1_w[i]), row(ln1_b[i]))
        x2 = _ffn(h2, p[i].reshape(t, P_DIM), w_ffn_up[i].astype(_BF16), ffn_conv_w[i].astype(_F32),
                  row(ffn_conv_b[i]), w_ffn_down[i].astype(_BF16), w_ple_gate[i].astype(_BF16),
                  w_ple_proj[i].astype(_BF16), row(ln2_w[i]), row(ln2_b[i]), bn, s)
    return x2.reshape(bn, s, D_MODEL)
```

```python
import functools
import math

import numpy as np
import jax
import jax.numpy as jnp
from jax import lax
from jax.experimental import pallas as pl
from jax.experimental.pallas import tpu as pltpu

D_MODEL = 1024
DEPTH = 2
P_DIM = 256
N_BRANCH = 4
BRANCH_W = 256
DA_HEADS = 4
DA_QK = 32
DA_V = 64
DSA_HEADS = 4
DSA_HD = 64
IDX_HEADS = 4
IDX_HD = 32
TOPK_MAX = 256
RET_HEADS = 4
RET_QK = 64
RET_V = 64
GLA_HEADS = 4
GLA_QK = 32
GLA_V = 64
GLA_RANK = 16
GLA_GATE_TEMP = 16.0
D_FF = 2816
CONV_W = 3
N_SOFTMAX_HEADS = DA_HEADS + DSA_HEADS
LN_EPS = 1e-5
NEG_INF = -1e30
DEEPNORM_ALPHA = (2.0 * DEPTH) ** 0.25

IN_SIZES = (256, 256, 256, 256, 256, 256, 128, 32, 4, 256, 256, 256, 256, 128, 128, 256, 16, 256, 4096)
IN_NAMES = ("a_q", "a_k", "a_v", "b_q", "b_k", "b_v", "b_iq", "b_ik", "b_iw",
            "c_q", "c_k", "c_v", "c_g", "d_q", "d_k", "d_v", "d_a", "d_g", "m_g")

_BF16 = jnp.bfloat16
_F32 = jnp.float32
_INT_MIN = -2 ** 31
_LOG2E = math.log2(math.e)

CH = 256
TQ = 256
RET_C = 128
GLA_SUB = 16
COUNT_UNROLL = 4
SCORE_UNROLL = 4
ATTEND_UNROLL = 4
TS_DENSE = 512
FF_COLS = 256
PREP_ROWS = 256
NEAR_CHUNKS = 4
SKIP_NATS = 32.0
FIXED_MAX_BITS = 100.0
V7X_VMEM_LIMIT = 56 * 1024 * 1024


def _dot(a, b, precision=None):
    return jnp.dot(a, b, preferred_element_type=_F32, precision=precision)


def _dot_nt(a, b):
    return lax.dot_general(a, b, (((1,), (1,)), ((), ())), preferred_element_type=_F32)


def _iota(shape, dim):
    return lax.broadcasted_iota(jnp.int32, shape, dim)


def _block_diag_tile(m_t, ngroups):
    r, tq = m_t.shape
    tiled = jnp.concatenate([m_t] * ngroups, axis=1)
    keep = (_iota(tiled.shape, 0) // (r // ngroups)) == (_iota(tiled.shape, 1) // tq)
    return jnp.where(keep, tiled, jnp.zeros_like(tiled))


def _layer_norm(x, w, b):
    mu = jnp.mean(x, -1, keepdims=True)
    var = jnp.mean(jnp.square(x - mu), -1, keepdims=True)
    return (x - mu) * lax.rsqrt(var + LN_EPS) * w + b


def _group_mean(x, group):
    lane_g = _iota(x.shape, 1) // group
    out = jnp.zeros_like(x)
    for h in range(x.shape[1] // group):
        mk = lane_g == h
        mh = jnp.sum(jnp.where(mk, x, 0.0), axis=1, keepdims=True) * (1.0 / group)
        out = jnp.where(mk, mh, out)
    return out


def _const_spec(shape):
    nd = len(shape)
    return pl.BlockSpec(shape, lambda *_: (0,) * nd, pipeline_mode=pl.Buffered(1))


def _prep_kernel(w_ref, wn32_ref, wn16_ref, wt16_ref, wt32_ref, wg_ref):
    offs = [0] + np.cumsum(IN_SIZES).tolist()
    col = {n: w_ref[0, :, offs[i]:offs[i + 1]] for i, n in enumerate(IN_NAMES)}
    zeros = lambda n: jnp.zeros((w_ref.shape[1], n), _F32)
    cat = lambda xs: jnp.concatenate(xs, axis=1)
    wn32_ref[...] = cat([col["c_g"], col["d_g"], col["d_q"], col["d_k"], col["d_a"],
                         zeros(128 - GLA_RANK)]).astype(_BF16)
    wn16_ref[...] = cat([col["a_k"], col["b_k"], col["c_q"], col["c_v"], col["d_v"]]
                        + [col["b_ik"]] * IDX_HEADS).astype(_BF16)
    wt16_ref[...] = cat([col["a_q"], col["a_v"], col["b_q"], col["b_v"], col["d_v"], col["b_iq"]]).T.astype(_BF16)
    wt32_ref[...] = cat([col["c_k"], col["b_iw"], zeros(16 - IDX_HEADS)]).T.astype(_BF16)
    wg_ref[...] = col["m_g"].astype(_BF16)


def _prep_weights(w_in, layer):
    rb = PREP_ROWS
    n32, n16, t16, t32, ng = 896, 1408, 1408, 272, N_BRANCH * D_MODEL
    rows = lambda n: pl.BlockSpec((rb, n), lambda r: (r, 0))
    cols = lambda n: pl.BlockSpec((n, rb), lambda r: (0, r))
    return pl.pallas_call(
        _prep_kernel,
        grid=(D_MODEL // rb,),
        in_specs=[pl.BlockSpec((1, rb, w_in.shape[2]), lambda r: (layer, r, 0))],
        out_specs=[rows(n32), rows(n16), cols(t16), cols(t32), rows(ng)],
        out_shape=[jax.ShapeDtypeStruct((D_MODEL, n32), _BF16), jax.ShapeDtypeStruct((D_MODEL, n16), _BF16),
                   jax.ShapeDtypeStruct((t16, D_MODEL), _BF16), jax.ShapeDtypeStruct((t32, D_MODEL), _BF16),
                   jax.ShapeDtypeStruct((D_MODEL, ng), _BF16)],
        compiler_params=pltpu.CompilerParams(
            dimension_semantics=("parallel",), vmem_limit_bytes=V7X_VMEM_LIMIT),
        name="prep_weights",
    )(w_in)


def _proj_kernel(x_ref, wn32_ref, wn16_ref, wt16_ref, wt32_ref, n32_ref, n16_ref, t16_ref, t32_ref):
    x = x_ref[...].astype(_BF16)
    n32_ref[...] = _dot(x, wn32_ref[...])
    n16_ref[...] = _dot(x, wn16_ref[...]).astype(_BF16)
    t16_ref[0, 0] = _dot_nt(wt16_ref[...], x).astype(_BF16)
    t32_ref[0, 0] = _dot_nt(wt32_ref[...], x)


def _project(x2, wn32, wn16, wt16, wt32, bn, s):
    ns = s // CH
    n32, n16, t16, t32 = wn32.shape[1], wn16.shape[1], wt16.shape[0], wt32.shape[0]
    return pl.pallas_call(
        _proj_kernel,
        grid=(bn, ns),
        in_specs=[
            pl.BlockSpec((CH, D_MODEL), lambda b, i: (b * ns + i, 0)),
            _const_spec(wn32.shape), _const_spec(wn16.shape), _const_spec(wt16.shape), _const_spec(wt32.shape),
        ],
        out_specs=[
            pl.BlockSpec((CH, n32), lambda b, i: (b * ns + i, 0)),
            pl.BlockSpec((CH, n16), lambda b, i: (b * ns + i, 0)),
            pl.BlockSpec((1, 1, t16, CH), lambda b, i: (b, i, 0, 0)),
            pl.BlockSpec((1, 1, t32, CH), lambda b, i: (b, i, 0, 0)),
        ],
        out_shape=[
            jax.ShapeDtypeStruct((bn * s, n32), _F32),
            jax.ShapeDtypeStruct((bn * s, n16), _BF16),
            jax.ShapeDtypeStruct((bn, ns, t16, CH), _BF16),
            jax.ShapeDtypeStruct((bn, ns, t32, CH), _F32),
        ],
        compiler_params=pltpu.CompilerParams(
            dimension_semantics=("parallel", "parallel"), vmem_limit_bytes=V7X_VMEM_LIMIT),
        name="proj",
    )(x2, wn32, wn16, wt16, wt32)


def _slope_row(slopes, reps, tq):
    return jnp.concatenate([jnp.full((1, tq), s * _LOG2E, _F32) for s in slopes for _ in range(reps)], axis=1)


def _alibi_rows(slopes):
    rows = _iota((CH, 128), 0).astype(_F32)
    return jnp.stack([rows * (s * _LOG2E) for s in slopes], axis=0)


def _key_norm_bound(k_ref, kn_ref, ngroups, nchunks):
    gt = jnp.where(_iota((16, 256), 0) == _iota((16, 256), 1) // (256 // ngroups), 1.0, 0.0).astype(_BF16)

    def body(j, best):
        kc = k_ref[0, pl.ds(pl.multiple_of(j * CH, CH), CH), :].astype(_F32)
        return jnp.maximum(best, _dot_nt(gt, (kc * kc).astype(_BF16)))

    best = lax.fori_loop(0, nchunks, body, jnp.zeros((16, CH), _F32))
    kn_ref[...] = jnp.broadcast_to(jnp.sqrt(jnp.max(best, axis=1, keepdims=True) * (1.0 + 2.0 ** -7)), kn_ref.shape)


def _logit_bound(qt, kn_ref, ngroups, scale):
    q = qt.astype(_F32)
    tq = q.shape[1]
    qn = jnp.sqrt(jnp.max(jnp.sum((q * q).reshape(ngroups, 256 // ngroups, tq), axis=1), axis=1, keepdims=True))
    return scale * qn * kn_ref[0:ngroups, 0:1]


def _first_chunks(bound, slopes, q0, nfull, nearest=0):
    per_head = bound.shape[0] // len(slopes)
    inv_slope = jnp.concatenate([jnp.full((per_head, 1), 1.0 / s, _F32) for s in slopes], axis=0)
    reach = ((2.0 * bound + SKIP_NATS) * inv_slope).astype(jnp.int32) + 2 + nearest
    last_far = q0 - CH + 1 - reach
    first = jnp.where(last_far < 0, 0, last_far // CH + 1)
    first = jnp.minimum(first, nfull)
    return [jnp.min(first[h * per_head:(h + 1) * per_head]) for h in range(len(slopes))]


def _logit_ceiling(bound, slopes, tq, nearest=0):
    per_head = bound.shape[0] // len(slopes)
    lane = _iota((1, tq), 1).astype(_F32)
    rows, worst = [], None
    for g in range(bound.shape[0]):
        slope = slopes[g // per_head]
        rows.append(bound[g:g + 1, :] * _LOG2E + (slope * _LOG2E) * lane)
        depth = (2.0 * bound[g:g + 1, :] + slope * nearest) * _LOG2E
        worst = depth if worst is None else jnp.maximum(worst, depth)
    return jnp.concatenate(rows, axis=1), jnp.max(jnp.where(worst > FIXED_MAX_BITS, 1, 0)) == 0


def _attend(k_ref, vt_ref, bd_ref, ab_ref, slope_row, c1, q0, nfull, tq, w, mask_fn, aux0, first, ceiling=None):
    g_tq = bd_ref.shape[1]
    nheads = 4

    def pv(j, p, heads):
        vt_c = vt_ref[0, j]
        return {h: _dot(vt_c[h * 64:(h + 1) * 64, :], p[:, h * w:(h + 1) * w]) for h in heads}

    def qk(j, heads):
        kc = k_ref[0, pl.ds(pl.multiple_of(j * CH, CH), CH), :]
        tiles = sorted({c0 // 256 for h in heads for c0 in range(h * w, (h + 1) * w, 128)})
        return {t: _dot(kc, bd_ref[:, t * 256:(t + 1) * 256]) for t in tiles}

    def softmax(s, j, m, l, aux, heads, diag):
        crow = slope_row * (j * CH - q0).astype(_F32)
        amask, aux = mask_fn(j, aux, diag)
        ps, ms, ls, alphas = [], [], [], []
        for c0 in range(0, g_tq, 128):
            cols = slice(c0, c0 + 128)
            if c0 // w not in heads:
                ps.append(jnp.zeros((CH, 128), _BF16))
                ms.append(m[:, cols])
                ls.append(l[:, cols])
                alphas.append(jnp.ones((1, 128), _F32))
                continue
            t = s[c0 // 256][:, c0 % 256:c0 % 256 + 128] * c1 + ab_ref[c0 // w]
            if amask is not None:
                t = t + amask[:, c0 % tq:c0 % tq + 128]
            if ceiling is not None:
                p = jnp.exp2(t + (crow[:, cols] - ceiling[:, cols]))
                m_new, alpha = m[:, cols], jnp.ones((1, 128), _F32)
            else:
                m_new = jnp.maximum(m[:, cols], jnp.max(t, axis=0, keepdims=True) + crow[:, cols])
                alpha = jnp.exp2(m[:, cols] - m_new)
                p = jnp.exp2(t - (m_new - crow[:, cols]))
            ls.append(alpha * l[:, cols] + jnp.sum(p, axis=0, keepdims=True))
            ps.append(p.astype(_BF16))
            ms.append(m_new)
            alphas.append(alpha)
        cat = lambda xs: jnp.concatenate(xs, axis=1)
        return cat(ms), cat(ls), cat(alphas), cat(ps), aux

    def step(s, j, carry, heads, diag=False):
        m, l, acc, aux = carry
        m, l, alpha, p, aux = softmax(s, j, m, l, aux, heads, diag)
        pvs = pv(j, p, heads)
        if ceiling is not None:
            acc = [acc[h] + pvs[h] if h in heads else acc[h] for h in range(nheads)]
        else:
            acc = [alpha[:, h * w:(h + 1) * w] * acc[h] + pvs[h] if h in heads else acc[h] for h in range(nheads)]
        return m, l, acc, aux

    def run(lo, hi, heads, carry):
        unroll = ATTEND_UNROLL

        def group(i, c):
            j = lo + unroll * i
            ss = [qk(j + u, heads) for u in range(unroll)]
            for u in range(unroll):
                c = step(ss[u], j + u, c, heads)
            return c

        ngroup = jnp.maximum(hi - lo, 0) // unroll
        carry = lax.fori_loop(0, ngroup, group, carry)
        return lax.fori_loop(lo + unroll * ngroup, hi, lambda j, c: step(qk(j, heads), j, c, heads), carry)

    carry = (jnp.full((1, g_tq), NEG_INF, _F32), jnp.zeros((1, g_tq), _F32),
             [jnp.zeros((64, w), _F32) for _ in range(nheads)], aux0)
    lows, hi = [], nfull
    for h in range(nheads):
        u = ATTEND_UNROLL
        lo = jnp.maximum(hi - u * ((hi - jnp.minimum(first[h], hi) + u - 1) // u), 0)
        lows.append(lo)
        hi = lo
    for h in reversed(range(nheads)):
        carry = run(lows[h], lows[h - 1] if h else nfull, tuple(range(h, nheads)), carry)
    heads = tuple(range(nheads))
    m, l, acc, aux = step(qk(nfull, heads), nfull, carry, heads, True)
    return l, acc


def _attn_a_kernel(lam_ref, nw_ref, qt_ref, k_ref, vt_ref, o_ref, bd_ref, ab_ref, kn_ref, *, tq, slopes, lam_init):
    g = 2 * DA_HEADS
    q0 = pl.program_id(1) * tq
    nfull = q0 // CH

    @pl.when(pl.program_id(1) == 0)
    def _():
        _key_norm_bound(k_ref, kn_ref, g, k_ref.shape[1] // CH)

    bd_ref[...] = _block_diag_tile(qt_ref[0, 0], g)
    ab_ref[...] = _alibi_rows(slopes)

    def mask(j, aux, diag):
        if not diag:
            return None, aux
        rel = _iota((CH, tq), 0) - _iota((CH, tq), 1)
        return jnp.where(rel <= q0 - j * CH, 0.0, NEG_INF), aux

    bound = _logit_bound(qt_ref[0, 0], kn_ref, g, DA_QK ** -0.5)
    first = _first_chunks(bound, slopes, q0, nfull)
    ceiling, ceiling_ok = _logit_ceiling(bound, slopes, tq)

    def attend(ceil):
        l, acc = _attend(k_ref, vt_ref, bd_ref, ab_ref, _slope_row(slopes, 2, tq), DA_QK ** -0.5 * _LOG2E,
                         q0, nfull, tq, 2 * tq, mask, jnp.zeros((1, tq), _F32), first, ceil)
        lp = lam_ref[...]
        lam = (jnp.exp(jnp.sum(lp[0:1] * lp[1:2], axis=1, keepdims=True))
               - jnp.exp(jnp.sum(lp[2:3] * lp[3:4], axis=1, keepdims=True)) + lam_init)
        linv = 1.0 / l
        outs = []
        for h in range(DA_HEADS):
            a = acc[h] * linv[:, h * 2 * tq:(h + 1) * 2 * tq]
            o = a[:, :tq] - lam * a[:, tq:]
            ms = jnp.mean(o * o, axis=0, keepdims=True)
            outs.append(o * lax.rsqrt(ms + LN_EPS) * nw_ref[...] * (1.0 - lam_init))
        o_ref[0] = jnp.concatenate(outs, axis=0).T.astype(_BF16)

    @pl.when(ceiling_ok)
    def _():
        attend(ceiling)

    @pl.when(jnp.logical_not(ceiling_ok))
    def _():
        attend(None)


def _diff_attention(lam_p, norm_w, n16, t16, bn, s, slopes, lam_init):
    tq = TQ
    ns, per = s // CH, CH // tq
    g = 2 * DA_HEADS
    kern = functools.partial(_attn_a_kernel, tq=tq, slopes=slopes, lam_init=lam_init)
    return pl.pallas_call(
        kern,
        grid=(bn, s // tq),
        in_specs=[
            _const_spec(lam_p.shape), _const_spec(norm_w.shape),
            pl.BlockSpec((1, 1, 256, tq), lambda b, i: (b, i // per, 0, i % per)),
            pl.BlockSpec((1, s, 256), lambda b, i: (b, 0, 0)),
            pl.BlockSpec((1, ns, 256, CH), lambda b, i: (b, 0, 1, 0)),
        ],
        out_specs=pl.BlockSpec((1, tq, 256), lambda b, i: (b, i, 0)),
        out_shape=jax.ShapeDtypeStruct((bn, s, 256), _BF16),
        scratch_shapes=[pltpu.VMEM((256, g * tq), _BF16), pltpu.VMEM((DA_HEADS, CH, 128), _F32),
                        pltpu.VMEM((16, 128), _F32)],
        compiler_params=pltpu.CompilerParams(
            dimension_semantics=("parallel", "arbitrary"), vmem_limit_bytes=V7X_VMEM_LIMIT),
        name="diff_attn",
    )(lam_p, norm_w, t16, n16, t16)


def _bit_planes(rows):
    a = list(rows)
    j, m = 16, 0x0000FFFF
    while j:
        k = 0
        while k < 32:
            t = (a[k] ^ lax.shift_right_logical(a[k + j], jnp.int32(j))) & jnp.int32(m)
            a[k] = a[k] ^ t
            a[k + j] = a[k + j] ^ (t << j)
            k = (k + j + 1) & ~j
        j >>= 1
        if j:
            m = (m ^ (m << j)) & 0xFFFFFFFF
            m = m - (1 << 32) if m >= (1 << 31) else m
    return a[::-1]


def _dsa_kernel(iqt_ref, ik_ref, iwt_ref, qt_ref, k_ref, vt_ref, o_ref,
                iqbd_ref, bd_ref, key_ref, planes_ref, alive_ref, ab_ref, kn_ref, *, tq, slopes, topk):
    g = DSA_HEADS
    q0 = pl.program_id(1) * tq
    nfull = q0 // CH

    @pl.when(pl.program_id(1) == 0)
    def _():
        _key_norm_bound(k_ref, kn_ref, g, k_ref.shape[1] // CH)

    ngrp = (nfull + COUNT_UNROLL) // COUNT_UNROLL
    iqbd_ref[...] = _block_diag_tile(iqt_ref[0, 0], IDX_HEADS)
    bd_ref[...] = _block_diag_tile(qt_ref[0, 0], g)
    ab_ref[...] = _alibi_rows(slopes)
    w = iwt_ref[0, 0][0:IDX_HEADS, :] * (IDX_HEADS ** -0.5 * IDX_HD ** -0.5)

    def logits(j):
        return _dot(ik_ref[0, pl.ds(pl.multiple_of(j * CH, CH), CH), :], iqbd_ref[...])

    def score(lg, j, diag):
        half = CH // 2
        rows = []
        for r0 in (0, half):
            sc = jnp.maximum(lg[r0:r0 + half, 0:tq], 0.0) * w[0:1]
            for h in range(1, IDX_HEADS):
                sc = sc + jnp.maximum(lg[r0:r0 + half, h * tq:(h + 1) * tq], 0.0) * w[h:h + 1]
            sc = jnp.where(sc == 0.0, 0.0, sc)
            bits = pltpu.bitcast(sc, jnp.int32)
            key = bits ^ ((bits >> 31) & 0x7FFFFFFF)
            if diag:
                rel = _iota(key.shape, 0) - _iota(key.shape, 1)
                key = jnp.where(rel <= q0 - j * CH - r0, key, _INT_MIN)
            key_ref[j, r0:r0 + half, :] = key
            key3 = key.reshape(half // 8, 8, tq)
            rows += [key3[v] for v in range(half // 8)]
        planes = _bit_planes(rows)
        planes[31] = ~planes[31]
        planes_ref[j] = jnp.stack(planes, axis=0)
        if diag:
            lim = q0 - j * CH + _iota((8, tq), 1) - _iota((8, tq), 0)
            nbits = jnp.clip((lim >> 3) + 1, 0, 32)
            alive_ref[j] = jnp.where(nbits == 0, 0, jnp.left_shift(jnp.int32(-1), 32 - jnp.maximum(nbits, 1)))
        else:
            alive_ref[j] = jnp.full((8, tq), -1, jnp.int32)

    def score_group(i, c):
        lgs = [logits(SCORE_UNROLL * i + u) for u in range(SCORE_UNROLL)]
        for u in range(SCORE_UNROLL):
            score(lgs[u], SCORE_UNROLL * i + u, False)
        return c

    def score_one(j, c):
        score(logits(j), j, False)
        return c

    ngroup = nfull // SCORE_UNROLL
    lax.fori_loop(0, ngroup, score_group, 0)
    lax.fori_loop(SCORE_UNROLL * ngroup, nfull, score_one, 0)
    score(logits(nfull), nfull, True)

    for u in range(1, COUNT_UNROLL):
        @pl.when(nfull + u < ngrp * COUNT_UNROLL)
        def _():
            planes_ref[nfull + u] = jnp.zeros((32, 8, tq), jnp.int32)
            alive_ref[nfull + u] = jnp.zeros((8, tq), jnp.int32)

    def sweep(b_upd, keep, b_cnt):
        def body(gi, acc8):
            for u in range(COUNT_UNROLL):
                j = gi * COUNT_UNROLL + u
                a = alive_ref[j]
                if b_upd is not None:
                    a = a & ~(planes_ref[j, b_upd] ^ keep)
                    alive_ref[j] = a
                acc8 = acc8 + lax.population_count(a if b_cnt is None else a & planes_ref[j, b_cnt])
            return acc8
        acc8 = lax.fori_loop(0, ngrp, body, jnp.zeros((8, tq), jnp.int32))
        return jnp.sum(acc8, axis=0, keepdims=True)

    def decide(b, ones, want, thr):
        take = ones >= want
        thr = jnp.where(take, thr | jnp.left_shift(jnp.int32(1), b), thr)
        return jnp.where(take, want, want - ones), thr, jnp.where(take, -1, 0)

    want, thr, keep = decide(31, sweep(None, None, 31), jnp.full((1, tq), topk, jnp.int32),
                             jnp.zeros((1, tq), jnp.int32))

    def bit_body(i, c):
        want, thr, keep = c
        b = 30 - i
        return decide(b, sweep(b + 1, keep, b), want, thr)

    want, thr, keep = lax.fori_loop(0, 31, bit_body, (want, thr, keep))
    ties = sweep(0, keep, None)
    thr = thr ^ _INT_MIN
    need_tie = jnp.max(jnp.where((ties > want) & (thr > _INT_MIN), 1, 0)) > 0

    def attend(tie):
        if tie:
            first = [jnp.int32(0)] * g
            room = want.astype(_F32)
            lower = jnp.where(_iota((CH, CH), 0) > _iota((CH, CH), 1), 1.0, 0.0).astype(_BF16)
        else:
            thr_eff = jnp.maximum(thr, _INT_MIN + 1)

            def near_body(j, best):
                pos = jnp.where(key_ref[j] >= thr_eff, _iota((CH, tq), 0) + j * CH, -1)
                return jnp.maximum(best, jnp.max(pos, axis=0, keepdims=True))

            best = lax.fori_loop(jnp.maximum(nfull - (NEAR_CHUNKS - 1), 0), nfull + 1, near_body,
                                 jnp.full((1, tq), -1, jnp.int32))
            gap = jnp.where(best >= 0, q0 + _iota((1, tq), 1) - best, 2 ** 24)
            nearest = jnp.max(gap, axis=1, keepdims=True)
            bound = _logit_bound(qt_ref[0, 0], kn_ref, g, DSA_HD ** -0.5)
            first = _first_chunks(bound, slopes, q0, nfull, nearest)

        def mask(j, seen, diag):
            key = key_ref[j]
            if tie:
                eq = key == thr
                eqf = jnp.where(eq, 1.0, 0.0)
                rank = _dot(lower, eqf.astype(_BF16)) + seen
                sel = ((key > thr) | (eq & (rank < room))) & (key > _INT_MIN)
                seen = seen + jnp.sum(eqf, axis=0, keepdims=True)
            else:
                sel = key >= thr_eff
            return jnp.where(sel, 0.0, NEG_INF), seen

        def run(ceil):
            l, acc = _attend(k_ref, vt_ref, bd_ref, ab_ref, _slope_row(slopes, 1, tq), DSA_HD ** -0.5 * _LOG2E,
                             q0, nfull, tq, tq, mask, jnp.zeros((1, tq), _F32), first, ceil)
            linv = 1.0 / l
            outs = [acc[h] * linv[:, h * tq:(h + 1) * tq] for h in range(g)]
            o_ref[0] = jnp.concatenate(outs, axis=0).T.astype(_BF16)

        if tie:
            run(None)
        else:
            ceiling, ceiling_ok = _logit_ceiling(bound, slopes, tq, nearest.astype(_F32))

            @pl.when(ceiling_ok)
            def _():
                run(ceiling)

            @pl.when(jnp.logical_not(ceiling_ok))
            def _():
                run(None)

    @pl.when(need_tie)
    def _():
        attend(True)

    @pl.when(jnp.logical_not(need_tie))
    def _():
        attend(False)


def _dsa_attention(n16, t16, t32, bn, s, slopes):
    tq = TQ
    ns, per = s // CH, CH // tq
    g = DSA_HEADS
    topk = min(TOPK_MAX, s // 4)
    assert ns % COUNT_UNROLL == 0
    kern = functools.partial(_dsa_kernel, tq=tq, slopes=slopes, topk=topk)
    return pl.pallas_call(
        kern,
        grid=(bn, s // tq),
        in_specs=[
            pl.BlockSpec((1, 1, 128, tq), lambda b, i: (b, i // per, 10, i % per)),
            pl.BlockSpec((1, s, 128), lambda b, i: (b, 0, 10)),
            pl.BlockSpec((1, 1, 16, tq), lambda b, i: (b, i // per, 16, i % per)),
            pl.BlockSpec((1, 1, 256, tq), lambda b, i: (b, i // per, 2, i % per)),
            pl.BlockSpec((1, s, 256), lambda b, i: (b, 0, 1)),
            pl.BlockSpec((1, ns, 256, CH), lambda b, i: (b, 0, 3, 0)),
        ],
        out_specs=pl.BlockSpec((1, tq, 256), lambda b, i: (b, i, 0)),
        out_shape=jax.ShapeDtypeStruct((bn, s, 256), _BF16),
        scratch_shapes=[
            pltpu.VMEM((128, IDX_HEADS * tq), _BF16), pltpu.VMEM((256, g * tq), _BF16),
            pltpu.VMEM((ns, CH, tq), jnp.int32), pltpu.VMEM((ns, 32, 8, tq), jnp.int32),
            pltpu.VMEM((ns, 8, tq), jnp.int32), pltpu.VMEM((DSA_HEADS, CH, 128), _F32),
            pltpu.VMEM((16, 128), _F32),
        ],
        compiler_params=pltpu.CompilerParams(
            dimension_semantics=("parallel", "arbitrary"), vmem_limit_bytes=V7X_VMEM_LIMIT),
        name="dsa_attn",
    )(t16, n16, t32, t16, n16, t16)


def _ret_kernel(q_ref, kt_ref, v_ref, g_ref, intra_ref, qdec_ref, kdect_ref, cd_ref, nw_ref, o_ref, s_ref, *, c):
    @pl.when(pl.program_id(1) == 0)
    def _():
        s_ref[...] = jnp.zeros(s_ref.shape, _F32)

    q = q_ref[0]
    v = v_ref[0]
    kt = kt_ref[0, 0] * (RET_QK ** -0.5)
    att = _dot(q, _block_diag_tile(kt.astype(_BF16), RET_HEADS)) * intra_ref[...]
    vt = jnp.concatenate([v] * RET_HEADS, axis=0)
    vbd = jnp.where((_iota(vt.shape, 0) // c) == (_iota(vt.shape, 1) // RET_V), vt, jnp.zeros_like(vt))
    st = s_ref[...]
    o = _dot(att.astype(_BF16), vbd) + _dot(q, st.astype(_BF16)) * qdec_ref[...]
    upd = _dot((kt * kdect_ref[...]).astype(_BF16), v)
    same_head = (_iota(upd.shape, 0) // RET_QK) == (_iota(upd.shape, 1) // RET_V)
    s_ref[...] = st * cd_ref[...] + jnp.where(same_head, upd, 0.0)

    mu = _group_mean(o, RET_V)
    d = o - mu
    var = _group_mean(d * d, RET_V)
    y = d * lax.rsqrt(var + LN_EPS) * nw_ref[...]
    gate = g_ref[0]
    o_ref[0] = (gate * jax.nn.sigmoid(gate) * y).astype(_BF16)


def _retention_consts(c):
    h = RET_HEADS
    log_g = np.log1p(-np.power(2.0, -5.0 - np.arange(h, dtype=np.float64)))
    pos = np.arange(c, dtype=np.float64)
    rel = pos[:, None] - pos[None, :]
    intra = np.where(rel >= 0, np.exp(log_g[:, None, None] * np.maximum(rel, 0.0)), 0.0)
    intra = np.transpose(intra, (1, 0, 2)).reshape(c, h * c)
    qdec = np.repeat(np.exp(log_g[:, None] * (pos[None, :] + 1.0)).T, RET_V, axis=1)
    kdect = np.repeat(np.exp(log_g[:, None] * (c - 1.0 - pos[None, :])), RET_QK, axis=0)
    cd = np.repeat(np.exp(log_g * c), RET_QK)[:, None] * np.ones((1, h * RET_V))
    return tuple(jnp.asarray(a, _F32) for a in (intra, qdec, kdect, cd))


def _retention(norm_w, n16, n32, t32, bn, s):
    c = RET_C
    per = CH // c
    intra, qdec, kdect, cd = _retention_consts(c)
    return pl.pallas_call(
        functools.partial(_ret_kernel, c=c),
        grid=(bn, s // c),
        in_specs=[
            pl.BlockSpec((1, c, 256), lambda b, i: (b, i, 2)),
            pl.BlockSpec((1, 1, 256, c), lambda b, i: (b, i // per, 0, i % per)),
            pl.BlockSpec((1, c, 256), lambda b, i: (b, i, 3)),
            pl.BlockSpec((1, c, 256), lambda b, i: (b, i, 0)),
            _const_spec(intra.shape), _const_spec(qdec.shape), _const_spec(kdect.shape),
            _const_spec(cd.shape), _const_spec(norm_w.shape),
        ],
        out_specs=pl.BlockSpec((1, c, 256), lambda b, i: (b, i, 0)),
        out_shape=jax.ShapeDtypeStruct((bn, s, 256), _BF16),
        scratch_shapes=[pltpu.VMEM((RET_HEADS * RET_QK, RET_HEADS * RET_V), _F32)],
        compiler_params=pltpu.CompilerParams(dimension_semantics=("parallel", "arbitrary")),
        name="retention",
    )(n16, t32, n16, n32, intra, qdec, kdect, cd, norm_w)


def _gla_kernel(q_ref, k_ref, a_ref, v_ref, vt_ref, g_ref, wa_ref, ba_ref, nw_ref, o_ref,
                st_ref, u_ref, oacc_ref, qs_ref, kk_ref, b_ref, qh_ref, dec_ref, *, ts, sub):
    nsub = ts // sub
    assert sub == 16

    @pl.when(pl.program_id(1) == 0)
    def _():
        st_ref[...] = jnp.zeros(st_ref.shape, _F32)

    la = jax.nn.log_sigmoid(_dot(a_ref[0].astype(_BF16), wa_ref[...]) + ba_ref[...]) * (1.0 / GLA_GATE_TEMP)
    in_blk = _iota(la.shape, 0) % sub
    b = la
    for sh in (1, 2, 4, 8):
        b = b + jnp.where(in_blk >= sh, pltpu.roll(b, sh, 0), 0.0)
    bl = jnp.where(in_blk == sub - 1, b, 0.0)
    for sh in (1, 2, 4, 8):
        bl = bl + jnp.where(in_blk < sub - sh, pltpu.roll(bl, ts - sh, 0), 0.0)
    qs = q_ref[0] * (GLA_QK ** -0.5)
    kk = k_ref[0]
    kd = (kk * jnp.exp(bl - b)).astype(_BF16)
    qs_ref[...] = qs
    kk_ref[...] = kk
    b_ref[...] = b
    qh_ref[...] = (qs * jnp.exp(b)).astype(_BF16)
    dec_ref[...] = jnp.exp(bl)
    vt = vt_ref[0, 0]
    row_blk = _iota(kd.shape, 0) // sub
    kd_wide = jnp.concatenate([jnp.where(row_blk == n, kd, jnp.zeros_like(kd)) for n in range(nsub)], axis=1)
    u_all = _dot(vt, kd_wide)
    for n in range(nsub):
        u_ref[n] = u_all[:, n * 128:(n + 1) * 128]

    st_keep = (_iota(st_ref.shape, 0) // GLA_V) == (_iota(st_ref.shape, 1) // GLA_QK)
    spread = jnp.where((_iota((128, 256), 0) // GLA_QK) == (_iota((128, 256), 1) // GLA_V), 1.0, 0.0).astype(_BF16)
    row16 = _iota((sub, 128), 0)

    def body(n, carry):
        r0 = pl.multiple_of(n * sub, sub)
        st = st_ref[...]
        o_cross = _dot_nt(qh_ref[pl.ds(r0, sub), :], st.astype(_BF16))
        q16 = qs_ref[pl.ds(r0, sub), :]
        k16 = kk_ref[pl.ds(r0, sub), :]
        b16 = b_ref[pl.ds(r0, sub), :]
        v16 = v_ref[0, pl.ds(r0, sub), :].astype(_F32)
        es = []
        for j in range(sub):
            e = q16 * k16[j:j + 1] * jnp.exp(jnp.minimum(b16 - b16[j:j + 1], 0.0))
            es.append(jnp.where(row16 >= j, e, 0.0))
        e_all = jnp.concatenate(es, axis=0)
        e_hi = e_all.astype(_BF16)
        e_lo = (e_all - e_hi.astype(_F32)).astype(_BF16)
        att = _dot(e_hi, spread) + _dot(e_lo, spread)
        o_diag = att[0:sub] * v16[0:1]
        for j in range(1, sub):
            o_diag = o_diag + att[j * sub:(j + 1) * sub] * v16[j:j + 1]
        oacc_ref[pl.ds(r0, sub), :] = o_cross + o_diag
        st_ref[...] = st * dec_ref[pl.ds(r0, 1), :] + jnp.where(st_keep, u_ref[n], 0.0)
        return carry

    lax.fori_loop(0, nsub, body, 0, unroll=2)

    o = oacc_ref[...]
    ms = _group_mean(o * o, GLA_V)
    y = o * lax.rsqrt(ms + LN_EPS) * nw_ref[...]
    gate = g_ref[0]
    o_ref[0] = (gate * jax.nn.sigmoid(gate) * y).astype(_BF16)


def _gla(wa, ba, norm_w, n16, n32, t16, bn, s):
    ts, sub = CH, GLA_SUB
    return pl.pallas_call(
        functools.partial(_gla_kernel, ts=ts, sub=sub),
        grid=(bn, s // ts),
        in_specs=[
            pl.BlockSpec((1, ts, 128), lambda b, i: (b, i, 4)),
            pl.BlockSpec((1, ts, 128), lambda b, i: (b, i, 5)),
            pl.BlockSpec((1, ts, 128), lambda b, i: (b, i, 6)),
            pl.BlockSpec((1, ts, 256), lambda b, i: (b, i, 4)),
            pl.BlockSpec((1, 1, 256, ts), lambda b, i: (b, i, 4, 0)),
            pl.BlockSpec((1, ts, 256), lambda b, i: (b, i, 1)),
            _const_spec(wa.shape), _const_spec(ba.shape), _const_spec(norm_w.shape),
        ],
        out_specs=pl.BlockSpec((1, ts, 256), lambda b, i: (b, i, 0)),
        out_shape=jax.ShapeDtypeStruct((bn, s, 256), _BF16),
        scratch_shapes=[
            pltpu.VMEM((GLA_HEADS * GLA_V, GLA_HEADS * GLA_QK), _F32),
            pltpu.VMEM((ts // sub, GLA_HEADS * GLA_V, GLA_HEADS * GLA_QK), _F32),
            pltpu.VMEM((ts, 256), _F32),
            pltpu.VMEM((ts, 128), _F32), pltpu.VMEM((ts, 128), _F32), pltpu.VMEM((ts, 128), _F32),
            pltpu.VMEM((ts, 128), _BF16), pltpu.VMEM((ts, 128), _F32),
        ],
        compiler_params=pltpu.CompilerParams(dimension_semantics=("parallel", "arbitrary")),
        name="gla",
    )(n32, n32, n32, n16, t16, n32, wa, ba, norm_w)


def _merge_kernel(x_ref, ya_ref, yb_ref, yc_ref, yd_ref, wg_ref, wbr_ref, wout_ref, lnw_ref, lnb_ref, h_ref):
    x = x_ref[...]
    xb = x.astype(_BF16)
    merged = None
    for n, y_ref in enumerate((ya_ref, yb_ref, yc_ref, yd_ref)):
        gate = jax.nn.sigmoid(_dot(xb, wg_ref[:, n * D_MODEL:(n + 1) * D_MODEL]))
        term = gate * _dot(y_ref[...], wbr_ref[n])
        merged = term if merged is None else merged + term
    mix = _dot(merged.astype(_BF16), wout_ref[...])
    h_ref[...] = _layer_norm(DEEPNORM_ALPHA * x + mix, lnw_ref[...], lnb_ref[...])


def _merge(x2, ys, wg, wbr, wout, lnw, lnb):
    t = x2.shape[0]
    ts = TS_DENSE
    tok = lambda w: pl.BlockSpec((ts, w), lambda i: (i, 0))
    return pl.pallas_call(
        _merge_kernel,
        grid=(t // ts,),
        in_specs=[tok(D_MODEL)] + [tok(BRANCH_W)] * 4 + [
            _const_spec(wg.shape), _const_spec(wbr.shape), _const_spec(wout.shape),
            _const_spec(lnw.shape), _const_spec(lnb.shape)],
        out_specs=tok(D_MODEL),
        out_shape=jax.ShapeDtypeStruct((t, D_MODEL), _F32),
        compiler_params=pltpu.CompilerParams(
            dimension_semantics=("parallel",), vmem_limit_bytes=V7X_VMEM_LIMIT),
        name="merge_ln",
    )(x2, *ys, wg, wbr, wout, lnw, lnb)


def _ffn_kernel(h_ref, p_ref, wup_ref, cw_ref, cb_ref, wdn_ref, wpg_ref, wpp_ref, lnw_ref, lnb_ref,
                o_ref, tail_ref, *, ts):
    @pl.when(pl.program_id(1) == 0)
    def _():
        tail_ref[...] = jnp.zeros(tail_ref.shape, _F32)

    h = h_ref[...]
    hb = h.astype(_BF16)
    row = _iota((ts, FF_COLS), 0)
    f = None
    for c0 in range(0, D_FF, FF_COLS):
        cols = slice(c0, c0 + FF_COLS)
        u = _dot(hb, wup_ref[:, cols])
        gt = _dot(hb, wup_ref[:, D_FF + c0:D_FF + c0 + FF_COLS])
        prev = tail_ref[:, cols]
        g1 = jnp.where(row == 0, prev[7:8], pltpu.roll(gt, 1, 0))
        g2 = jnp.where(row == 0, prev[6:7], jnp.where(row == 1, prev[7:8], pltpu.roll(gt, 2, 0)))
        tail_ref[:, cols] = gt[ts - 8:ts]
        gc = cb_ref[:, cols] + cw_ref[0:1, cols] * g2
        gc = gc + cw_ref[1:2, cols] * g1
        gc = gc + cw_ref[2:3, cols] * gt
        term = _dot((jax.nn.gelu(gc) * u).astype(_BF16), wdn_ref[cols, :])
        f = term if f is None else f + term
    e = jax.nn.sigmoid(_dot(hb, wpg_ref[...])) * _dot(p_ref[...].astype(_BF16), wpp_ref[...])
    o_ref[...] = _layer_norm(DEEPNORM_ALPHA * h + f + e, lnw_ref[...], lnb_ref[...])


def _ffn(h2, p2, wup, cw, cb, wdn, wpg, wpp, lnw, lnb, bn, s):
    ts = TS_DENSE
    ns = s // ts
    tok = lambda w: pl.BlockSpec((ts, w), lambda b, i: (b * ns + i, 0))
    return pl.pallas_call(
        functools.partial(_ffn_kernel, ts=ts),
        grid=(bn, ns),
        in_specs=[tok(D_MODEL), tok(P_DIM)] + [_const_spec(a.shape) for a in (wup, cw, cb, wdn, wpg, wpp, lnw, lnb)],
        out_specs=tok(D_MODEL),
        out_shape=jax.ShapeDtypeStruct((bn * s, D_MODEL), _F32),
        scratch_shapes=[pltpu.VMEM((8, D_FF), _F32)],
        compiler_params=pltpu.CompilerParams(
            dimension_semantics=("parallel", "arbitrary"), vmem_limit_bytes=V7X_VMEM_LIMIT),
        name="ffn_ple_ln",
    )(h2, p2, wup, cw, cb, wdn, wpg, wpp, lnw, lnb)


def kernel(x, p, w_in, a_lambda, a_norm_w, ret_norm_w, gla_w_a2, gla_b_a, gla_norm_w, w_branch, w_out,
           ln1_w, ln1_b, w_ffn_up, ffn_conv_w, ffn_conv_b, w_ffn_down, w_ple_gate, w_ple_proj, ln2_w, ln2_b):
    bn, s, _ = x.shape
    t = bn * s
    slopes = [2.0 ** (-(8.0 / N_SOFTMAX_HEADS) * i) for i in range(1, N_SOFTMAX_HEADS + 1)]
    slopes_a, slopes_b = tuple(slopes[0::2]), tuple(slopes[1::2])
    row = lambda v: v.astype(_F32).reshape(1, -1)
    x2 = x.reshape(t, D_MODEL)
    for i in range(DEPTH):
        *proj_w, wg = _prep_weights(w_in, i)
        n32, n16, t16, t32 = _project(x2, *proj_w, bn, s)
        n32 = n32.reshape(bn, s, -1)
        n16 = n16.reshape(bn, s, -1)
        lam_init = 0.8 - 0.6 * math.exp(-0.3 * i)
        y_a = _diff_attention(a_lambda[i].astype(_F32), a_norm_w[i].astype(_F32).reshape(DA_V, 1),
                              n16, t16, bn, s, slopes_a, lam_init)
        y_b = _dsa_attention(n16, t16, t32, bn, s, slopes_b)
        y_c = _retention(row(ret_norm_w[i]), n16, n32, t32, bn, s)
        wa = jnp.pad(gla_w_a2[i], ((0, 128 - GLA_RANK), (0, 0))).astype(_BF16)
        y_d = _gla(wa, row(gla_b_a[i]), row(jnp.tile(gla_norm_w[i], GLA_HEADS)), n16, n32, t16, bn, s)
        ys = [y.reshape(t, BRANCH_W) for y in (y_a, y_b, y_c, y_d)]
        h2 = _merge(x2, ys, wg, w_branch[i].astype(_BF16), w_out[i].astype(_BF16), row(ln1_w[i]), row(ln1_b[i]))
        x2 = _ffn(h2, p[i].reshape(t, P_DIM), w_ffn_up[i].astype(_BF16), ffn_conv_w[i].astype(_F32),
                  row(ffn_conv_b[i]), w_ffn_down[i].astype(_BF16), w_ple_gate[i].astype(_BF16),
                  w_ple_proj[i].astype(_BF16), row(ln2_w[i]), row(ln2_b[i]), bn, s)
    return x2.reshape(bn, s, D_MODEL)
```

```python
import functools
import math

import numpy as np
import jax
import jax.numpy as jnp
from jax import lax
from jax.experimental import pallas as pl
from jax.experimental.pallas import tpu as pltpu

D_MODEL = 1024
DEPTH = 2
P_DIM = 256
N_BRANCH = 4
BRANCH_W = 256
DA_HEADS = 4
DA_QK = 32
DA_V = 64
DSA_HEADS = 4
DSA_HD = 64
IDX_HEADS = 4
IDX_HD = 32
TOPK_MAX = 256
RET_HEADS = 4
RET_QK = 64
RET_V = 64
GLA_HEADS = 4
GLA_QK = 32
GLA_V = 64
GLA_RANK = 16
GLA_GATE_TEMP = 16.0
D_FF = 2816
CONV_W = 3
N_SOFTMAX_HEADS = DA_HEADS + DSA_HEADS
LN_EPS = 1e-5
NEG_INF = -1e30
DEEPNORM_ALPHA = (2.0 * DEPTH) ** 0.25

IN_SIZES = (256, 256, 256, 256, 256, 256, 128, 32, 4, 256, 256, 256, 256, 128, 128, 256, 16, 256, 4096)
IN_NAMES = ("a_q", "a_k", "a_v", "b_q", "b_k", "b_v", "b_iq", "b_ik", "b_iw",
            "c_q", "c_k", "c_v", "c_g", "d_q", "d_k", "d_v", "d_a", "d_g", "m_g")

_BF16 = jnp.bfloat16
_F32 = jnp.float32
_INT_MIN = -2 ** 31
_LOG2E = math.log2(math.e)

CH = 256
TQ = 256
RET_C = 128
GLA_SUB = 16
COUNT_UNROLL = 4
SCORE_UNROLL = 4
ATTEND_UNROLL = 4
TS_DENSE = 512
FF_COLS = 256
PREP_ROWS = 256
NEAR_CHUNKS = 4
SKIP_NATS = 32.0
FIXED_MAX_BITS = 100.0
V7X_VMEM_LIMIT = 56 * 1024 * 1024


def _dot(a, b, precision=None):
    return jnp.dot(a, b, preferred_element_type=_F32, precision=precision)


def _dot_nt(a, b):
    return lax.dot_general(a, b, (((1,), (1,)), ((), ())), preferred_element_type=_F32)


def _iota(shape, dim):
    return lax.broadcasted_iota(jnp.int32, shape, dim)


def _block_diag_tile(m_t, ngroups):
    r, tq = m_t.shape
    tiled = jnp.concatenate([m_t] * ngroups, axis=1)
    keep = (_iota(tiled.shape, 0) // (r // ngroups)) == (_iota(tiled.shape, 1) // tq)
    return jnp.where(keep, tiled, jnp.zeros_like(tiled))


def _layer_norm(x, w, b):
    mu = jnp.mean(x, -1, keepdims=True)
    var = jnp.mean(jnp.square(x - mu), -1, keepdims=True)
    return (x - mu) * lax.rsqrt(var + LN_EPS) * w + b


def _group_mean(x, group):
    lane_g = _iota(x.shape, 1) // group
    out = jnp.zeros_like(x)
    for h in range(x.shape[1] // group):
        mk = lane_g == h
        mh = jnp.sum(jnp.where(mk, x, 0.0), axis=1, keepdims=True) * (1.0 / group)
        out = jnp.where(mk, mh, out)
    return out


def _const_spec(shape):
    nd = len(shape)
    return pl.BlockSpec(shape, lambda *_: (0,) * nd, pipeline_mode=pl.Buffered(1))


def _prep_kernel(w_ref, wn32_ref, wn16_ref, wt16_ref, wt32_ref, wg_ref):
    offs = [0] + np.cumsum(IN_SIZES).tolist()
    col = {n: w_ref[0, :, offs[i]:offs[i + 1]] for i, n in enumerate(IN_NAMES)}
    zeros = lambda n: jnp.zeros((w_ref.shape[1], n), _F32)
    cat = lambda xs: jnp.concatenate(xs, axis=1)
    wn32_ref[...] = cat([col["c_g"], col["d_g"], col["d_q"], col["d_k"], col["d_a"],
                         zeros(128 - GLA_RANK)]).astype(_BF16)
    wn16_ref[...] = cat([col["a_k"], col["b_k"], col["c_q"], col["c_v"], col["d_v"]]
                        + [col["b_ik"]] * IDX_HEADS).astype(_BF16)
    wt16_ref[...] = cat([col["a_q"], col["a_v"], col["b_q"], col["b_v"], col["d_v"], col["b_iq"]]).T.astype(_BF16)
    wt32_ref[...] = cat([col["c_k"], col["b_iw"], zeros(16 - IDX_HEADS)]).T.astype(_BF16)
    wg_ref[...] = col["m_g"].astype(_BF16)


def _prep_weights(w_in, layer):
    rb = PREP_ROWS
    n32, n16, t16, t32, ng = 896, 1408, 1408, 272, N_BRANCH * D_MODEL
    rows = lambda n: pl.BlockSpec((rb, n), lambda r: (r, 0))
    cols = lambda n: pl.BlockSpec((n, rb), lambda r: (0, r))
    return pl.pallas_call(
        _prep_kernel,
        grid=(D_MODEL // rb,),
        in_specs=[pl.BlockSpec((1, rb, w_in.shape[2]), lambda r: (layer, r, 0))],
        out_specs=[rows(n32), rows(n16), cols(t16), cols(t32), rows(ng)],
        out_shape=[jax.ShapeDtypeStruct((D_MODEL, n32), _BF16), jax.ShapeDtypeStruct((D_MODEL, n16), _BF16),
                   jax.ShapeDtypeStruct((t16, D_MODEL), _BF16), jax.ShapeDtypeStruct((t32, D_MODEL), _BF16),
                   jax.ShapeDtypeStruct((D_MODEL, ng), _BF16)],
        compiler_params=pltpu.CompilerParams(
            dimension_semantics=("parallel",), vmem_limit_bytes=V7X_VMEM_LIMIT),
        name="prep_weights",
    )(w_in)


def _proj_kernel(x_ref, wn32_ref, wn16_ref, wt16_ref, wt32_ref, n32_ref, n16_ref, t16_ref, t32_ref):
    x = x_ref[...].astype(_BF16)
    n32_ref[...] = _dot(x, wn32_ref[...])
    n16_ref[...] = _dot(x, wn16_ref[...]).astype(_BF16)
    t16_ref[0, 0] = _dot_nt(wt16_ref[...], x).astype(_BF16)
    t32_ref[0, 0] = _dot_nt(wt32_ref[...], x)


def _project(x2, wn32, wn16, wt16, wt32, bn, s):
    ns = s // CH
    n32, n16, t16, t32 = wn32.shape[1], wn16.shape[1], wt16.shape[0], wt32.shape[0]
    return pl.pallas_call(
        _proj_kernel,
        grid=(bn, ns),
        in_specs=[
            pl.BlockSpec((CH, D_MODEL), lambda b, i: (b * ns + i, 0)),
            _const_spec(wn32.shape), _const_spec(wn16.shape), _const_spec(wt16.shape), _const_spec(wt32.shape),
        ],
        out_specs=[
            pl.BlockSpec((CH, n32), lambda b, i: (b * ns + i, 0)),
            pl.BlockSpec((CH, n16), lambda b, i: (b * ns + i, 0)),
            pl.BlockSpec((1, 1, t16, CH), lambda b, i: (b, i, 0, 0)),
            pl.BlockSpec((1, 1, t32, CH), lambda b, i: (b, i, 0, 0)),
        ],
        out_shape=[
            jax.ShapeDtypeStruct((bn * s, n32), _F32),
            jax.ShapeDtypeStruct((bn * s, n16), _BF16),
            jax.ShapeDtypeStruct((bn, ns, t16, CH), _BF16),
            jax.ShapeDtypeStruct((bn, ns, t32, CH), _F32),
        ],
        compiler_params=pltpu.CompilerParams(
            dimension_semantics=("parallel", "parallel"), vmem_limit_bytes=V7X_VMEM_LIMIT),
        name="proj",
    )(x2, wn32, wn16, wt16, wt32)


def _slope_row(slopes, reps, tq):
    return jnp.concatenate([jnp.full((1, tq), s * _LOG2E, _F32) for s in slopes for _ in range(reps)], axis=1)


def _alibi_rows(slopes):
    rows = _iota((CH, 128), 0).astype(_F32)
    return jnp.stack([rows * (s * _LOG2E) for s in slopes], axis=0)


def _key_norm_bound(k_ref, kn_ref, ngroups, nchunks):
    gt = jnp.where(_iota((16, 256), 0) == _iota((16, 256), 1) // (256 // ngroups), 1.0, 0.0).astype(_BF16)

    def body(j, best):
        kc = k_ref[0, pl.ds(pl.multiple_of(j * CH, CH), CH), :].astype(_F32)
        return jnp.maximum(best, _dot_nt(gt, (kc * kc).astype(_BF16)))

    best = lax.fori_loop(0, nchunks, body, jnp.zeros((16, CH), _F32))
    kn_ref[...] = jnp.broadcast_to(jnp.sqrt(jnp.max(best, axis=1, keepdims=True) * (1.0 + 2.0 ** -7)), kn_ref.shape)


def _logit_bound(qt, kn_ref, ngroups, scale):
    q = qt.astype(_F32)
    tq = q.shape[1]
    qn = jnp.sqrt(jnp.max(jnp.sum((q * q).reshape(ngroups, 256 // ngroups, tq), axis=1), axis=1, keepdims=True))
    return scale * qn * kn_ref[0:ngroups, 0:1]


def _first_chunks(bound, slopes, q0, nfull, nearest=0):
    per_head = bound.shape[0] // len(slopes)
    inv_slope = jnp.concatenate([jnp.full((per_head, 1), 1.0 / s, _F32) for s in slopes], axis=0)
    reach = ((2.0 * bound + SKIP_NATS) * inv_slope).astype(jnp.int32) + 2 + nearest
    last_far = q0 - CH + 1 - reach
    first = jnp.where(last_far < 0, 0, last_far // CH + 1)
    first = jnp.minimum(first, nfull)
    return [jnp.min(first[h * per_head:(h + 1) * per_head]) for h in range(len(slopes))]


def _logit_ceiling(bound, slopes, tq, nearest=None):
    per_head = bound.shape[0] // len(slopes)
    lane = _iota((1, tq), 1).astype(_F32)
    if nearest is not None:
        lane = lane - nearest
    rows = [bound[g:g + 1, :] * _LOG2E + (slopes[g // per_head] * _LOG2E) * lane for g in range(bound.shape[0])]
    deep = jnp.max(jnp.where(2.0 * _LOG2E * bound > FIXED_MAX_BITS, 1, 0)) > 0
    return jnp.concatenate(rows, axis=1), jnp.logical_not(deep)


def _attend(k_ref, vt_ref, bd_ref, ab_ref, slope_row, c1, q0, nfull, tq, w, mask_fn, aux0, first, ceiling=None):
    g_tq = bd_ref.shape[1]
    nheads = 4

    def pv(j, p, heads):
        vt_c = vt_ref[0, j]
        return {h: _dot(vt_c[h * 64:(h + 1) * 64, :], p[:, h * w:(h + 1) * w]) for h in heads}

    def qk(j, heads):
        kc = k_ref[0, pl.ds(pl.multiple_of(j * CH, CH), CH), :]
        tiles = sorted({c0 // 256 for h in heads for c0 in range(h * w, (h + 1) * w, 128)})
        return {t: _dot(kc, bd_ref[:, t * 256:(t + 1) * 256]) for t in tiles}

    def softmax(s, j, m, l, aux, heads, diag):
        crow = slope_row * (j * CH - q0).astype(_F32)
        amask, aux = mask_fn(j, aux, diag)
        ps, ms, ls, alphas = [], [], [], []
        for c0 in range(0, g_tq, 128):
            cols = slice(c0, c0 + 128)
            if c0 // w not in heads:
                ps.append(jnp.zeros((CH, 128), _BF16))
                ms.append(m[:, cols])
                ls.append(l[:, cols])
                alphas.append(jnp.ones((1, 128), _F32))
                continue
            t = s[c0 // 256][:, c0 % 256:c0 % 256 + 128] * c1 + ab_ref[c0 // w]
            if amask is not None:
                t = t + amask[:, c0 % tq:c0 % tq + 128]
            if ceiling is not None:
                p = jnp.exp2(t + (crow[:, cols] - ceiling[:, cols]))
                m_new, alpha = m[:, cols], jnp.ones((1, 128), _F32)
            else:
                m_new = jnp.maximum(m[:, cols], jnp.max(t, axis=0, keepdims=True) + crow[:, cols])
                alpha = jnp.exp2(m[:, cols] - m_new)
                p = jnp.exp2(t - (m_new - crow[:, cols]))
            ls.append(alpha * l[:, cols] + jnp.sum(p, axis=0, keepdims=True))
            ps.append(p.astype(_BF16))
            ms.append(m_new)
            alphas.append(alpha)
        cat = lambda xs: jnp.concatenate(xs, axis=1)
        return cat(ms), cat(ls), cat(alphas), cat(ps), aux

    def step(s, j, carry, heads, diag=False):
        m, l, acc, aux = carry
        m, l, alpha, p, aux = softmax(s, j, m, l, aux, heads, diag)
        pvs = pv(j, p, heads)
        if ceiling is not None:
            acc = [acc[h] + pvs[h] if h in heads else acc[h] for h in range(nheads)]
        else:
            acc = [alpha[:, h * w:(h + 1) * w] * acc[h] + pvs[h] if h in heads else acc[h] for h in range(nheads)]
        return m, l, acc, aux

    def run(lo, hi, heads, carry):
        unroll = ATTEND_UNROLL

        def group(i, c):
            j = lo + unroll * i
            ss = [qk(j + u, heads) for u in range(unroll)]
            for u in range(unroll):
                c = step(ss[u], j + u, c, heads)
            return c

        ngroup = jnp.maximum(hi - lo, 0) // unroll
        carry = lax.fori_loop(0, ngroup, group, carry)
        return lax.fori_loop(lo + unroll * ngroup, hi, lambda j, c: step(qk(j, heads), j, c, heads), carry)

    carry = (jnp.full((1, g_tq), NEG_INF, _F32), jnp.zeros((1, g_tq), _F32),
             [jnp.zeros((64, w), _F32) for _ in range(nheads)], aux0)
    lows, hi = [], nfull
    for h in range(nheads):
        u = ATTEND_UNROLL
        lo = jnp.maximum(hi - u * ((hi - jnp.minimum(first[h], hi) + u - 1) // u), 0)
        lows.append(lo)
        hi = lo
    for h in reversed(range(nheads)):
        carry = run(lows[h], lows[h - 1] if h else nfull, tuple(range(h, nheads)), carry)
    heads = tuple(range(nheads))
    m, l, acc, aux = step(qk(nfull, heads), nfull, carry, heads, True)
    return l, acc


def _attn_a_kernel(lam_ref, nw_ref, qt_ref, k_ref, vt_ref, o_ref, bd_ref, ab_ref, kn_ref, *, tq, slopes, lam_init):
    g = 2 * DA_HEADS
    q0 = pl.program_id(1) * tq
    nfull = q0 // CH

    @pl.when(pl.program_id(1) == 0)
    def _():
        _key_norm_bound(k_ref, kn_ref, g, k_ref.shape[1] // CH)

    bd_ref[...] = _block_diag_tile(qt_ref[0, 0], g)
    ab_ref[...] = _alibi_rows(slopes)

    def mask(j, aux, diag):
        if not diag:
            return None, aux
        rel = _iota((CH, tq), 0) - _iota((CH, tq), 1)
        return jnp.where(rel <= q0 - j * CH, 0.0, NEG_INF), aux

    bound = _logit_bound(qt_ref[0, 0], kn_ref, g, DA_QK ** -0.5)
    first = _first_chunks(bound, slopes, q0, nfull)
    ceiling, ceiling_ok = _logit_ceiling(bound, slopes, tq)

    def attend(ceil):
        l, acc = _attend(k_ref, vt_ref, bd_ref, ab_ref, _slope_row(slopes, 2, tq), DA_QK ** -0.5 * _LOG2E,
                         q0, nfull, tq, 2 * tq, mask, jnp.zeros((1, tq), _F32), first, ceil)
        lp = lam_ref[...]
        lam = (jnp.exp(jnp.sum(lp[0:1] * lp[1:2], axis=1, keepdims=True))
               - jnp.exp(jnp.sum(lp[2:3] * lp[3:4], axis=1, keepdims=True)) + lam_init)
        linv = 1.0 / l
        outs = []
        for h in range(DA_HEADS):
            a = acc[h] * linv[:, h * 2 * tq:(h + 1) * 2 * tq]
            o = a[:, :tq] - lam * a[:, tq:]
            ms = jnp.mean(o * o, axis=0, keepdims=True)
            outs.append(o * lax.rsqrt(ms + LN_EPS) * nw_ref[...] * (1.0 - lam_init))
        o_ref[0] = jnp.concatenate(outs, axis=0).T.astype(_BF16)

    @pl.when(ceiling_ok)
    def _():
        attend(ceiling)

    @pl.when(jnp.logical_not(ceiling_ok))
    def _():
        attend(None)


def _diff_attention(lam_p, norm_w, n16, t16, bn, s, slopes, lam_init):
    tq = TQ
    ns, per = s // CH, CH // tq
    g = 2 * DA_HEADS
    kern = functools.partial(_attn_a_kernel, tq=tq, slopes=slopes, lam_init=lam_init)
    return pl.pallas_call(
        kern,
        grid=(bn, s // tq),
        in_specs=[
            _const_spec(lam_p.shape), _const_spec(norm_w.shape),
            pl.BlockSpec((1, 1, 256, tq), lambda b, i: (b, i // per, 0, i % per)),
            pl.BlockSpec((1, s, 256), lambda b, i: (b, 0, 0)),
            pl.BlockSpec((1, ns, 256, CH), lambda b, i: (b, 0, 1, 0)),
        ],
        out_specs=pl.BlockSpec((1, tq, 256), lambda b, i: (b, i, 0)),
        out_shape=jax.ShapeDtypeStruct((bn, s, 256), _BF16),
        scratch_shapes=[pltpu.VMEM((256, g * tq), _BF16), pltpu.VMEM((DA_HEADS, CH, 128), _F32),
                        pltpu.VMEM((16, 128), _F32)],
        compiler_params=pltpu.CompilerParams(
            dimension_semantics=("parallel", "arbitrary"), vmem_limit_bytes=V7X_VMEM_LIMIT),
        name="diff_attn",
    )(lam_p, norm_w, t16, n16, t16)


def _bit_planes(rows):
    a = list(rows)
    j, m = 16, 0x0000FFFF
    while j:
        k = 0
        while k < 32:
            t = (a[k] ^ lax.shift_right_logical(a[k + j], jnp.int32(j))) & jnp.int32(m)
            a[k] = a[k] ^ t
            a[k + j] = a[k + j] ^ (t << j)
            k = (k + j + 1) & ~j
        j >>= 1
        if j:
            m = (m ^ (m << j)) & 0xFFFFFFFF
            m = m - (1 << 32) if m >= (1 << 31) else m
    return a[::-1]


def _dsa_kernel(iqt_ref, ik_ref, iwt_ref, qt_ref, k_ref, vt_ref, o_ref,
                iqbd_ref, bd_ref, key_ref, planes_ref, alive_ref, ab_ref, kn_ref, *, tq, slopes, topk):
    g = DSA_HEADS
    q0 = pl.program_id(1) * tq
    nfull = q0 // CH

    @pl.when(pl.program_id(1) == 0)
    def _():
        _key_norm_bound(k_ref, kn_ref, g, k_ref.shape[1] // CH)

    ngrp = (nfull + COUNT_UNROLL) // COUNT_UNROLL
    iqbd_ref[...] = _block_diag_tile(iqt_ref[0, 0], IDX_HEADS)
    bd_ref[...] = _block_diag_tile(qt_ref[0, 0], g)
    ab_ref[...] = _alibi_rows(slopes)
    w = iwt_ref[0, 0][0:IDX_HEADS, :] * (IDX_HEADS ** -0.5 * IDX_HD ** -0.5)

    def logits(j):
        return _dot(ik_ref[0, pl.ds(pl.multiple_of(j * CH, CH), CH), :], iqbd_ref[...])

    def score(lg, j, diag):
        half = CH // 2
        rows = []
        for r0 in (0, half):
            sc = jnp.maximum(lg[r0:r0 + half, 0:tq], 0.0) * w[0:1]
            for h in range(1, IDX_HEADS):
                sc = sc + jnp.maximum(lg[r0:r0 + half, h * tq:(h + 1) * tq], 0.0) * w[h:h + 1]
            sc = jnp.where(sc == 0.0, 0.0, sc)
            bits = pltpu.bitcast(sc, jnp.int32)
            key = bits ^ ((bits >> 31) & 0x7FFFFFFF)
            if diag:
                rel = _iota(key.shape, 0) - _iota(key.shape, 1)
                key = jnp.where(rel <= q0 - j * CH - r0, key, _INT_MIN)
            key_ref[j, r0:r0 + half, :] = key
            key3 = key.reshape(half // 8, 8, tq)
            rows += [key3[v] for v in range(half // 8)]
        planes = _bit_planes(rows)
        planes[31] = ~planes[31]
        planes_ref[j] = jnp.stack(planes, axis=0)
        if diag:
            lim = q0 - j * CH + _iota((8, tq), 1) - _iota((8, tq), 0)
            nbits = jnp.clip((lim >> 3) + 1, 0, 32)
            alive_ref[j] = jnp.where(nbits == 0, 0, jnp.left_shift(jnp.int32(-1), 32 - jnp.maximum(nbits, 1)))
        else:
            alive_ref[j] = jnp.full((8, tq), -1, jnp.int32)

    def score_group(i, c):
        lgs = [logits(SCORE_UNROLL * i + u) for u in range(SCORE_UNROLL)]
        for u in range(SCORE_UNROLL):
            score(lgs[u], SCORE_UNROLL * i + u, False)
        return c

    def score_one(j, c):
        score(logits(j), j, False)
        return c

    ngroup = nfull // SCORE_UNROLL
    lax.fori_loop(0, ngroup, score_group, 0)
    lax.fori_loop(SCORE_UNROLL * ngroup, nfull, score_one, 0)
    score(logits(nfull), nfull, True)

    for u in range(1, COUNT_UNROLL):
        @pl.when(nfull + u < ngrp * COUNT_UNROLL)
        def _():
            planes_ref[nfull + u] = jnp.zeros((32, 8, tq), jnp.int32)
            alive_ref[nfull + u] = jnp.zeros((8, tq), jnp.int32)

    def sweep(b_upd, keep, b_cnt):
        def body(gi, acc8):
            for u in range(COUNT_UNROLL):
                j = gi * COUNT_UNROLL + u
                a = alive_ref[j]
                if b_upd is not None:
                    a = a & ~(planes_ref[j, b_upd] ^ keep)
                    alive_ref[j] = a
                acc8 = acc8 + lax.population_count(a if b_cnt is None else a & planes_ref[j, b_cnt])
            return acc8
        acc8 = lax.fori_loop(0, ngrp, body, jnp.zeros((8, tq), jnp.int32))
        return jnp.sum(acc8, axis=0, keepdims=True)

    def decide(b, ones, want, thr):
        take = ones >= want
        thr = jnp.where(take, thr | jnp.left_shift(jnp.int32(1), b), thr)
        return jnp.where(take, want, want - ones), thr, jnp.where(take, -1, 0)

    want, thr, keep = decide(31, sweep(None, None, 31), jnp.full((1, tq), topk, jnp.int32),
                             jnp.zeros((1, tq), jnp.int32))

    def bit_body(i, c):
        want, thr, keep = c
        b = 30 - i
        return decide(b, sweep(b + 1, keep, b), want, thr)

    want, thr, keep = lax.fori_loop(0, 31, bit_body, (want, thr, keep))
    ties = sweep(0, keep, None)
    thr = thr ^ _INT_MIN
    need_tie = jnp.max(jnp.where((ties > want) & (thr > _INT_MIN), 1, 0)) > 0

    def attend(tie):
        if tie:
            first = [jnp.int32(0)] * g
            room = want.astype(_F32)
            lower = jnp.where(_iota((CH, CH), 0) > _iota((CH, CH), 1), 1.0, 0.0).astype(_BF16)
        else:
            thr_eff = jnp.maximum(thr, _INT_MIN + 1)

            def near_body(j, best):
                pos = jnp.where(key_ref[j] >= thr_eff, _iota((CH, tq), 0) + j * CH, -1)
                return jnp.maximum(best, jnp.max(pos, axis=0, keepdims=True))

            best = lax.fori_loop(jnp.maximum(nfull - (NEAR_CHUNKS - 1), 0), nfull + 1, near_body,
                                 jnp.full((1, tq), -1, jnp.int32))
            gap = jnp.where(best >= 0, q0 + _iota((1, tq), 1) - best, 2 ** 24)
            nearest = jnp.max(gap, axis=1, keepdims=True)
            bound = _logit_bound(qt_ref[0, 0], kn_ref, g, DSA_HD ** -0.5)
            first = _first_chunks(bound, slopes, q0, nfull, nearest)

        def mask(j, seen, diag):
            key = key_ref[j]
            if tie:
                eq = key == thr
                eqf = jnp.where(eq, 1.0, 0.0)
                rank = _dot(lower, eqf.astype(_BF16)) + seen
                sel = ((key > thr) | (eq & (rank < room))) & (key > _INT_MIN)
                seen = seen + jnp.sum(eqf, axis=0, keepdims=True)
            else:
                sel = key >= thr_eff
            return jnp.where(sel, 0.0, NEG_INF), seen

        def run(ceil):
            l, acc = _attend(k_ref, vt_ref, bd_ref, ab_ref, _slope_row(slopes, 1, tq), DSA_HD ** -0.5 * _LOG2E,
                             q0, nfull, tq, tq, mask, jnp.zeros((1, tq), _F32), first, ceil)
            linv = 1.0 / l
            outs = [acc[h] * linv[:, h * tq:(h + 1) * tq] for h in range(g)]
            o_ref[0] = jnp.concatenate(outs, axis=0).T.astype(_BF16)

        if tie:
            run(None)
        else:
            ceiling, ceiling_ok = _logit_ceiling(bound, slopes, tq, gap.astype(_F32))
            ceiling_ok = ceiling_ok & (jnp.min(best) >= 0)

            @pl.when(ceiling_ok)
            def _():
                run(ceiling)

            @pl.when(jnp.logical_not(ceiling_ok))
            def _():
                run(None)

    @pl.when(need_tie)
    def _():
        attend(True)

    @pl.when(jnp.logical_not(need_tie))
    def _():
        attend(False)


def _dsa_attention(n16, t16, t32, bn, s, slopes):
    tq = TQ
    ns, per = s // CH, CH // tq
    g = DSA_HEADS
    topk = min(TOPK_MAX, s // 4)
    assert ns % COUNT_UNROLL == 0
    kern = functools.partial(_dsa_kernel, tq=tq, slopes=slopes, topk=topk)
    return pl.pallas_call(
        kern,
        grid=(bn, s // tq),
        in_specs=[
            pl.BlockSpec((1, 1, 128, tq), lambda b, i: (b, i // per, 10, i % per)),
            pl.BlockSpec((1, s, 128), lambda b, i: (b, 0, 10)),
            pl.BlockSpec((1, 1, 16, tq), lambda b, i: (b, i // per, 16, i % per)),
            pl.BlockSpec((1, 1, 256, tq), lambda b, i: (b, i // per, 2, i % per)),
            pl.BlockSpec((1, s, 256), lambda b, i: (b, 0, 1)),
            pl.BlockSpec((1, ns, 256, CH), lambda b, i: (b, 0, 3, 0)),
        ],
        out_specs=pl.BlockSpec((1, tq, 256), lambda b, i: (b, i, 0)),
        out_shape=jax.ShapeDtypeStruct((bn, s, 256), _BF16),
        scratch_shapes=[
            pltpu.VMEM((128, IDX_HEADS * tq), _BF16), pltpu.VMEM((256, g * tq), _BF16),
            pltpu.VMEM((ns, CH, tq), jnp.int32), pltpu.VMEM((ns, 32, 8, tq), jnp.int32),
            pltpu.VMEM((ns, 8, tq), jnp.int32), pltpu.VMEM((DSA_HEADS, CH, 128), _F32),
            pltpu.VMEM((16, 128), _F32),
        ],
        compiler_params=pltpu.CompilerParams(
            dimension_semantics=("parallel", "arbitrary"), vmem_limit_bytes=V7X_VMEM_LIMIT),
        name="dsa_attn",
    )(t16, n16, t32, t16, n16, t16)


def _ret_kernel(q_ref, kt_ref, v_ref, g_ref, intra_ref, qdec_ref, kdect_ref, cd_ref, nw_ref, o_ref, s_ref, *, c):
    @pl.when(pl.program_id(1) == 0)
    def _():
        s_ref[...] = jnp.zeros(s_ref.shape, _F32)

    q = q_ref[0]
    v = v_ref[0]
    kt = kt_ref[0, 0] * (RET_QK ** -0.5)
    att = _dot(q, _block_diag_tile(kt.astype(_BF16), RET_HEADS)) * intra_ref[...]
    vt = jnp.concatenate([v] * RET_HEADS, axis=0)
    vbd = jnp.where((_iota(vt.shape, 0) // c) == (_iota(vt.shape, 1) // RET_V), vt, jnp.zeros_like(vt))
    st = s_ref[...]
    o = _dot(att.astype(_BF16), vbd) + _dot(q, st.astype(_BF16)) * qdec_ref[...]
    upd = _dot((kt * kdect_ref[...]).astype(_BF16), v)
    same_head = (_iota(upd.shape, 0) // RET_QK) == (_iota(upd.shape, 1) // RET_V)
    s_ref[...] = st * cd_ref[...] + jnp.where(same_head, upd, 0.0)

    mu = _group_mean(o, RET_V)
    d = o - mu
    var = _group_mean(d * d, RET_V)
    y = d * lax.rsqrt(var + LN_EPS) * nw_ref[...]
    gate = g_ref[0]
    o_ref[0] = (gate * jax.nn.sigmoid(gate) * y).astype(_BF16)


def _retention_consts(c):
    h = RET_HEADS
    log_g = np.log1p(-np.power(2.0, -5.0 - np.arange(h, dtype=np.float64)))
    pos = np.arange(c, dtype=np.float64)
    rel = pos[:, None] - pos[None, :]
    intra = np.where(rel >= 0, np.exp(log_g[:, None, None] * np.maximum(rel, 0.0)), 0.0)
    intra = np.transpose(intra, (1, 0, 2)).reshape(c, h * c)
    qdec = np.repeat(np.exp(log_g[:, None] * (pos[None, :] + 1.0)).T, RET_V, axis=1)
    kdect = np.repeat(np.exp(log_g[:, None] * (c - 1.0 - pos[None, :])), RET_QK, axis=0)
    cd = np.repeat(np.exp(log_g * c), RET_QK)[:, None] * np.ones((1, h * RET_V))
    return tuple(jnp.asarray(a, _F32) for a in (intra, qdec, kdect, cd))


def _retention(norm_w, n16, n32, t32, bn, s):
    c = RET_C
    per = CH // c
    intra, qdec, kdect, cd = _retention_consts(c)
    return pl.pallas_call(
        functools.partial(_ret_kernel, c=c),
        grid=(bn, s // c),
        in_specs=[
            pl.BlockSpec((1, c, 256), lambda b, i: (b, i, 2)),
            pl.BlockSpec((1, 1, 256, c), lambda b, i: (b, i // per, 0, i % per)),
            pl.BlockSpec((1, c, 256), lambda b, i: (b, i, 3)),
            pl.BlockSpec((1, c, 256), lambda b, i: (b, i, 0)),
            _const_spec(intra.shape), _const_spec(qdec.shape), _const_spec(kdect.shape),
            _const_spec(cd.shape), _const_spec(norm_w.shape),
        ],
        out_specs=pl.BlockSpec((1, c, 256), lambda b, i: (b, i, 0)),
        out_shape=jax.ShapeDtypeStruct((bn, s, 256), _BF16),
        scratch_shapes=[pltpu.VMEM((RET_HEADS * RET_QK, RET_HEADS * RET_V), _F32)],
        compiler_params=pltpu.CompilerParams(dimension_semantics=("parallel", "arbitrary")),
        name="retention",
    )(n16, t32, n16, n32, intra, qdec, kdect, cd, norm_w)


def _gla_kernel(q_ref, k_ref, a_ref, v_ref, vt_ref, g_ref, wa_ref, ba_ref, nw_ref, o_ref,
                st_ref, u_ref, oacc_ref, qs_ref, kk_ref, b_ref, qh_ref, dec_ref, *, ts, sub):
    nsub = ts // sub
    assert sub == 16

    @pl.when(pl.program_id(1) == 0)
    def _():
        st_ref[...] = jnp.zeros(st_ref.shape, _F32)

    la = jax.nn.log_sigmoid(_dot(a_ref[0].astype(_BF16), wa_ref[...]) + ba_ref[...]) * (1.0 / GLA_GATE_TEMP)
    in_blk = _iota(la.shape, 0) % sub
    b = la
    for sh in (1, 2, 4, 8):
        b = b + jnp.where(in_blk >= sh, pltpu.roll(b, sh, 0), 0.0)
    bl = jnp.where(in_blk == sub - 1, b, 0.0)
    for sh in (1, 2, 4, 8):
        bl = bl + jnp.where(in_blk < sub - sh, pltpu.roll(bl, ts - sh, 0), 0.0)
    qs = q_ref[0] * (GLA_QK ** -0.5)
    kk = k_ref[0]
    kd = (kk * jnp.exp(bl - b)).astype(_BF16)
    qs_ref[...] = qs
    kk_ref[...] = kk
    b_ref[...] = b
    qh_ref[...] = (qs * jnp.exp(b)).astype(_BF16)
    dec_ref[...] = jnp.exp(bl)
    vt = vt_ref[0, 0]
    row_blk = _iota(kd.shape, 0) // sub
    kd_wide = jnp.concatenate([jnp.where(row_blk == n, kd, jnp.zeros_like(kd)) for n in range(nsub)], axis=1)
    u_all = _dot(vt, kd_wide)
    for n in range(nsub):
        u_ref[n] = u_all[:, n * 128:(n + 1) * 128]

    st_keep = (_iota(st_ref.shape, 0) // GLA_V) == (_iota(st_ref.shape, 1) // GLA_QK)
    spread = jnp.where((_iota((128, 256), 0) // GLA_QK) == (_iota((128, 256), 1) // GLA_V), 1.0, 0.0).astype(_BF16)
    row16 = _iota((sub, 128), 0)

    def body(n, carry):
        r0 = pl.multiple_of(n * sub, sub)
        st = st_ref[...]
        o_cross = _dot_nt(qh_ref[pl.ds(r0, sub), :], st.astype(_BF16))
        q16 = qs_ref[pl.ds(r0, sub), :]
        k16 = kk_ref[pl.ds(r0, sub), :]
        b16 = b_ref[pl.ds(r0, sub), :]
        v16 = v_ref[0, pl.ds(r0, sub), :].astype(_F32)
        es = []
        for j in range(sub):
            e = q16 * k16[j:j + 1] * jnp.exp(jnp.minimum(b16 - b16[j:j + 1], 0.0))
            es.append(jnp.where(row16 >= j, e, 0.0))
        e_all = jnp.concatenate(es, axis=0)
        e_hi = e_all.astype(_BF16)
        e_lo = (e_all - e_hi.astype(_F32)).astype(_BF16)
        att = _dot(e_hi, spread) + _dot(e_lo, spread)
        o_diag = att[0:sub] * v16[0:1]
        for j in range(1, sub):
            o_diag = o_diag + att[j * sub:(j + 1) * sub] * v16[j:j + 1]
        oacc_ref[pl.ds(r0, sub), :] = o_cross + o_diag
        st_ref[...] = st * dec_ref[pl.ds(r0, 1), :] + jnp.where(st_keep, u_ref[n], 0.0)
        return carry

    lax.fori_loop(0, nsub, body, 0, unroll=2)

    o = oacc_ref[...]
    ms = _group_mean(o * o, GLA_V)
    y = o * lax.rsqrt(ms + LN_EPS) * nw_ref[...]
    gate = g_ref[0]
    o_ref[0] = (gate * jax.nn.sigmoid(gate) * y).astype(_BF16)


def _gla(wa, ba, norm_w, n16, n32, t16, bn, s):
    ts, sub = CH, GLA_SUB
    return pl.pallas_call(
        functools.partial(_gla_kernel, ts=ts, sub=sub),
        grid=(bn, s // ts),
        in_specs=[
            pl.BlockSpec((1, ts, 128), lambda b, i: (b, i, 4)),
            pl.BlockSpec((1, ts, 128), lambda b, i: (b, i, 5)),
            pl.BlockSpec((1, ts, 128), lambda b, i: (b, i, 6)),
            pl.BlockSpec((1, ts, 256), lambda b, i: (b, i, 4)),
            pl.BlockSpec((1, 1, 256, ts), lambda b, i: (b, i, 4, 0)),
            pl.BlockSpec((1, ts, 256), lambda b, i: (b, i, 1)),
            _const_spec(wa.shape), _const_spec(ba.shape), _const_spec(norm_w.shape),
        ],
        out_specs=pl.BlockSpec((1, ts, 256), lambda b, i: (b, i, 0)),
        out_shape=jax.ShapeDtypeStruct((bn, s, 256), _BF16),
        scratch_shapes=[
            pltpu.VMEM((GLA_HEADS * GLA_V, GLA_HEADS * GLA_QK), _F32),
            pltpu.VMEM((ts // sub, GLA_HEADS * GLA_V, GLA_HEADS * GLA_QK), _F32),
            pltpu.VMEM((ts, 256), _F32),
            pltpu.VMEM((ts, 128), _F32), pltpu.VMEM((ts, 128), _F32), pltpu.VMEM((ts, 128), _F32),
            pltpu.VMEM((ts, 128), _BF16), pltpu.VMEM((ts, 128), _F32),
        ],
        compiler_params=pltpu.CompilerParams(dimension_semantics=("parallel", "arbitrary")),
        name="gla",
    )(n32, n32, n32, n16, t16, n32, wa, ba, norm_w)


def _merge_kernel(x_ref, ya_ref, yb_ref, yc_ref, yd_ref, wg_ref, wbr_ref, wout_ref, lnw_ref, lnb_ref, h_ref):
    x = x_ref[...]
    xb = x.astype(_BF16)
    merged = None
    for n, y_ref in enumerate((ya_ref, yb_ref, yc_ref, yd_ref)):
        gate = jax.nn.sigmoid(_dot(xb, wg_ref[:, n * D_MODEL:(n + 1) * D_MODEL]))
        term = gate * _dot(y_ref[...], wbr_ref[n])
        merged = term if merged is None else merged + term
    mix = _dot(merged.astype(_BF16), wout_ref[...])
    h_ref[...] = _layer_norm(DEEPNORM_ALPHA * x + mix, lnw_ref[...], lnb_ref[...])


def _merge(x2, ys, wg, wbr, wout, lnw, lnb):
    t = x2.shape[0]
    ts = TS_DENSE
    tok = lambda w: pl.BlockSpec((ts, w), lambda i: (i, 0))
    return pl.pallas_call(
        _merge_kernel,
        grid=(t // ts,),
        in_specs=[tok(D_MODEL)] + [tok(BRANCH_W)] * 4 + [
            _const_spec(wg.shape), _const_spec(wbr.shape), _const_spec(wout.shape),
            _const_spec(lnw.shape), _const_spec(lnb.shape)],
        out_specs=tok(D_MODEL),
        out_shape=jax.ShapeDtypeStruct((t, D_MODEL), _F32),
        compiler_params=pltpu.CompilerParams(
            dimension_semantics=("parallel",), vmem_limit_bytes=V7X_VMEM_LIMIT),
        name="merge_ln",
    )(x2, *ys, wg, wbr, wout, lnw, lnb)


def _ffn_kernel(h_ref, p_ref, wup_ref, cw_ref, cb_ref, wdn_ref, wpg_ref, wpp_ref, lnw_ref, lnb_ref,
                o_ref, tail_ref, *, ts):
    @pl.when(pl.program_id(1) == 0)
    def _():
        tail_ref[...] = jnp.zeros(tail_ref.shape, _F32)

    h = h_ref[...]
    hb = h.astype(_BF16)
    row = _iota((ts, FF_COLS), 0)
    f = None
    for c0 in range(0, D_FF, FF_COLS):
        cols = slice(c0, c0 + FF_COLS)
        u = _dot(hb, wup_ref[:, cols])
        gt = _dot(hb, wup_ref[:, D_FF + c0:D_FF + c0 + FF_COLS])
        prev = tail_ref[:, cols]
        g1 = jnp.where(row == 0, prev[7:8], pltpu.roll(gt, 1, 0))
        g2 = jnp.where(row == 0, prev[6:7], jnp.where(row == 1, prev[7:8], pltpu.roll(gt, 2, 0)))
        tail_ref[:, cols] = gt[ts - 8:ts]
        gc = cb_ref[:, cols] + cw_ref[0:1, cols] * g2
        gc = gc + cw_ref[1:2, cols] * g1
        gc = gc + cw_ref[2:3, cols] * gt
        term = _dot((jax.nn.gelu(gc) * u).astype(_BF16), wdn_ref[cols, :])
        f = term if f is None else f + term
    e = jax.nn.sigmoid(_dot(hb, wpg_ref[...])) * _dot(p_ref[...].astype(_BF16), wpp_ref[...])
    o_ref[...] = _layer_norm(DEEPNORM_ALPHA * h + f + e, lnw_ref[...], lnb_ref[...])


def _ffn(h2, p2, wup, cw, cb, wdn, wpg, wpp, lnw, lnb, bn, s):
    ts = TS_DENSE
    ns = s // ts
    tok = lambda w: pl.BlockSpec((ts, w), lambda b, i: (b * ns + i, 0))
    return pl.pallas_call(
        functools.partial(_ffn_kernel, ts=ts),
        grid=(bn, ns),
        in_specs=[tok(D_MODEL), tok(P_DIM)] + [_const_spec(a.shape) for a in (wup, cw, cb, wdn, wpg, wpp, lnw, lnb)],
        out_specs=tok(D_MODEL),
        out_shape=jax.ShapeDtypeStruct((bn * s, D_MODEL), _F32),
        scratch_shapes=[pltpu.VMEM((8, D_FF), _F32)],
        compiler_params=pltpu.CompilerParams(
            dimension_semantics=("parallel", "arbitrary"), vmem_limit_bytes=V7X_VMEM_LIMIT),
        name="ffn_ple_ln",
    )(h2, p2, wup, cw, cb, wdn, wpg, wpp, lnw, lnb)


def kernel(x, p, w_in, a_lambda, a_norm_w, ret_norm_w, gla_w_a2, gla_b_a, gla_norm_w, w_branch, w_out,
           ln1_w, ln1_b, w_ffn_up, ffn_conv_w, ffn_conv_b, w_ffn_down, w_ple_gate, w_ple_proj, ln2_w, ln2_b):
    bn, s, _ = x.shape
    t = bn * s
    slopes = [2.0 ** (-(8.0 / N_SOFTMAX_HEADS) * i) for i in range(1, N_SOFTMAX_HEADS + 1)]
    slopes_a, slopes_b = tuple(slopes[0::2]), tuple(slopes[1::2])
    row = lambda v: v.astype(_F32).reshape(1, -1)
    x2 = x.reshape(t, D_MODEL)
    for i in range(DEPTH):
        *proj_w, wg = _prep_weights(w_in, i)
        n32, n16, t16, t32 = _project(x2, *proj_w, bn, s)
        n32 = n32.reshape(bn, s, -1)
        n16 = n16.reshape(bn, s, -1)
        lam_init = 0.8 - 0.6 * math.exp(-0.3 * i)
        y_a = _diff_attention(a_lambda[i].astype(_F32), a_norm_w[i].astype(_F32).reshape(DA_V, 1),
                              n16, t16, bn, s, slopes_a, lam_init)
        y_b = _dsa_attention(n16, t16, t32, bn, s, slopes_b)
        y_c = _retention(row(ret_norm_w[i]), n16, n32, t32, bn, s)
        wa = jnp.pad(gla_w_a2[i], ((0, 128 - GLA_RANK), (0, 0))).astype(_BF16)
        y_d = _gla(wa, row(gla_b_a[i]), row(jnp.tile(gla_norm_w[i], GLA_HEADS)), n16, n32, t16, bn, s)
        ys = [y.reshape(t, BRANCH_W) for y in (y_a, y_b, y_c, y_d)]
        h2 = _merge(x2, ys, wg, w_branch[i].astype(_BF16), w_out[i].astype(_BF16), row(ln1_w[i]), row(ln1_b[i]))
        x2 = _ffn(h2, p[i].reshape(t, P_DIM), w_ffn_up[i].astype(_BF16), ffn_conv_w[i].astype(_F32),
                  row(ffn_conv_b[i]), w_ffn_down[i].astype(_BF16), w_ple_gate[i].astype(_BF16),
                  w_ple_proj[i].astype(_BF16), row(ln2_w[i]), row(ln2_b[i]), bn, s)
    return x2.reshape(bn, s, D_MODEL)
```

```python
import functools
import math

import numpy as np
import jax
import jax.numpy as jnp
from jax import lax
from jax.experimental import pallas as pl
from jax.experimental.pallas import tpu as pltpu

D_MODEL = 1024
DEPTH = 2
P_DIM = 256
N_BRANCH = 4
BRANCH_W = 256
DA_HEADS = 4
DA_QK = 32
DA_V = 64
DSA_HEADS = 4
DSA_HD = 64
IDX_HEADS = 4
IDX_HD = 32
TOPK_MAX = 256
RET_HEADS = 4
RET_QK = 64
RET_V = 64
GLA_HEADS = 4
GLA_QK = 32
GLA_V = 64
GLA_RANK = 16
GLA_GATE_TEMP = 16.0
D_FF = 2816
CONV_W = 3
N_SOFTMAX_HEADS = DA_HEADS + DSA_HEADS
LN_EPS = 1e-5
NEG_INF = -1e30
DEEPNORM_ALPHA = (2.0 * DEPTH) ** 0.25

IN_SIZES = (256, 256, 256, 256, 256, 256, 128, 32, 4, 256, 256, 256, 256, 128, 128, 256, 16, 256, 4096)
IN_NAMES = ("a_q", "a_k", "a_v", "b_q", "b_k", "b_v", "b_iq", "b_ik", "b_iw",
            "c_q", "c_k", "c_v", "c_g", "d_q", "d_k", "d_v", "d_a", "d_g", "m_g")

_BF16 = jnp.bfloat16
_F32 = jnp.float32
_INT_MIN = -2 ** 31
_LOG2E = math.log2(math.e)

CH = 256
TQ = 256
RET_C = 128
GLA_SUB = 16
COUNT_UNROLL = 4
SCORE_UNROLL = 4
ATTEND_UNROLL = 4
TS_DENSE = 512
FF_COLS = 256
PREP_ROWS = 256
NEAR_CHUNKS = 4
SKIP_NATS = 32.0
FIXED_MAX_BITS = 100.0
V7X_VMEM_LIMIT = 56 * 1024 * 1024


def _dot(a, b, precision=None):
    return jnp.dot(a, b, preferred_element_type=_F32, precision=precision)


def _dot_nt(a, b):
    return lax.dot_general(a, b, (((1,), (1,)), ((), ())), preferred_element_type=_F32)


def _iota(shape, dim):
    return lax.broadcasted_iota(jnp.int32, shape, dim)


def _block_diag_tile(m_t, ngroups):
    r, tq = m_t.shape
    tiled = jnp.concatenate([m_t] * ngroups, axis=1)
    keep = (_iota(tiled.shape, 0) // (r // ngroups)) == (_iota(tiled.shape, 1) // tq)
    return jnp.where(keep, tiled, jnp.zeros_like(tiled))


def _layer_norm(x, w, b):
    mu = jnp.mean(x, -1, keepdims=True)
    var = jnp.mean(jnp.square(x - mu), -1, keepdims=True)
    return (x - mu) * lax.rsqrt(var + LN_EPS) * w + b


def _group_mean(x, group):
    lane_g = _iota(x.shape, 1) // group
    out = jnp.zeros_like(x)
    for h in range(x.shape[1] // group):
        mk = lane_g == h
        mh = jnp.sum(jnp.where(mk, x, 0.0), axis=1, keepdims=True) * (1.0 / group)
        out = jnp.where(mk, mh, out)
    return out


def _const_spec(shape):
    nd = len(shape)
    return pl.BlockSpec(shape, lambda *_: (0,) * nd, pipeline_mode=pl.Buffered(1))


def _prep_kernel(w_ref, wn32_ref, wn16_ref, wt16_ref, wt32_ref, wg_ref):
    offs = [0] + np.cumsum(IN_SIZES).tolist()
    col = {n: w_ref[0, :, offs[i]:offs[i + 1]] for i, n in enumerate(IN_NAMES)}
    zeros = lambda n: jnp.zeros((w_ref.shape[1], n), _F32)
    cat = lambda xs: jnp.concatenate(xs, axis=1)
    wn32_ref[...] = cat([col["c_g"], col["d_g"], col["d_q"], col["d_k"], col["d_a"],
                         zeros(128 - GLA_RANK)]).astype(_BF16)
    wn16_ref[...] = cat([col["a_k"], col["b_k"], col["c_q"], col["c_v"], col["d_v"]]
                        + [col["b_ik"]] * IDX_HEADS).astype(_BF16)
    wt16_ref[...] = cat([col["a_q"], col["a_v"], col["b_q"], col["b_v"], col["d_v"], col["b_iq"]]).T.astype(_BF16)
    wt32_ref[...] = cat([col["c_k"], col["b_iw"], zeros(16 - IDX_HEADS)]).T.astype(_BF16)
    wg_ref[...] = col["m_g"].astype(_BF16)


def _prep_weights(w_in, layer):
    rb = PREP_ROWS
    n32, n16, t16, t32, ng = 896, 1408, 1408, 272, N_BRANCH * D_MODEL
    rows = lambda n: pl.BlockSpec((rb, n), lambda r: (r, 0))
    cols = lambda n: pl.BlockSpec((n, rb), lambda r: (0, r))
    return pl.pallas_call(
        _prep_kernel,
        grid=(D_MODEL // rb,),
        in_specs=[pl.BlockSpec((1, rb, w_in.shape[2]), lambda r: (layer, r, 0))],
        out_specs=[rows(n32), rows(n16), cols(t16), cols(t32), rows(ng)],
        out_shape=[jax.ShapeDtypeStruct((D_MODEL, n32), _BF16), jax.ShapeDtypeStruct((D_MODEL, n16), _BF16),
                   jax.ShapeDtypeStruct((t16, D_MODEL), _BF16), jax.ShapeDtypeStruct((t32, D_MODEL), _BF16),
                   jax.ShapeDtypeStruct((D_MODEL, ng), _BF16)],
        compiler_params=pltpu.CompilerParams(
            dimension_semantics=("parallel",), vmem_limit_bytes=V7X_VMEM_LIMIT),
        name="prep_weights",
    )(w_in)


def _proj_kernel(x_ref, wn32_ref, wn16_ref, wt16_ref, wt32_ref, n32_ref, n16_ref, t16_ref, t32_ref):
    x = x_ref[...].astype(_BF16)
    n32_ref[...] = _dot(x, wn32_ref[...])
    n16_ref[...] = _dot(x, wn16_ref[...]).astype(_BF16)
    t16_ref[0, 0] = _dot_nt(wt16_ref[...], x).astype(_BF16)
    t32_ref[0, 0] = _dot_nt(wt32_ref[...], x)


def _project(x2, wn32, wn16, wt16, wt32, bn, s):
    ns = s // CH
    n32, n16, t16, t32 = wn32.shape[1], wn16.shape[1], wt16.shape[0], wt32.shape[0]
    return pl.pallas_call(
        _proj_kernel,
        grid=(bn, ns),
        in_specs=[
            pl.BlockSpec((CH, D_MODEL), lambda b, i: (b * ns + i, 0)),
            _const_spec(wn32.shape), _const_spec(wn16.shape), _const_spec(wt16.shape), _const_spec(wt32.shape),
        ],
        out_specs=[
            pl.BlockSpec((CH, n32), lambda b, i: (b * ns + i, 0)),
            pl.BlockSpec((CH, n16), lambda b, i: (b * ns + i, 0)),
            pl.BlockSpec((1, 1, t16, CH), lambda b, i: (b, i, 0, 0)),
            pl.BlockSpec((1, 1, t32, CH), lambda b, i: (b, i, 0, 0)),
        ],
        out_shape=[
            jax.ShapeDtypeStruct((bn * s, n32), _F32),
            jax.ShapeDtypeStruct((bn * s, n16), _BF16),
            jax.ShapeDtypeStruct((bn, ns, t16, CH), _BF16),
            jax.ShapeDtypeStruct((bn, ns, t32, CH), _F32),
        ],
        compiler_params=pltpu.CompilerParams(
            dimension_semantics=("parallel", "parallel"), vmem_limit_bytes=V7X_VMEM_LIMIT),
        name="proj",
    )(x2, wn32, wn16, wt16, wt32)


def _slope_row(slopes, reps, tq):
    return jnp.concatenate([jnp.full((1, tq), s * _LOG2E, _F32) for s in slopes for _ in range(reps)], axis=1)


def _alibi_rows(slopes):
    rows = _iota((CH, 128), 0).astype(_F32)
    return jnp.stack([rows * (s * _LOG2E) for s in slopes], axis=0)


def _key_norm_bound(k_ref, kn_ref, ngroups, nchunks):
    gt = jnp.where(_iota((16, 256), 0) == _iota((16, 256), 1) // (256 // ngroups), 1.0, 0.0).astype(_BF16)

    def body(j, best):
        kc = k_ref[0, pl.ds(pl.multiple_of(j * CH, CH), CH), :].astype(_F32)
        return jnp.maximum(best, _dot_nt(gt, (kc * kc).astype(_BF16)))

    best = lax.fori_loop(0, nchunks, body, jnp.zeros((16, CH), _F32))
    kn_ref[...] = jnp.broadcast_to(jnp.sqrt(jnp.max(best, axis=1, keepdims=True) * (1.0 + 2.0 ** -7)), kn_ref.shape)


def _logit_bound(qt, kn_ref, ngroups, scale):
    q = qt.astype(_F32)
    tq = q.shape[1]
    qn = jnp.sqrt(jnp.max(jnp.sum((q * q).reshape(ngroups, 256 // ngroups, tq), axis=1), axis=1, keepdims=True))
    return scale * qn * kn_ref[0:ngroups, 0:1]


def _first_chunks(bound, slopes, q0, nfull, nearest=0):
    per_head = bound.shape[0] // len(slopes)
    inv_slope = jnp.concatenate([jnp.full((per_head, 1), 1.0 / s, _F32) for s in slopes], axis=0)
    reach = ((2.0 * bound + SKIP_NATS) * inv_slope).astype(jnp.int32) + 2 + nearest
    last_far = q0 - CH + 1 - reach
    first = jnp.where(last_far < 0, 0, last_far // CH + 1)
    first = jnp.minimum(first, nfull)
    return [jnp.min(first[h * per_head:(h + 1) * per_head]) for h in range(len(slopes))]


def _logit_ceiling(bound, slopes, tq, nearest=None):
    per_head = bound.shape[0] // len(slopes)
    lane = _iota((1, tq), 1).astype(_F32)
    if nearest is not None:
        lane = lane - nearest
    rows = [bound[g:g + 1, :] * _LOG2E + (slopes[g // per_head] * _LOG2E) * lane for g in range(bound.shape[0])]
    deep = jnp.max(jnp.where(2.0 * _LOG2E * bound > FIXED_MAX_BITS, 1, 0)) > 0
    return jnp.concatenate(rows, axis=1), jnp.logical_not(deep)


def _attend(k_ref, vt_ref, bd_ref, ab_ref, slope_row, c1, q0, nfull, tq, w, mask_fn, aux0, first, ceiling=None):
    g_tq = bd_ref.shape[1]
    nheads = 4

    def pv(j, p, heads):
        vt_c = vt_ref[0, j]
        return {h: _dot(vt_c[h * 64:(h + 1) * 64, :], p[:, h * w:(h + 1) * w]) for h in heads}

    def qk(j, heads):
        kc = k_ref[0, pl.ds(pl.multiple_of(j * CH, CH), CH), :]
        tiles = sorted({c0 // 256 for h in heads for c0 in range(h * w, (h + 1) * w, 128)})
        return {t: _dot(kc, bd_ref[:, t * 256:(t + 1) * 256]) for t in tiles}

    def softmax(s, j, m, l, aux, heads, diag):
        crow = slope_row * (j * CH - q0).astype(_F32)
        amask, aux = mask_fn(j, aux, diag)
        ps, ms, ls, alphas = [], [], [], []
        for c0 in range(0, g_tq, 128):
            cols = slice(c0, c0 + 128)
            if c0 // w not in heads:
                ps.append(jnp.zeros((CH, 128), _BF16))
                ms.append(m[:, cols])
                ls.append(l[:, cols])
                alphas.append(jnp.ones((1, 128), _F32))
                continue
            t = s[c0 // 256][:, c0 % 256:c0 % 256 + 128] * c1 + ab_ref[c0 // w]
            if amask is not None:
                t = t + amask[:, c0 % tq:c0 % tq + 128]
            if ceiling is not None:
                p = jnp.exp2(t + (crow[:, cols] - ceiling[:, cols]))
                m_new, alpha = m[:, cols], jnp.ones((1, 128), _F32)
            else:
                m_new = jnp.maximum(m[:, cols], jnp.max(t, axis=0, keepdims=True) + crow[:, cols])
                alpha = jnp.exp2(m[:, cols] - m_new)
                p = jnp.exp2(t - (m_new - crow[:, cols]))
            ls.append(alpha * l[:, cols] + jnp.sum(p, axis=0, keepdims=True))
            ps.append(p.astype(_BF16))
            ms.append(m_new)
            alphas.append(alpha)
        cat = lambda xs: jnp.concatenate(xs, axis=1)
        return cat(ms), cat(ls), cat(alphas), cat(ps), aux

    def step(s, j, carry, heads, diag=False):
        m, l, acc, aux = carry
        m, l, alpha, p, aux = softmax(s, j, m, l, aux, heads, diag)
        pvs = pv(j, p, heads)
        if ceiling is not None:
            acc = [acc[h] + pvs[h] if h in heads else acc[h] for h in range(nheads)]
        else:
            acc = [alpha[:, h * w:(h + 1) * w] * acc[h] + pvs[h] if h in heads else acc[h] for h in range(nheads)]
        return m, l, acc, aux

    def run(lo, hi, heads, carry):
        unroll = ATTEND_UNROLL

        def group(i, c):
            j = lo + unroll * i
            ss = [qk(j + u, heads) for u in range(unroll)]
            for u in range(unroll):
                c = step(ss[u], j + u, c, heads)
            return c

        ngroup = jnp.maximum(hi - lo, 0) // unroll
        carry = lax.fori_loop(0, ngroup, group, carry)
        return lax.fori_loop(lo + unroll * ngroup, hi, lambda j, c: step(qk(j, heads), j, c, heads), carry)

    carry = (jnp.full((1, g_tq), NEG_INF, _F32), jnp.zeros((1, g_tq), _F32),
             [jnp.zeros((64, w), _F32) for _ in range(nheads)], aux0)
    lows, hi = [], nfull
    for h in range(nheads):
        u = ATTEND_UNROLL
        lo = jnp.maximum(hi - u * ((hi - jnp.minimum(first[h], hi) + u - 1) // u), 0)
        lows.append(lo)
        hi = lo
    for h in reversed(range(nheads)):
        carry = run(lows[h], lows[h - 1] if h else nfull, tuple(range(h, nheads)), carry)
    heads = tuple(range(nheads))
    m, l, acc, aux = step(qk(nfull, heads), nfull, carry, heads, True)
    return l, acc


def _attn_a_kernel(lam_ref, nw_ref, qt_ref, k_ref, vt_ref, o_ref, bd_ref, ab_ref, kn_ref, *, tq, slopes, lam_init):
    g = 2 * DA_HEADS
    q0 = pl.program_id(1) * tq
    nfull = q0 // CH

    @pl.when(pl.program_id(1) == 0)
    def _():
        _key_norm_bound(k_ref, kn_ref, g, k_ref.shape[1] // CH)

    bd_ref[...] = _block_diag_tile(qt_ref[0, 0], g)
    ab_ref[...] = _alibi_rows(slopes)

    def mask(j, aux, diag):
        if not diag:
            return None, aux
        rel = _iota((CH, tq), 0) - _iota((CH, tq), 1)
        return jnp.where(rel <= q0 - j * CH, 0.0, NEG_INF), aux

    bound = _logit_bound(qt_ref[0, 0], kn_ref, g, DA_QK ** -0.5)
    first = _first_chunks(bound, slopes, q0, nfull)
    ceiling, ceiling_ok = _logit_ceiling(bound, slopes, tq)

    def attend(ceil):
        l, acc = _attend(k_ref, vt_ref, bd_ref, ab_ref, _slope_row(slopes, 2, tq), DA_QK ** -0.5 * _LOG2E,
                         q0, nfull, tq, 2 * tq, mask, jnp.zeros((1, tq), _F32), first, ceil)
        lp = lam_ref[...]
        lam = (jnp.exp(jnp.sum(lp[0:1] * lp[1:2], axis=1, keepdims=True))
               - jnp.exp(jnp.sum(lp[2:3] * lp[3:4], axis=1, keepdims=True)) + lam_init)
        linv = 1.0 / l
        outs = []
        for h in range(DA_HEADS):
            a = acc[h] * linv[:, h * 2 * tq:(h + 1) * 2 * tq]
            o = a[:, :tq] - lam * a[:, tq:]
            ms = jnp.mean(o * o, axis=0, keepdims=True)
            outs.append(o * lax.rsqrt(ms + LN_EPS) * nw_ref[...] * (1.0 - lam_init))
        o_ref[0] = jnp.concatenate(outs, axis=0).T.astype(_BF16)

    @pl.when(ceiling_ok)
    def _():
        attend(ceiling)

    @pl.when(jnp.logical_not(ceiling_ok))
    def _():
        attend(None)


def _diff_attention(lam_p, norm_w, n16, t16, bn, s, slopes, lam_init):
    tq = TQ
    ns, per = s // CH, CH // tq
    g = 2 * DA_HEADS
    kern = functools.partial(_attn_a_kernel, tq=tq, slopes=slopes, lam_init=lam_init)
    return pl.pallas_call(
        kern,
        grid=(bn, s // tq),
        in_specs=[
            _const_spec(lam_p.shape), _const_spec(norm_w.shape),
            pl.BlockSpec((1, 1, 256, tq), lambda b, i: (b, i // per, 0, i % per)),
            pl.BlockSpec((1, s, 256), lambda b, i: (b, 0, 0)),
            pl.BlockSpec((1, ns, 256, CH), lambda b, i: (b, 0, 1, 0)),
        ],
        out_specs=pl.BlockSpec((1, tq, 256), lambda b, i: (b, i, 0)),
        out_shape=jax.ShapeDtypeStruct((bn, s, 256), _BF16),
        scratch_shapes=[pltpu.VMEM((256, g * tq), _BF16), pltpu.VMEM((DA_HEADS, CH, 128), _F32),
                        pltpu.VMEM((16, 128), _F32)],
        compiler_params=pltpu.CompilerParams(
            dimension_semantics=("parallel", "arbitrary"), vmem_limit_bytes=V7X_VMEM_LIMIT),
        name="diff_attn",
    )(lam_p, norm_w, t16, n16, t16)


def _bit_planes(rows):
    a = list(rows)
    j, m = 16, 0x0000FFFF
    while j:
        k = 0
        while k < 32:
            t = (a[k] ^ lax.shift_right_logical(a[k + j], jnp.int32(j))) & jnp.int32(m)
            a[k] = a[k] ^ t
            a[k + j] = a[k + j] ^ (t << j)
            k = (k + j + 1) & ~j
        j >>= 1
        if j:
            m = (m ^ (m << j)) & 0xFFFFFFFF
            m = m - (1 << 32) if m >= (1 << 31) else m
    return a[::-1]


def _dsa_kernel(iqt_ref, ik_ref, iwt_ref, qt_ref, k_ref, vt_ref, o_ref,
                iqbd_ref, bd_ref, key_ref, planes_ref, alive_ref, ab_ref, kn_ref, *, tq, slopes, topk):
    g = DSA_HEADS
    q0 = pl.program_id(1) * tq
    nfull = q0 // CH

    @pl.when(pl.program_id(1) == 0)
    def _():
        _key_norm_bound(k_ref, kn_ref, g, k_ref.shape[1] // CH)

    ngrp = (nfull + COUNT_UNROLL) // COUNT_UNROLL
    iqbd_ref[...] = _block_diag_tile(iqt_ref[0, 0], IDX_HEADS)
    bd_ref[...] = _block_diag_tile(qt_ref[0, 0], g)
    ab_ref[...] = _alibi_rows(slopes)
    w = iwt_ref[0, 0][0:IDX_HEADS, :] * (IDX_HEADS ** -0.5 * IDX_HD ** -0.5)

    def logits(j):
        return _dot(ik_ref[0, pl.ds(pl.multiple_of(j * CH, CH), CH), :], iqbd_ref[...])

    def score(lg, j, diag):
        half = CH // 2
        rows = []
        for r0 in (0, half):
            sc = jnp.maximum(lg[r0:r0 + half, 0:tq], 0.0) * w[0:1]
            for h in range(1, IDX_HEADS):
                sc = sc + jnp.maximum(lg[r0:r0 + half, h * tq:(h + 1) * tq], 0.0) * w[h:h + 1]
            sc = jnp.where(sc == 0.0, 0.0, sc)
            bits = pltpu.bitcast(sc, jnp.int32)
            key = bits ^ ((bits >> 31) & 0x7FFFFFFF)
            if diag:
                rel = _iota(key.shape, 0) - _iota(key.shape, 1)
                key = jnp.where(rel <= q0 - j * CH - r0, key, _INT_MIN)
            key_ref[j, r0:r0 + half, :] = key
            key3 = key.reshape(half // 8, 8, tq)
            rows += [key3[v] for v in range(half // 8)]
        planes = _bit_planes(rows)
        planes[31] = ~planes[31]
        planes_ref[j] = jnp.stack(planes, axis=0)
        if diag:
            lim = q0 - j * CH + _iota((8, tq), 1) - _iota((8, tq), 0)
            nbits = jnp.clip((lim >> 3) + 1, 0, 32)
            alive_ref[j] = jnp.where(nbits == 0, 0, jnp.left_shift(jnp.int32(-1), 32 - jnp.maximum(nbits, 1)))
        else:
            alive_ref[j] = jnp.full((8, tq), -1, jnp.int32)

    def score_group(i, c):
        lgs = [logits(SCORE_UNROLL * i + u) for u in range(SCORE_UNROLL)]
        for u in range(SCORE_UNROLL):
            score(lgs[u], SCORE_UNROLL * i + u, False)
        return c

    def score_one(j, c):
        score(logits(j), j, False)
        return c

    ngroup = nfull // SCORE_UNROLL
    lax.fori_loop(0, ngroup, score_group, 0)
    lax.fori_loop(SCORE_UNROLL * ngroup, nfull, score_one, 0)
    score(logits(nfull), nfull, True)

    for u in range(1, COUNT_UNROLL):
        @pl.when(nfull + u < ngrp * COUNT_UNROLL)
        def _():
            planes_ref[nfull + u] = jnp.zeros((32, 8, tq), jnp.int32)
            alive_ref[nfull + u] = jnp.zeros((8, tq), jnp.int32)

    def sweep(b_upd, keep, b_cnt):
        def body(gi, acc8):
            for u in range(COUNT_UNROLL):
                j = gi * COUNT_UNROLL + u
                a = alive_ref[j]
                if b_upd is not None:
                    a = a & ~(planes_ref[j, b_upd] ^ keep)
                    alive_ref[j] = a
                acc8 = acc8 + lax.population_count(a if b_cnt is None else a & planes_ref[j, b_cnt])
            return acc8
        acc8 = lax.fori_loop(0, ngrp, body, jnp.zeros((8, tq), jnp.int32))
        return jnp.sum(acc8, axis=0, keepdims=True)

    def decide(b, ones, want, thr):
        take = ones >= want
        thr = jnp.where(take, thr | jnp.left_shift(jnp.int32(1), b), thr)
        return jnp.where(take, want, want - ones), thr, jnp.where(take, -1, 0)

    want, thr, keep = decide(31, sweep(None, None, 31), jnp.full((1, tq), topk, jnp.int32),
                             jnp.zeros((1, tq), jnp.int32))

    def bit_body(i, c):
        want, thr, keep = c
        b = 30 - i
        return decide(b, sweep(b + 1, keep, b), want, thr)

    want, thr, keep = lax.fori_loop(0, 31, bit_body, (want, thr, keep))
    ties = sweep(0, keep, None)
    thr = thr ^ _INT_MIN
    tie_lane = (ties > want) & (thr > _INT_MIN)

    @pl.when(jnp.max(jnp.where(tie_lane, 1, 0)) > 0)
    def _():
        room = want.astype(_F32)
        lower = jnp.where(_iota((CH, CH), 0) > _iota((CH, CH), 1), 1.0, 0.0).astype(_BF16)

        def demote(j, seen):
            key = key_ref[j]
            eq = (key == thr) & tie_lane
            eqf = jnp.where(eq, 1.0, 0.0)
            rank = _dot(lower, eqf.astype(_BF16)) + seen
            key_ref[j] = jnp.where(eq & (rank >= room), thr - 1, key)
            return seen + jnp.sum(eqf, axis=0, keepdims=True)

        lax.fori_loop(0, nfull + 1, demote, jnp.zeros((1, tq), _F32))

    thr_eff = jnp.maximum(thr, _INT_MIN + 1)

    def near_body(j, best):
        pos = jnp.where(key_ref[j] >= thr_eff, _iota((CH, tq), 0) + j * CH, -1)
        return jnp.maximum(best, jnp.max(pos, axis=0, keepdims=True))

    best = lax.fori_loop(jnp.maximum(nfull - (NEAR_CHUNKS - 1), 0), nfull + 1, near_body,
                         jnp.full((1, tq), -1, jnp.int32))
    gap = jnp.where(best >= 0, q0 + _iota((1, tq), 1) - best, 2 ** 24)
    bound = _logit_bound(qt_ref[0, 0], kn_ref, g, DSA_HD ** -0.5)
    first = _first_chunks(bound, slopes, q0, nfull, jnp.max(gap, axis=1, keepdims=True))
    ceiling, ceiling_ok = _logit_ceiling(bound, slopes, tq, gap.astype(_F32))
    ceiling_ok = ceiling_ok & (jnp.min(best) >= 0)

    def mask(j, aux, diag):
        return jnp.where(key_ref[j] >= thr_eff, 0.0, NEG_INF), aux

    def attend(ceil):
        l, acc = _attend(k_ref, vt_ref, bd_ref, ab_ref, _slope_row(slopes, 1, tq), DSA_HD ** -0.5 * _LOG2E,
                         q0, nfull, tq, tq, mask, jnp.zeros((1, tq), _F32), first, ceil)
        linv = 1.0 / l
        outs = [acc[h] * linv[:, h * tq:(h + 1) * tq] for h in range(g)]
        o_ref[0] = jnp.concatenate(outs, axis=0).T.astype(_BF16)

    @pl.when(ceiling_ok)
    def _():
        attend(ceiling)

    @pl.when(jnp.logical_not(ceiling_ok))
    def _():
        attend(None)


def _dsa_attention(n16, t16, t32, bn, s, slopes):
    tq = TQ
    ns, per = s // CH, CH // tq
    g = DSA_HEADS
    topk = min(TOPK_MAX, s // 4)
    assert ns % COUNT_UNROLL == 0
    kern = functools.partial(_dsa_kernel, tq=tq, slopes=slopes, topk=topk)
    return pl.pallas_call(
        kern,
        grid=(bn, s // tq),
        in_specs=[
            pl.BlockSpec((1, 1, 128, tq), lambda b, i: (b, i // per, 10, i % per)),
            pl.BlockSpec((1, s, 128), lambda b, i: (b, 0, 10)),
            pl.BlockSpec((1, 1, 16, tq), lambda b, i: (b, i // per, 16, i % per)),
            pl.BlockSpec((1, 1, 256, tq), lambda b, i: (b, i // per, 2, i % per)),
            pl.BlockSpec((1, s, 256), lambda b, i: (b, 0, 1)),
            pl.BlockSpec((1, ns, 256, CH), lambda b, i: (b, 0, 3, 0)),
        ],
        out_specs=pl.BlockSpec((1, tq, 256), lambda b, i: (b, i, 0)),
        out_shape=jax.ShapeDtypeStruct((bn, s, 256), _BF16),
        scratch_shapes=[
            pltpu.VMEM((128, IDX_HEADS * tq), _BF16), pltpu.VMEM((256, g * tq), _BF16),
            pltpu.VMEM((ns, CH, tq), jnp.int32), pltpu.VMEM((ns, 32, 8, tq), jnp.int32),
            pltpu.VMEM((ns, 8, tq), jnp.int32), pltpu.VMEM((DSA_HEADS, CH, 128), _F32),
            pltpu.VMEM((16, 128), _F32),
        ],
        compiler_params=pltpu.CompilerParams(
            dimension_semantics=("parallel", "arbitrary"), vmem_limit_bytes=V7X_VMEM_LIMIT),
        name="dsa_attn",
    )(t16, n16, t32, t16, n16, t16)


def _ret_kernel(q_ref, kt_ref, v_ref, g_ref, intra_ref, qdec_ref, kdect_ref, cd_ref, nw_ref, o_ref, s_ref, *, c):
    @pl.when(pl.program_id(1) == 0)
    def _():
        s_ref[...] = jnp.zeros(s_ref.shape, _F32)

    q = q_ref[0]
    v = v_ref[0]
    kt = kt_ref[0, 0] * (RET_QK ** -0.5)
    att = _dot(q, _block_diag_tile(kt.astype(_BF16), RET_HEADS)) * intra_ref[...]
    vt = jnp.concatenate([v] * RET_HEADS, axis=0)
    vbd = jnp.where((_iota(vt.shape, 0) // c) == (_iota(vt.shape, 1) // RET_V), vt, jnp.zeros_like(vt))
    st = s_ref[...]
    o = _dot(att.astype(_BF16), vbd) + _dot(q, st.astype(_BF16)) * qdec_ref[...]
    upd = _dot((kt * kdect_ref[...]).astype(_BF16), v)
    same_head = (_iota(upd.shape, 0) // RET_QK) == (_iota(upd.shape, 1) // RET_V)
    s_ref[...] = st * cd_ref[...] + jnp.where(same_head, upd, 0.0)

    mu = _group_mean(o, RET_V)
    d = o - mu
    var = _group_mean(d * d, RET_V)
    y = d * lax.rsqrt(var + LN_EPS) * nw_ref[...]
    gate = g_ref[0]
    o_ref[0] = (gate * jax.nn.sigmoid(gate) * y).astype(_BF16)


def _retention_consts(c):
    h = RET_HEADS
    log_g = np.log1p(-np.power(2.0, -5.0 - np.arange(h, dtype=np.float64)))
    pos = np.arange(c, dtype=np.float64)
    rel = pos[:, None] - pos[None, :]
    intra = np.where(rel >= 0, np.exp(log_g[:, None, None] * np.maximum(rel, 0.0)), 0.0)
    intra = np.transpose(intra, (1, 0, 2)).reshape(c, h * c)
    qdec = np.repeat(np.exp(log_g[:, None] * (pos[None, :] + 1.0)).T, RET_V, axis=1)
    kdect = np.repeat(np.exp(log_g[:, None] * (c - 1.0 - pos[None, :])), RET_QK, axis=0)
    cd = np.repeat(np.exp(log_g * c), RET_QK)[:, None] * np.ones((1, h * RET_V))
    return tuple(jnp.asarray(a, _F32) for a in (intra, qdec, kdect, cd))


def _retention(norm_w, n16, n32, t32, bn, s):
    c = RET_C
    per = CH // c
    intra, qdec, kdect, cd = _retention_consts(c)
    return pl.pallas_call(
        functools.partial(_ret_kernel, c=c),
        grid=(bn, s // c),
        in_specs=[
            pl.BlockSpec((1, c, 256), lambda b, i: (b, i, 2)),
            pl.BlockSpec((1, 1, 256, c), lambda b, i: (b, i // per, 0, i % per)),
            pl.BlockSpec((1, c, 256), lambda b, i: (b, i, 3)),
            pl.BlockSpec((1, c, 256), lambda b, i: (b, i, 0)),
            _const_spec(intra.shape), _const_spec(qdec.shape), _const_spec(kdect.shape),
            _const_spec(cd.shape), _const_spec(norm_w.shape),
        ],
        out_specs=pl.BlockSpec((1, c, 256), lambda b, i: (b, i, 0)),
        out_shape=jax.ShapeDtypeStruct((bn, s, 256), _BF16),
        scratch_shapes=[pltpu.VMEM((RET_HEADS * RET_QK, RET_HEADS * RET_V), _F32)],
        compiler_params=pltpu.CompilerParams(dimension_semantics=("parallel", "arbitrary")),
        name="retention",
    )(n16, t32, n16, n32, intra, qdec, kdect, cd, norm_w)


def _gla_kernel(q_ref, k_ref, a_ref, v_ref, vt_ref, g_ref, wa_ref, ba_ref, nw_ref, o_ref,
                st_ref, u_ref, oacc_ref, qs_ref, kk_ref, b_ref, qh_ref, dec_ref, *, ts, sub):
    nsub = ts // sub
    assert sub == 16

    @pl.when(pl.program_id(1) == 0)
    def _():
        st_ref[...] = jnp.zeros(st_ref.shape, _F32)

    la = jax.nn.log_sigmoid(_dot(a_ref[0].astype(_BF16), wa_ref[...]) + ba_ref[...]) * (1.0 / GLA_GATE_TEMP)
    in_blk = _iota(la.shape, 0) % sub
    b = la
    for sh in (1, 2, 4, 8):
        b = b + jnp.where(in_blk >= sh, pltpu.roll(b, sh, 0), 0.0)
    bl = jnp.where(in_blk == sub - 1, b, 0.0)
    for sh in (1, 2, 4, 8):
        bl = bl + jnp.where(in_blk < sub - sh, pltpu.roll(bl, ts - sh, 0), 0.0)
    qs = q_ref[0] * (GLA_QK ** -0.5)
    kk = k_ref[0]
    kd = (kk * jnp.exp(bl - b)).astype(_BF16)
    qs_ref[...] = qs
    kk_ref[...] = kk
    b_ref[...] = b
    qh_ref[...] = (qs * jnp.exp(b)).astype(_BF16)
    dec_ref[...] = jnp.exp(bl)
    vt = vt_ref[0, 0]
    row_blk = _iota(kd.shape, 0) // sub
    kd_wide = jnp.concatenate([jnp.where(row_blk == n, kd, jnp.zeros_like(kd)) for n in range(nsub)], axis=1)
    u_all = _dot(vt, kd_wide)
    for n in range(nsub):
        u_ref[n] = u_all[:, n * 128:(n + 1) * 128]

    st_keep = (_iota(st_ref.shape, 0) // GLA_V) == (_iota(st_ref.shape, 1) // GLA_QK)
    spread = jnp.where((_iota((128, 256), 0) // GLA_QK) == (_iota((128, 256), 1) // GLA_V), 1.0, 0.0).astype(_BF16)
    row16 = _iota((sub, 128), 0)

    def body(n, carry):
        r0 = pl.multiple_of(n * sub, sub)
        st = st_ref[...]
        o_cross = _dot_nt(qh_ref[pl.ds(r0, sub), :], st.astype(_BF16))
        q16 = qs_ref[pl.ds(r0, sub), :]
        k16 = kk_ref[pl.ds(r0, sub), :]
        b16 = b_ref[pl.ds(r0, sub), :]
        v16 = v_ref[0, pl.ds(r0, sub), :].astype(_F32)
        es = []
        for j in range(sub):
            e = q16 * k16[j:j + 1] * jnp.exp(jnp.minimum(b16 - b16[j:j + 1], 0.0))
            es.append(jnp.where(row16 >= j, e, 0.0))
        e_all = jnp.concatenate(es, axis=0)
        e_hi = e_all.astype(_BF16)
        e_lo = (e_all - e_hi.astype(_F32)).astype(_BF16)
        att = _dot(e_hi, spread) + _dot(e_lo, spread)
        o_diag = att[0:sub] * v16[0:1]
        for j in range(1, sub):
            o_diag = o_diag + att[j * sub:(j + 1) * sub] * v16[j:j + 1]
        oacc_ref[pl.ds(r0, sub), :] = o_cross + o_diag
        st_ref[...] = st * dec_ref[pl.ds(r0, 1), :] + jnp.where(st_keep, u_ref[n], 0.0)
        return carry

    lax.fori_loop(0, nsub, body, 0, unroll=2)

    o = oacc_ref[...]
    ms = _group_mean(o * o, GLA_V)
    y = o * lax.rsqrt(ms + LN_EPS) * nw_ref[...]
    gate = g_ref[0]
    o_ref[0] = (gate * jax.nn.sigmoid(gate) * y).astype(_BF16)


def _gla(wa, ba, norm_w, n16, n32, t16, bn, s):
    ts, sub = CH, GLA_SUB
    return pl.pallas_call(
        functools.partial(_gla_kernel, ts=ts, sub=sub),
        grid=(bn, s // ts),
        in_specs=[
            pl.BlockSpec((1, ts, 128), lambda b, i: (b, i, 4)),
            pl.BlockSpec((1, ts, 128), lambda b, i: (b, i, 5)),
            pl.BlockSpec((1, ts, 128), lambda b, i: (b, i, 6)),
            pl.BlockSpec((1, ts, 256), lambda b, i: (b, i, 4)),
            pl.BlockSpec((1, 1, 256, ts), lambda b, i: (b, i, 4, 0)),
            pl.BlockSpec((1, ts, 256), lambda b, i: (b, i, 1)),
            _const_spec(wa.shape), _const_spec(ba.shape), _const_spec(norm_w.shape),
        ],
        out_specs=pl.BlockSpec((1, ts, 256), lambda b, i: (b, i, 0)),
        out_shape=jax.ShapeDtypeStruct((bn, s, 256), _BF16),
        scratch_shapes=[
            pltpu.VMEM((GLA_HEADS * GLA_V, GLA_HEADS * GLA_QK), _F32),
            pltpu.VMEM((ts // sub, GLA_HEADS * GLA_V, GLA_HEADS * GLA_QK), _F32),
            pltpu.VMEM((ts, 256), _F32),
            pltpu.VMEM((ts, 128), _F32), pltpu.VMEM((ts, 128), _F32), pltpu.VMEM((ts, 128), _F32),
            pltpu.VMEM((ts, 128), _BF16), pltpu.VMEM((ts, 128), _F32),
        ],
        compiler_params=pltpu.CompilerParams(dimension_semantics=("parallel", "arbitrary")),
        name="gla",
    )(n32, n32, n32, n16, t16, n32, wa, ba, norm_w)


def _merge_kernel(x_ref, ya_ref, yb_ref, yc_ref, yd_ref, wg_ref, wbr_ref, wout_ref, lnw_ref, lnb_ref, h_ref):
    x = x_ref[...]
    xb = x.astype(_BF16)
    merged = None
    for n, y_ref in enumerate((ya_ref, yb_ref, yc_ref, yd_ref)):
        gate = jax.nn.sigmoid(_dot(xb, wg_ref[:, n * D_MODEL:(n + 1) * D_MODEL]))
        term = gate * _dot(y_ref[...], wbr_ref[n])
        merged = term if merged is None else merged + term
    mix = _dot(merged.astype(_BF16), wout_ref[...])
    h_ref[...] = _layer_norm(DEEPNORM_ALPHA * x + mix, lnw_ref[...], lnb_ref[...])


def _merge(x2, ys, wg, wbr, wout, lnw, lnb):
    t = x2.shape[0]
    ts = TS_DENSE
    tok = lambda w: pl.BlockSpec((ts, w), lambda i: (i, 0))
    return pl.pallas_call(
        _merge_kernel,
        grid=(t // ts,),
        in_specs=[tok(D_MODEL)] + [tok(BRANCH_W)] * 4 + [
            _const_spec(wg.shape), _const_spec(wbr.shape), _const_spec(wout.shape),
            _const_spec(lnw.shape), _const_spec(lnb.shape)],
        out_specs=tok(D_MODEL),
        out_shape=jax.ShapeDtypeStruct((t, D_MODEL), _F32),
        compiler_params=pltpu.CompilerParams(
            dimension_semantics=("parallel",), vmem_limit_bytes=V7X_VMEM_LIMIT),
        name="merge_ln",
    )(x2, *ys, wg, wbr, wout, lnw, lnb)


def _ffn_kernel(h_ref, p_ref, wup_ref, cw_ref, cb_ref, wdn_ref, wpg_ref, wpp_ref, lnw_ref, lnb_ref,
                o_ref, tail_ref, *, ts):
    @pl.when(pl.program_id(1) == 0)
    def _():
        tail_ref[...] = jnp.zeros(tail_ref.shape, _F32)

    h = h_ref[...]
    hb = h.astype(_BF16)
    row = _iota((ts, FF_COLS), 0)
    f = None
    for c0 in range(0, D_FF, FF_COLS):
        cols = slice(c0, c0 + FF_COLS)
        u = _dot(hb, wup_ref[:, cols])
        gt = _dot(hb, wup_ref[:, D_FF + c0:D_FF + c0 + FF_COLS])
        prev = tail_ref[:, cols]
        g1 = jnp.where(row == 0, prev[7:8], pltpu.roll(gt, 1, 0))
        g2 = jnp.where(row == 0, prev[6:7], jnp.where(row == 1, prev[7:8], pltpu.roll(gt, 2, 0)))
        tail_ref[:, cols] = gt[ts - 8:ts]
        gc = cb_ref[:, cols] + cw_ref[0:1, cols] * g2
        gc = gc + cw_ref[1:2, cols] * g1
        gc = gc + cw_ref[2:3, cols] * gt
        term = _dot((jax.nn.gelu(gc) * u).astype(_BF16), wdn_ref[cols, :])
        f = term if f is None else f + term
    e = jax.nn.sigmoid(_dot(hb, wpg_ref[...])) * _dot(p_ref[...].astype(_BF16), wpp_ref[...])
    o_ref[...] = _layer_norm(DEEPNORM_ALPHA * h + f + e, lnw_ref[...], lnb_ref[...])


def _ffn(h2, p2, wup, cw, cb, wdn, wpg, wpp, lnw, lnb, bn, s):
    ts = TS_DENSE
    ns = s // ts
    tok = lambda w: pl.BlockSpec((ts, w), lambda b, i: (b * ns + i, 0))
    return pl.pallas_call(
        functools.partial(_ffn_kernel, ts=ts),
        grid=(bn, ns),
        in_specs=[tok(D_MODEL), tok(P_DIM)] + [_const_spec(a.shape) for a in (wup, cw, cb, wdn, wpg, wpp, lnw, lnb)],
        out_specs=tok(D_MODEL),
        out_shape=jax.ShapeDtypeStruct((bn * s, D_MODEL), _F32),
        scratch_shapes=[pltpu.VMEM((8, D_FF), _F32)],
        compiler_params=pltpu.CompilerParams(
            dimension_semantics=("parallel", "arbitrary"), vmem_limit_bytes=V7X_VMEM_LIMIT),
        name="ffn_ple_ln",
    )(h2, p2, wup, cw, cb, wdn, wpg, wpp, lnw, lnb)


def kernel(x, p, w_in, a_lambda, a_norm_w, ret_norm_w, gla_w_a2, gla_b_a, gla_norm_w, w_branch, w_out,
           ln1_w, ln1_b, w_ffn_up, ffn_conv_w, ffn_conv_b, w_ffn_down, w_ple_gate, w_ple_proj, ln2_w, ln2_b):
    bn, s, _ = x.shape
    t = bn * s
    slopes = [2.0 ** (-(8.0 / N_SOFTMAX_HEADS) * i) for i in range(1, N_SOFTMAX_HEADS + 1)]
    slopes_a, slopes_b = tuple(slopes[0::2]), tuple(slopes[1::2])
    row = lambda v: v.astype(_F32).reshape(1, -1)
    x2 = x.reshape(t, D_MODEL)
    for i in range(DEPTH):
        *proj_w, wg = _prep_weights(w_in, i)
        n32, n16, t16, t32 = _project(x2, *proj_w, bn, s)
        n32 = n32.reshape(bn, s, -1)
        n16 = n16.reshape(bn, s, -1)
        lam_init = 0.8 - 0.6 * math.exp(-0.3 * i)
        y_a = _diff_attention(a_lambda[i].astype(_F32), a_norm_w[i].astype(_F32).reshape(DA_V, 1),
                              n16, t16, bn, s, slopes_a, lam_init)
        y_b = _dsa_attention(n16, t16, t32, bn, s, slopes_b)
        y_c = _retention(row(ret_norm_w[i]), n16, n32, t32, bn, s)
        wa = jnp.pad(gla_w_a2[i], ((0, 128 - GLA_RANK), (0, 0))).astype(_BF16)
        y_d = _gla(wa, row(gla_b_a[i]), row(jnp.tile(gla_norm_w[i], GLA_HEADS)), n16, n32, t16, bn, s)
        ys = [y.reshape(t, BRANCH_W) for y in (y_a, y_b, y_c, y_d)]
        h2 = _merge(x2, ys, wg, w_branch[i].astype(_BF16), w_out[i].astype(_BF16), row(ln1_w[i]), row(ln1_b[i]))
        x2 = _ffn(h2, p[i].reshape(t, P_DIM), w_ffn_up[i].astype(_BF16), ffn_conv_w[i].astype(_F32),
                  row(ffn_conv_b[i]), w_ffn_down[i].astype(_BF16), w_ple_gate[i].astype(_BF16),
                  w_ple_proj[i].astype(_BF16), row(ln2_w[i]), row(ln2_b[i]), bn, s)
    return x2.reshape(bn, s, D_MODEL)
```

```python
import functools
import math

import numpy as np
import jax
import jax.numpy as jnp
from jax import lax
from jax.experimental import pallas as pl
from jax.experimental.pallas import tpu as pltpu

D_MODEL = 1024
DEPTH = 2
P_DIM = 256
N_BRANCH = 4
BRANCH_W = 256
DA_HEADS = 4
DA_QK = 32
DA_V = 64
DSA_HEADS = 4
DSA_HD = 64
IDX_HEADS = 4
IDX_HD = 32
TOPK_MAX = 256
RET_HEADS = 4
RET_QK = 64
RET_V = 64
GLA_HEADS = 4
GLA_QK = 32
GLA_V = 64
GLA_RANK = 16
GLA_GATE_TEMP = 16.0
D_FF = 2816
CONV_W = 3
N_SOFTMAX_HEADS = DA_HEADS + DSA_HEADS
LN_EPS = 1e-5
NEG_INF = -1e30
DEEPNORM_ALPHA = (2.0 * DEPTH) ** 0.25

IN_SIZES = (256, 256, 256, 256, 256, 256, 128, 32, 4, 256, 256, 256, 256, 128, 128, 256, 16, 256, 4096)
IN_NAMES = ("a_q", "a_k", "a_v", "b_q", "b_k", "b_v", "b_iq", "b_ik", "b_iw",
            "c_q", "c_k", "c_v", "c_g", "d_q", "d_k", "d_v", "d_a", "d_g", "m_g")

_BF16 = jnp.bfloat16
_F32 = jnp.float32
_INT_MIN = -2 ** 31
_LOG2E = math.log2(math.e)

CH = 256
TQ = 256
RET_C = 128
GLA_SUB = 16
COUNT_UNROLL = 4
SCORE_UNROLL = 4
ATTEND_UNROLL = 4
TS_DENSE = 512
FF_COLS = 256
PREP_ROWS = 256
NEAR_CHUNKS = 4
SKIP_NATS = 32.0
FIXED_MAX_BITS = 100.0
V7X_VMEM_LIMIT = 56 * 1024 * 1024


def _dot(a, b, precision=None):
    return jnp.dot(a, b, preferred_element_type=_F32, precision=precision)


def _dot_nt(a, b):
    return lax.dot_general(a, b, (((1,), (1,)), ((), ())), preferred_element_type=_F32)


def _iota(shape, dim):
    return lax.broadcasted_iota(jnp.int32, shape, dim)


def _block_diag_tile(m_t, ngroups):
    r, tq = m_t.shape
    tiled = jnp.concatenate([m_t] * ngroups, axis=1)
    keep = (_iota(tiled.shape, 0) // (r // ngroups)) == (_iota(tiled.shape, 1) // tq)
    return jnp.where(keep, tiled, jnp.zeros_like(tiled))


def _layer_norm(x, w, b):
    mu = jnp.mean(x, -1, keepdims=True)
    var = jnp.mean(jnp.square(x - mu), -1, keepdims=True)
    return (x - mu) * lax.rsqrt(var + LN_EPS) * w + b


def _group_mean(x, group):
    lane_g = _iota(x.shape, 1) // group
    out = jnp.zeros_like(x)
    for h in range(x.shape[1] // group):
        mk = lane_g == h
        mh = jnp.sum(jnp.where(mk, x, 0.0), axis=1, keepdims=True) * (1.0 / group)
        out = jnp.where(mk, mh, out)
    return out


def _const_spec(shape):
    nd = len(shape)
    return pl.BlockSpec(shape, lambda *_: (0,) * nd, pipeline_mode=pl.Buffered(1))


def _prep_kernel(w_ref, wn32_ref, wn16_ref, wt16_ref, wt32_ref, wg_ref):
    offs = [0] + np.cumsum(IN_SIZES).tolist()
    col = {n: w_ref[0, :, offs[i]:offs[i + 1]] for i, n in enumerate(IN_NAMES)}
    zeros = lambda n: jnp.zeros((w_ref.shape[1], n), _F32)
    cat = lambda xs: jnp.concatenate(xs, axis=1)
    wn32_ref[...] = cat([col["c_g"], col["d_g"], col["d_q"], col["d_k"], col["d_a"],
                         zeros(128 - GLA_RANK)]).astype(_BF16)
    wn16_ref[...] = cat([col["a_k"], col["b_k"], col["c_q"], col["c_v"], col["d_v"]]
                        + [col["b_ik"]] * IDX_HEADS).astype(_BF16)
    wt16_ref[...] = cat([col["a_q"], col["a_v"], col["b_q"], col["b_v"], col["d_v"], col["b_iq"]]).T.astype(_BF16)
    wt32_ref[...] = cat([col["c_k"], col["b_iw"], zeros(16 - IDX_HEADS)]).T.astype(_BF16)
    wg_ref[...] = col["m_g"].astype(_BF16)


def _prep_weights(w_in, layer):
    rb = PREP_ROWS
    n32, n16, t16, t32, ng = 896, 1408, 1408, 272, N_BRANCH * D_MODEL
    rows = lambda n: pl.BlockSpec((rb, n), lambda r: (r, 0))
    cols = lambda n: pl.BlockSpec((n, rb), lambda r: (0, r))
    return pl.pallas_call(
        _prep_kernel,
        grid=(D_MODEL // rb,),
        in_specs=[pl.BlockSpec((1, rb, w_in.shape[2]), lambda r: (layer, r, 0))],
        out_specs=[rows(n32), rows(n16), cols(t16), cols(t32), rows(ng)],
        out_shape=[jax.ShapeDtypeStruct((D_MODEL, n32), _BF16), jax.ShapeDtypeStruct((D_MODEL, n16), _BF16),
                   jax.ShapeDtypeStruct((t16, D_MODEL), _BF16), jax.ShapeDtypeStruct((t32, D_MODEL), _BF16),
                   jax.ShapeDtypeStruct((D_MODEL, ng), _BF16)],
        compiler_params=pltpu.CompilerParams(
            dimension_semantics=("parallel",), vmem_limit_bytes=V7X_VMEM_LIMIT),
        name="prep_weights",
    )(w_in)


def _proj_kernel(x_ref, wn32_ref, wn16_ref, wt16_ref, wt32_ref, n32_ref, n16_ref, t16_ref, t32_ref):
    x = x_ref[...].astype(_BF16)
    n32_ref[...] = _dot(x, wn32_ref[...])
    n16_ref[...] = _dot(x, wn16_ref[...]).astype(_BF16)
    t16_ref[0, 0] = _dot_nt(wt16_ref[...], x).astype(_BF16)
    t32_ref[0, 0] = _dot_nt(wt32_ref[...], x)


def _project(x2, wn32, wn16, wt16, wt32, bn, s):
    ns = s // CH
    n32, n16, t16, t32 = wn32.shape[1], wn16.shape[1], wt16.shape[0], wt32.shape[0]
    return pl.pallas_call(
        _proj_kernel,
        grid=(bn, ns),
        in_specs=[
            pl.BlockSpec((CH, D_MODEL), lambda b, i: (b * ns + i, 0)),
            _const_spec(wn32.shape), _const_spec(wn16.shape), _const_spec(wt16.shape), _const_spec(wt32.shape),
        ],
        out_specs=[
            pl.BlockSpec((CH, n32), lambda b, i: (b * ns + i, 0)),
            pl.BlockSpec((CH, n16), lambda b, i: (b * ns + i, 0)),
            pl.BlockSpec((1, 1, t16, CH), lambda b, i: (b, i, 0, 0)),
            pl.BlockSpec((1, 1, t32, CH), lambda b, i: (b, i, 0, 0)),
        ],
        out_shape=[
            jax.ShapeDtypeStruct((bn * s, n32), _F32),
            jax.ShapeDtypeStruct((bn * s, n16), _BF16),
            jax.ShapeDtypeStruct((bn, ns, t16, CH), _BF16),
            jax.ShapeDtypeStruct((bn, ns, t32, CH), _F32),
        ],
        compiler_params=pltpu.CompilerParams(
            dimension_semantics=("parallel", "parallel"), vmem_limit_bytes=V7X_VMEM_LIMIT),
        name="proj",
    )(x2, wn32, wn16, wt16, wt32)


def _slope_row(slopes, reps, tq):
    return jnp.concatenate([jnp.full((1, tq), s * _LOG2E, _F32) for s in slopes for _ in range(reps)], axis=1)


def _alibi_rows(slopes):
    rows = _iota((CH, 128), 0).astype(_F32)
    return jnp.stack([rows * (s * _LOG2E) for s in slopes], axis=0)


def _key_norm_bound(k_ref, kn_ref, ngroups, nchunks):
    gt = jnp.where(_iota((16, 256), 0) == _iota((16, 256), 1) // (256 // ngroups), 1.0, 0.0).astype(_BF16)

    def body(j, best):
        kc = k_ref[0, pl.ds(pl.multiple_of(j * CH, CH), CH), :].astype(_F32)
        return jnp.maximum(best, _dot_nt(gt, (kc * kc).astype(_BF16)))

    best = lax.fori_loop(0, nchunks, body, jnp.zeros((16, CH), _F32))
    kn_ref[...] = jnp.broadcast_to(jnp.sqrt(jnp.max(best, axis=1, keepdims=True) * (1.0 + 2.0 ** -7)), kn_ref.shape)


def _logit_bound(qt, kn_ref, ngroups, scale):
    q = qt.astype(_F32)
    tq = q.shape[1]
    qn = jnp.sqrt(jnp.max(jnp.sum((q * q).reshape(ngroups, 256 // ngroups, tq), axis=1), axis=1, keepdims=True))
    return scale * qn * kn_ref[0:ngroups, 0:1]


def _first_chunks(bound, slopes, q0, nfull, nearest=0):
    per_head = bound.shape[0] // len(slopes)
    inv_slope = jnp.concatenate([jnp.full((per_head, 1), 1.0 / s, _F32) for s in slopes], axis=0)
    reach = ((2.0 * bound + SKIP_NATS) * inv_slope).astype(jnp.int32) + 2 + nearest
    last_far = q0 - CH + 1 - reach
    first = jnp.where(last_far < 0, 0, last_far // CH + 1)
    first = jnp.minimum(first, nfull)
    return [jnp.min(first[h * per_head:(h + 1) * per_head]) for h in range(len(slopes))]


def _logit_ceiling(bound, slopes, tq, nearest=None):
    per_head = bound.shape[0] // len(slopes)
    lane = _iota((1, tq), 1).astype(_F32)
    if nearest is not None:
        lane = lane - nearest
    rows = [bound[g:g + 1, :] * _LOG2E + (slopes[g // per_head] * _LOG2E) * lane for g in range(bound.shape[0])]
    deep = jnp.max(jnp.where(2.0 * _LOG2E * bound > FIXED_MAX_BITS, 1, 0)) > 0
    return jnp.concatenate(rows, axis=1), jnp.logical_not(deep)


def _attend(k_ref, vt_ref, bd_ref, ab_ref, slope_row, c1, q0, nfull, tq, w, mask_fn, aux0, first, ceiling=None):
    g_tq = bd_ref.shape[1]
    nheads = 4

    def pv(j, p, heads):
        vt_c = vt_ref[0, j]
        return {h: _dot(vt_c[h * 64:(h + 1) * 64, :], p[:, h * w:(h + 1) * w]) for h in heads}

    def qk(j, heads):
        kc = k_ref[0, pl.ds(pl.multiple_of(j * CH, CH), CH), :]
        tiles = sorted({c0 // 256 for h in heads for c0 in range(h * w, (h + 1) * w, 128)})
        return {t: _dot(kc, bd_ref[:, t * 256:(t + 1) * 256]) for t in tiles}

    def softmax(s, j, m, l, aux, heads, diag):
        crow = slope_row * (j * CH - q0).astype(_F32)
        amask, aux = mask_fn(j, aux, diag)
        ps, ms, ls, alphas = [], [], [], []
        for c0 in range(0, g_tq, 128):
            cols = slice(c0, c0 + 128)
            if c0 // w not in heads:
                ps.append(jnp.zeros((CH, 128), _BF16))
                ms.append(m[:, cols])
                ls.append(l[:, cols])
                alphas.append(jnp.ones((1, 128), _F32))
                continue
            t = s[c0 // 256][:, c0 % 256:c0 % 256 + 128] * c1 + ab_ref[c0 // w]
            if amask is not None:
                t = t + amask[:, c0 % tq:c0 % tq + 128]
            if ceiling is not None:
                p = jnp.exp2(t + (crow[:, cols] - ceiling[:, cols]))
                m_new, alpha = m[:, cols], jnp.ones((1, 128), _F32)
            else:
                m_new = jnp.maximum(m[:, cols], jnp.max(t, axis=0, keepdims=True) + crow[:, cols])
                alpha = jnp.exp2(m[:, cols] - m_new)
                p = jnp.exp2(t - (m_new - crow[:, cols]))
            ls.append(alpha * l[:, cols] + jnp.sum(p, axis=0, keepdims=True))
            ps.append(p.astype(_BF16))
            ms.append(m_new)
            alphas.append(alpha)
        cat = lambda xs: jnp.concatenate(xs, axis=1)
        return cat(ms), cat(ls), cat(alphas), cat(ps), aux

    def step(s, j, carry, heads, diag=False):
        m, l, acc, aux = carry
        m, l, alpha, p, aux = softmax(s, j, m, l, aux, heads, diag)
        pvs = pv(j, p, heads)
        if ceiling is not None:
            acc = [acc[h] + pvs[h] if h in heads else acc[h] for h in range(nheads)]
        else:
            acc = [alpha[:, h * w:(h + 1) * w] * acc[h] + pvs[h] if h in heads else acc[h] for h in range(nheads)]
        return m, l, acc, aux

    def run(lo, hi, heads, carry):
        unroll = ATTEND_UNROLL

        def group(i, c):
            j = lo + unroll * i
            ss = [qk(j + u, heads) for u in range(unroll)]
            for u in range(unroll):
                c = step(ss[u], j + u, c, heads)
            return c

        ngroup = jnp.maximum(hi - lo, 0) // unroll
        carry = lax.fori_loop(0, ngroup, group, carry)
        return lax.fori_loop(lo + unroll * ngroup, hi, lambda j, c: step(qk(j, heads), j, c, heads), carry)

    carry = (jnp.full((1, g_tq), NEG_INF, _F32), jnp.zeros((1, g_tq), _F32),
             [jnp.zeros((64, w), _F32) for _ in range(nheads)], aux0)
    lows, hi = [], nfull
    for h in range(nheads):
        u = ATTEND_UNROLL
        lo = jnp.maximum(hi - u * ((hi - jnp.minimum(first[h], hi) + u - 1) // u), 0)
        lows.append(lo)
        hi = lo
    for h in reversed(range(nheads)):
        carry = run(lows[h], lows[h - 1] if h else nfull, tuple(range(h, nheads)), carry)
    heads = tuple(range(nheads))
    m, l, acc, aux = step(qk(nfull, heads), nfull, carry, heads, True)
    return l, acc


def _attn_a_kernel(lam_ref, nw_ref, qt_ref, k_ref, vt_ref, o_ref, bd_ref, ab_ref, kn_ref, *, tq, slopes, lam_init):
    g = 2 * DA_HEADS
    q0 = pl.program_id(1) * tq
    nfull = q0 // CH

    @pl.when(pl.program_id(1) == 0)
    def _():
        _key_norm_bound(k_ref, kn_ref, g, k_ref.shape[1] // CH)

    bd_ref[...] = _block_diag_tile(qt_ref[0, 0], g)
    ab_ref[...] = _alibi_rows(slopes)

    def mask(j, aux, diag):
        if not diag:
            return None, aux
        rel = _iota((CH, tq), 0) - _iota((CH, tq), 1)
        return jnp.where(rel <= q0 - j * CH, 0.0, NEG_INF), aux

    bound = _logit_bound(qt_ref[0, 0], kn_ref, g, DA_QK ** -0.5)
    first = _first_chunks(bound, slopes, q0, nfull)
    ceiling, ceiling_ok = _logit_ceiling(bound, slopes, tq)

    def attend(ceil):
        l, acc = _attend(k_ref, vt_ref, bd_ref, ab_ref, _slope_row(slopes, 2, tq), DA_QK ** -0.5 * _LOG2E,
                         q0, nfull, tq, 2 * tq, mask, jnp.zeros((1, tq), _F32), first, ceil)
        lp = lam_ref[...]
        lam = (jnp.exp(jnp.sum(lp[0:1] * lp[1:2], axis=1, keepdims=True))
               - jnp.exp(jnp.sum(lp[2:3] * lp[3:4], axis=1, keepdims=True)) + lam_init)
        linv = 1.0 / l
        outs = []
        for h in range(DA_HEADS):
            a = acc[h] * linv[:, h * 2 * tq:(h + 1) * 2 * tq]
            o = a[:, :tq] - lam * a[:, tq:]
            ms = jnp.mean(o * o, axis=0, keepdims=True)
            outs.append(o * lax.rsqrt(ms + LN_EPS) * nw_ref[...] * (1.0 - lam_init))
        o_ref[0] = jnp.concatenate(outs, axis=0).T.astype(_BF16)

    @pl.when(ceiling_ok)
    def _():
        attend(ceiling)

    @pl.when(jnp.logical_not(ceiling_ok))
    def _():
        attend(None)


def _diff_attention(lam_p, norm_w, n16, t16, bn, s, slopes, lam_init):
    tq = TQ
    ns, per = s // CH, CH // tq
    g = 2 * DA_HEADS
    kern = functools.partial(_attn_a_kernel, tq=tq, slopes=slopes, lam_init=lam_init)
    return pl.pallas_call(
        kern,
        grid=(bn, s // tq),
        in_specs=[
            _const_spec(lam_p.shape), _const_spec(norm_w.shape),
            pl.BlockSpec((1, 1, 256, tq), lambda b, i: (b, i // per, 0, i % per)),
            pl.BlockSpec((1, s, 256), lambda b, i: (b, 0, 0)),
            pl.BlockSpec((1, ns, 256, CH), lambda b, i: (b, 0, 1, 0)),
        ],
        out_specs=pl.BlockSpec((1, tq, 256), lambda b, i: (b, i, 0)),
        out_shape=jax.ShapeDtypeStruct((bn, s, 256), _BF16),
        scratch_shapes=[pltpu.VMEM((256, g * tq), _BF16), pltpu.VMEM((DA_HEADS, CH, 128), _F32),
                        pltpu.VMEM((16, 128), _F32)],
        compiler_params=pltpu.CompilerParams(
            dimension_semantics=("parallel", "arbitrary"), vmem_limit_bytes=V7X_VMEM_LIMIT),
        name="diff_attn",
    )(lam_p, norm_w, t16, n16, t16)


def _bit_planes(rows):
    a = list(rows)
    j, m = 16, 0x0000FFFF
    while j:
        k = 0
        while k < 32:
            t = (a[k] ^ lax.shift_right_logical(a[k + j], jnp.int32(j))) & jnp.int32(m)
            a[k] = a[k] ^ t
            a[k + j] = a[k + j] ^ (t << j)
            k = (k + j + 1) & ~j
        j >>= 1
        if j:
            m = (m ^ (m << j)) & 0xFFFFFFFF
            m = m - (1 << 32) if m >= (1 << 31) else m
    return a[::-1]


def _dsa_kernel(iqt_ref, ik_ref, iwt_ref, qt_ref, k_ref, vt_ref, o_ref,
                iqbd_ref, bd_ref, key_ref, planes_ref, alive_ref, ab_ref, kn_ref, *, tq, slopes, topk):
    g = DSA_HEADS
    q0 = pl.program_id(1) * tq
    nfull = q0 // CH

    @pl.when(pl.program_id(1) == 0)
    def _():
        _key_norm_bound(k_ref, kn_ref, g, k_ref.shape[1] // CH)

    ngrp = (nfull + COUNT_UNROLL) // COUNT_UNROLL
    iqbd_ref[...] = _block_diag_tile(iqt_ref[0, 0], IDX_HEADS)
    bd_ref[...] = _block_diag_tile(qt_ref[0, 0], g)
    ab_ref[...] = _alibi_rows(slopes)
    w = iwt_ref[0, 0][0:IDX_HEADS, :] * (IDX_HEADS ** -0.5 * IDX_HD ** -0.5)

    def logits(j):
        return _dot(ik_ref[0, pl.ds(pl.multiple_of(j * CH, CH), CH), :], iqbd_ref[...])

    def score(lg, j, diag):
        half = CH // 2
        rows = []
        for r0 in (0, half):
            sc = jnp.maximum(lg[r0:r0 + half, 0:tq], 0.0) * w[0:1]
            for h in range(1, IDX_HEADS):
                sc = sc + jnp.maximum(lg[r0:r0 + half, h * tq:(h + 1) * tq], 0.0) * w[h:h + 1]
            sc = jnp.where(sc == 0.0, 0.0, sc)
            bits = pltpu.bitcast(sc, jnp.int32)
            key = bits ^ ((bits >> 31) & 0x7FFFFFFF)
            if diag:
                rel = _iota(key.shape, 0) - _iota(key.shape, 1)
                key = jnp.where(rel <= q0 - j * CH - r0, key, _INT_MIN)
            key_ref[j, r0:r0 + half, :] = key
            key3 = key.reshape(half // 8, 8, tq)
            rows += [key3[v] for v in range(half // 8)]
        planes = _bit_planes(rows)
        planes[31] = ~planes[31]
        planes_ref[j] = jnp.stack(planes, axis=0)
        if diag:
            lim = q0 - j * CH + _iota((8, tq), 1) - _iota((8, tq), 0)
            nbits = jnp.clip((lim >> 3) + 1, 0, 32)
            alive_ref[j] = jnp.where(nbits == 0, 0, jnp.left_shift(jnp.int32(-1), 32 - jnp.maximum(nbits, 1)))
        else:
            alive_ref[j] = jnp.full((8, tq), -1, jnp.int32)

    def score_group(i, c):
        lgs = [logits(SCORE_UNROLL * i + u) for u in range(SCORE_UNROLL)]
        for u in range(SCORE_UNROLL):
            score(lgs[u], SCORE_UNROLL * i + u, False)
        return c

    def score_one(j, c):
        score(logits(j), j, False)
        return c

    ngroup = nfull // SCORE_UNROLL
    lax.fori_loop(0, ngroup, score_group, 0)
    lax.fori_loop(SCORE_UNROLL * ngroup, nfull, score_one, 0)
    score(logits(nfull), nfull, True)

    for u in range(1, COUNT_UNROLL):
        @pl.when(nfull + u < ngrp * COUNT_UNROLL)
        def _():
            planes_ref[nfull + u] = jnp.zeros((32, 8, tq), jnp.int32)
            alive_ref[nfull + u] = jnp.zeros((8, tq), jnp.int32)

    def sweep(b_upd, keep, b_cnt):
        def body(gi, acc8):
            for u in range(COUNT_UNROLL):
                j = gi * COUNT_UNROLL + u
                a = alive_ref[j]
                if b_upd is not None:
                    a = a & ~(planes_ref[j, b_upd] ^ keep)
                    alive_ref[j] = a
                acc8 = acc8 + lax.population_count(a if b_cnt is None else a & planes_ref[j, b_cnt])
            return acc8
        acc8 = lax.fori_loop(0, ngrp, body, jnp.zeros((8, tq), jnp.int32))
        return jnp.sum(acc8, axis=0, keepdims=True)

    def decide(b, ones, want, thr):
        take = ones >= want
        thr = jnp.where(take, thr | jnp.left_shift(jnp.int32(1), b), thr)
        return jnp.where(take, want, want - ones), thr, jnp.where(take, -1, 0)

    want, thr, keep = decide(31, sweep(None, None, 31), jnp.full((1, tq), topk, jnp.int32),
                             jnp.zeros((1, tq), jnp.int32))

    def bit_body(i, c):
        want, thr, keep = c
        b = 30 - i
        return decide(b, sweep(b + 1, keep, b), want, thr)

    want, thr, keep = lax.fori_loop(0, 31, bit_body, (want, thr, keep))
    ties = sweep(0, keep, None)
    thr = thr ^ _INT_MIN
    tie_lane = (ties > want) & (thr > _INT_MIN)

    @pl.when(jnp.max(jnp.where(tie_lane, 1, 0)) > 0)
    def _():
        room = want.astype(_F32)
        lower = jnp.where(_iota((CH, CH), 0) > _iota((CH, CH), 1), 1.0, 0.0).astype(_BF16)

        def demote(j0, n, seen):
            keys = [key_ref[j0 + u] for u in range(n)]
            eqs = [(key == thr) & tie_lane for key in keys]
            eqfs = [jnp.where(eq, 1.0, 0.0) for eq in eqs]
            inside = [_dot(lower, eqf.astype(_BF16)) for eqf in eqfs]
            for u in range(n):
                key_ref[j0 + u] = jnp.where(eqs[u] & (inside[u] + seen >= room), thr - 1, keys[u])
                seen = seen + jnp.sum(eqfs[u], axis=0, keepdims=True)
            return seen

        ngroup = (nfull + 1) // COUNT_UNROLL
        seen = lax.fori_loop(0, ngroup, lambda i, c: demote(COUNT_UNROLL * i, COUNT_UNROLL, c),
                             jnp.zeros((1, tq), _F32))
        lax.fori_loop(COUNT_UNROLL * ngroup, nfull + 1, lambda j, c: demote(j, 1, c), seen)

    thr_eff = jnp.maximum(thr, _INT_MIN + 1)

    def near_body(j, best):
        pos = jnp.where(key_ref[j] >= thr_eff, _iota((CH, tq), 0) + j * CH, -1)
        return jnp.maximum(best, jnp.max(pos, axis=0, keepdims=True))

    best = lax.fori_loop(jnp.maximum(nfull - (NEAR_CHUNKS - 1), 0), nfull + 1, near_body,
                         jnp.full((1, tq), -1, jnp.int32))
    gap = jnp.where(best >= 0, q0 + _iota((1, tq), 1) - best, 2 ** 24)
    bound = _logit_bound(qt_ref[0, 0], kn_ref, g, DSA_HD ** -0.5)
    first = _first_chunks(bound, slopes, q0, nfull, jnp.max(gap, axis=1, keepdims=True))
    ceiling, ceiling_ok = _logit_ceiling(bound, slopes, tq, gap.astype(_F32))
    ceiling_ok = ceiling_ok & (jnp.min(best) >= 0)

    def mask(j, aux, diag):
        return jnp.where(key_ref[j] >= thr_eff, 0.0, NEG_INF), aux

    def attend(ceil):
        l, acc = _attend(k_ref, vt_ref, bd_ref, ab_ref, _slope_row(slopes, 1, tq), DSA_HD ** -0.5 * _LOG2E,
                         q0, nfull, tq, tq, mask, jnp.zeros((1, tq), _F32), first, ceil)
        linv = 1.0 / l
        outs = [acc[h] * linv[:, h * tq:(h + 1) * tq] for h in range(g)]
        o_ref[0] = jnp.concatenate(outs, axis=0).T.astype(_BF16)

    @pl.when(ceiling_ok)
    def _():
        attend(ceiling)

    @pl.when(jnp.logical_not(ceiling_ok))
    def _():
        attend(None)


def _dsa_attention(n16, t16, t32, bn, s, slopes):
    tq = TQ
    ns, per = s // CH, CH // tq
    g = DSA_HEADS
    topk = min(TOPK_MAX, s // 4)
    assert ns % COUNT_UNROLL == 0
    kern = functools.partial(_dsa_kernel, tq=tq, slopes=slopes, topk=topk)
    return pl.pallas_call(
        kern,
        grid=(bn, s // tq),
        in_specs=[
            pl.BlockSpec((1, 1, 128, tq), lambda b, i: (b, i // per, 10, i % per)),
            pl.BlockSpec((1, s, 128), lambda b, i: (b, 0, 10)),
            pl.BlockSpec((1, 1, 16, tq), lambda b, i: (b, i // per, 16, i % per)),
            pl.BlockSpec((1, 1, 256, tq), lambda b, i: (b, i // per, 2, i % per)),
            pl.BlockSpec((1, s, 256), lambda b, i: (b, 0, 1)),
            pl.BlockSpec((1, ns, 256, CH), lambda b, i: (b, 0, 3, 0)),
        ],
        out_specs=pl.BlockSpec((1, tq, 256), lambda b, i: (b, i, 0)),
        out_shape=jax.ShapeDtypeStruct((bn, s, 256), _BF16),
        scratch_shapes=[
            pltpu.VMEM((128, IDX_HEADS * tq), _BF16), pltpu.VMEM((256, g * tq), _BF16),
            pltpu.VMEM((ns, CH, tq), jnp.int32), pltpu.VMEM((ns, 32, 8, tq), jnp.int32),
            pltpu.VMEM((ns, 8, tq), jnp.int32), pltpu.VMEM((DSA_HEADS, CH, 128), _F32),
            pltpu.VMEM((16, 128), _F32),
        ],
        compiler_params=pltpu.CompilerParams(
            dimension_semantics=("parallel", "arbitrary"), vmem_limit_bytes=V7X_VMEM_LIMIT),
        name="dsa_attn",
    )(t16, n16, t32, t16, n16, t16)


def _ret_kernel(q_ref, kt_ref, v_ref, g_ref, intra_ref, qdec_ref, kdect_ref, cd_ref, nw_ref, o_ref, s_ref, *, c):
    @pl.when(pl.program_id(1) == 0)
    def _():
        s_ref[...] = jnp.zeros(s_ref.shape, _F32)

    q = q_ref[0]
    v = v_ref[0]
    kt = kt_ref[0, 0] * (RET_QK ** -0.5)
    att = _dot(q, _block_diag_tile(kt.astype(_BF16), RET_HEADS)) * intra_ref[...]
    vt = jnp.concatenate([v] * RET_HEADS, axis=0)
    vbd = jnp.where((_iota(vt.shape, 0) // c) == (_iota(vt.shape, 1) // RET_V), vt, jnp.zeros_like(vt))
    st = s_ref[...]
    o = _dot(att.astype(_BF16), vbd) + _dot(q, st.astype(_BF16)) * qdec_ref[...]
    upd = _dot((kt * kdect_ref[...]).astype(_BF16), v)
    same_head = (_iota(upd.shape, 0) // RET_QK) == (_iota(upd.shape, 1) // RET_V)
    s_ref[...] = st * cd_ref[...] + jnp.where(same_head, upd, 0.0)

    mu = _group_mean(o, RET_V)
    d = o - mu
    var = _group_mean(d * d, RET_V)
    y = d * lax.rsqrt(var + LN_EPS) * nw_ref[...]
    gate = g_ref[0]
    o_ref[0] = (gate * jax.nn.sigmoid(gate) * y).astype(_BF16)


def _retention_consts(c):
    h = RET_HEADS
    log_g = np.log1p(-np.power(2.0, -5.0 - np.arange(h, dtype=np.float64)))
    pos = np.arange(c, dtype=np.float64)
    rel = pos[:, None] - pos[None, :]
    intra = np.where(rel >= 0, np.exp(log_g[:, None, None] * np.maximum(rel, 0.0)), 0.0)
    intra = np.transpose(intra, (1, 0, 2)).reshape(c, h * c)
    qdec = np.repeat(np.exp(log_g[:, None] * (pos[None, :] + 1.0)).T, RET_V, axis=1)
    kdect = np.repeat(np.exp(log_g[:, None] * (c - 1.0 - pos[None, :])), RET_QK, axis=0)
    cd = np.repeat(np.exp(log_g * c), RET_QK)[:, None] * np.ones((1, h * RET_V))
    return tuple(jnp.asarray(a, _F32) for a in (intra, qdec, kdect, cd))


def _retention(norm_w, n16, n32, t32, bn, s):
    c = RET_C
    per = CH // c
    intra, qdec, kdect, cd = _retention_consts(c)
    return pl.pallas_call(
        functools.partial(_ret_kernel, c=c),
        grid=(bn, s // c),
        in_specs=[
            pl.BlockSpec((1, c, 256), lambda b, i: (b, i, 2)),
            pl.BlockSpec((1, 1, 256, c), lambda b, i: (b, i // per, 0, i % per)),
            pl.BlockSpec((1, c, 256), lambda b, i: (b, i, 3)),
            pl.BlockSpec((1, c, 256), lambda b, i: (b, i, 0)),
            _const_spec(intra.shape), _const_spec(qdec.shape), _const_spec(kdect.shape),
            _const_spec(cd.shape), _const_spec(norm_w.shape),
        ],
        out_specs=pl.BlockSpec((1, c, 256), lambda b, i: (b, i, 0)),
        out_shape=jax.ShapeDtypeStruct((bn, s, 256), _BF16),
        scratch_shapes=[pltpu.VMEM((RET_HEADS * RET_QK, RET_HEADS * RET_V), _F32)],
        compiler_params=pltpu.CompilerParams(dimension_semantics=("parallel", "arbitrary")),
        name="retention",
    )(n16, t32, n16, n32, intra, qdec, kdect, cd, norm_w)


def _gla_kernel(q_ref, k_ref, a_ref, v_ref, vt_ref, g_ref, wa_ref, ba_ref, nw_ref, o_ref,
                st_ref, u_ref, oacc_ref, qs_ref, kk_ref, b_ref, qh_ref, dec_ref, *, ts, sub):
    nsub = ts // sub
    assert sub == 16

    @pl.when(pl.program_id(1) == 0)
    def _():
        st_ref[...] = jnp.zeros(st_ref.shape, _F32)

    la = jax.nn.log_sigmoid(_dot(a_ref[0].astype(_BF16), wa_ref[...]) + ba_ref[...]) * (1.0 / GLA_GATE_TEMP)
    in_blk = _iota(la.shape, 0) % sub
    b = la
    for sh in (1, 2, 4, 8):
        b = b + jnp.where(in_blk >= sh, pltpu.roll(b, sh, 0), 0.0)
    bl = jnp.where(in_blk == sub - 1, b, 0.0)
    for sh in (1, 2, 4, 8):
        bl = bl + jnp.where(in_blk < sub - sh, pltpu.roll(bl, ts - sh, 0), 0.0)
    qs = q_ref[0] * (GLA_QK ** -0.5)
    kk = k_ref[0]
    kd = (kk * jnp.exp(bl - b)).astype(_BF16)
    qs_ref[...] = qs
    kk_ref[...] = kk
    b_ref[...] = b
    qh_ref[...] = (qs * jnp.exp(b)).astype(_BF16)
    dec_ref[...] = jnp.exp(bl)
    vt = vt_ref[0, 0]
    row_blk = _iota(kd.shape, 0) // sub
    kd_wide = jnp.concatenate([jnp.where(row_blk == n, kd, jnp.zeros_like(kd)) for n in range(nsub)], axis=1)
    u_all = _dot(vt, kd_wide)
    for n in range(nsub):
        u_ref[n] = u_all[:, n * 128:(n + 1) * 128]

    st_keep = (_iota(st_ref.shape, 0) // GLA_V) == (_iota(st_ref.shape, 1) // GLA_QK)
    spread = jnp.where((_iota((128, 256), 0) // GLA_QK) == (_iota((128, 256), 1) // GLA_V), 1.0, 0.0).astype(_BF16)
    row16 = _iota((sub, 128), 0)

    def body(n, carry):
        r0 = pl.multiple_of(n * sub, sub)
        st = st_ref[...]
        o_cross = _dot_nt(qh_ref[pl.ds(r0, sub), :], st.astype(_BF16))
        q16 = qs_ref[pl.ds(r0, sub), :]
        k16 = kk_ref[pl.ds(r0, sub), :]
        b16 = b_ref[pl.ds(r0, sub), :]
        v16 = v_ref[0, pl.ds(r0, sub), :].astype(_F32)
        es = []
        for j in range(sub):
            e = q16 * k16[j:j + 1] * jnp.exp(jnp.minimum(b16 - b16[j:j + 1], 0.0))
            es.append(jnp.where(row16 >= j, e, 0.0))
        e_all = jnp.concatenate(es, axis=0)
        e_hi = e_all.astype(_BF16)
        e_lo = (e_all - e_hi.astype(_F32)).astype(_BF16)
        att = _dot(e_hi, spread) + _dot(e_lo, spread)
        o_diag = att[0:sub] * v16[0:1]
        for j in range(1, sub):
            o_diag = o_diag + att[j * sub:(j + 1) * sub] * v16[j:j + 1]
        oacc_ref[pl.ds(r0, sub), :] = o_cross + o_diag
        st_ref[...] = st * dec_ref[pl.ds(r0, 1), :] + jnp.where(st_keep, u_ref[n], 0.0)
        return carry

    lax.fori_loop(0, nsub, body, 0, unroll=2)

    o = oacc_ref[...]
    ms = _group_mean(o * o, GLA_V)
    y = o * lax.rsqrt(ms + LN_EPS) * nw_ref[...]
    gate = g_ref[0]
    o_ref[0] = (gate * jax.nn.sigmoid(gate) * y).astype(_BF16)


def _gla(wa, ba, norm_w, n16, n32, t16, bn, s):
    ts, sub = CH, GLA_SUB
    return pl.pallas_call(
        functools.partial(_gla_kernel, ts=ts, sub=sub),
        grid=(bn, s // ts),
        in_specs=[
            pl.BlockSpec((1, ts, 128), lambda b, i: (b, i, 4)),
            pl.BlockSpec((1, ts, 128), lambda b, i: (b, i, 5)),
            pl.BlockSpec((1, ts, 128), lambda b, i: (b, i, 6)),
            pl.BlockSpec((1, ts, 256), lambda b, i: (b, i, 4)),
            pl.BlockSpec((1, 1, 256, ts), lambda b, i: (b, i, 4, 0)),
            pl.BlockSpec((1, ts, 256), lambda b, i: (b, i, 1)),
            _const_spec(wa.shape), _const_spec(ba.shape), _const_spec(norm_w.shape),
        ],
        out_specs=pl.BlockSpec((1, ts, 256), lambda b, i: (b, i, 0)),
        out_shape=jax.ShapeDtypeStruct((bn, s, 256), _BF16),
        scratch_shapes=[
            pltpu.VMEM((GLA_HEADS * GLA_V, GLA_HEADS * GLA_QK), _F32),
            pltpu.VMEM((ts // sub, GLA_HEADS * GLA_V, GLA_HEADS * GLA_QK), _F32),
            pltpu.VMEM((ts, 256), _F32),
            pltpu.VMEM((ts, 128), _F32), pltpu.VMEM((ts, 128), _F32), pltpu.VMEM((ts, 128), _F32),
            pltpu.VMEM((ts, 128), _BF16), pltpu.VMEM((ts, 128), _F32),
        ],
        compiler_params=pltpu.CompilerParams(dimension_semantics=("parallel", "arbitrary")),
        name="gla",
    )(n32, n32, n32, n16, t16, n32, wa, ba, norm_w)


def _merge_kernel(x_ref, ya_ref, yb_ref, yc_ref, yd_ref, wg_ref, wbr_ref, wout_ref, lnw_ref, lnb_ref, h_ref):
    x = x_ref[...]
    xb = x.astype(_BF16)
    merged = None
    for n, y_ref in enumerate((ya_ref, yb_ref, yc_ref, yd_ref)):
        gate = jax.nn.sigmoid(_dot(xb, wg_ref[:, n * D_MODEL:(n + 1) * D_MODEL]))
        term = gate * _dot(y_ref[...], wbr_ref[n])
        merged = term if merged is None else merged + term
    mix = _dot(merged.astype(_BF16), wout_ref[...])
    h_ref[...] = _layer_norm(DEEPNORM_ALPHA * x + mix, lnw_ref[...], lnb_ref[...])


def _merge(x2, ys, wg, wbr, wout, lnw, lnb):
    t = x2.shape[0]
    ts = TS_DENSE
    tok = lambda w: pl.BlockSpec((ts, w), lambda i: (i, 0))
    return pl.pallas_call(
        _merge_kernel,
        grid=(t // ts,),
        in_specs=[tok(D_MODEL)] + [tok(BRANCH_W)] * 4 + [
            _const_spec(wg.shape), _const_spec(wbr.shape), _const_spec(wout.shape),
            _const_spec(lnw.shape), _const_spec(lnb.shape)],
        out_specs=tok(D_MODEL),
        out_shape=jax.ShapeDtypeStruct((t, D_MODEL), _F32),
        compiler_params=pltpu.CompilerParams(
            dimension_semantics=("parallel",), vmem_limit_bytes=V7X_VMEM_LIMIT),
        name="merge_ln",
    )(x2, *ys, wg, wbr, wout, lnw, lnb)


def _ffn_kernel(h_ref, p_ref, wup_ref, cw_ref, cb_ref, wdn_ref, wpg_ref, wpp_ref, lnw_ref, lnb_ref,
                o_ref, tail_ref, *, ts):
    @pl.when(pl.program_id(1) == 0)
    def _():
        tail_ref[...] = jnp.zeros(tail_ref.shape, _F32)

    h = h_ref[...]
    hb = h.astype(_BF16)
    row = _iota((ts, FF_COLS), 0)
    f = None
    for c0 in range(0, D_FF, FF_COLS):
        cols = slice(c0, c0 + FF_COLS)
        u = _dot(hb, wup_ref[:, cols])
        gt = _dot(hb, wup_ref[:, D_FF + c0:D_FF + c0 + FF_COLS])
        prev = tail_ref[:, cols]
        g1 = jnp.where(row == 0, prev[7:8], pltpu.roll(gt, 1, 0))
        g2 = jnp.where(row == 0, prev[6:7], jnp.where(row == 1, prev[7:8], pltpu.roll(gt, 2, 0)))
        tail_ref[:, cols] = gt[ts - 8:ts]
        gc = cb_ref[:, cols] + cw_ref[0:1, cols] * g2
        gc = gc + cw_ref[1:2, cols] * g1
        gc = gc + cw_ref[2:3, cols] * gt
        term = _dot((jax.nn.gelu(gc) * u).astype(_BF16), wdn_ref[cols, :])
        f = term if f is None else f + term
    e = jax.nn.sigmoid(_dot(hb, wpg_ref[...])) * _dot(p_ref[...].astype(_BF16), wpp_ref[...])
    o_ref[...] = _layer_norm(DEEPNORM_ALPHA * h + f + e, lnw_ref[...], lnb_ref[...])


def _ffn(h2, p2, wup, cw, cb, wdn, wpg, wpp, lnw, lnb, bn, s):
    ts = TS_DENSE
    ns = s // ts
    tok = lambda w: pl.BlockSpec((ts, w), lambda b, i: (b * ns + i, 0))
    return pl.pallas_call(
        functools.partial(_ffn_kernel, ts=ts),
        grid=(bn, ns),
        in_specs=[tok(D_MODEL), tok(P_DIM)] + [_const_spec(a.shape) for a in (wup, cw, cb, wdn, wpg, wpp, lnw, lnb)],
        out_specs=tok(D_MODEL),
        out_shape=jax.ShapeDtypeStruct((bn * s, D_MODEL), _F32),
        scratch_shapes=[pltpu.VMEM((8, D_FF), _F32)],
        compiler_params=pltpu.CompilerParams(
            dimension_semantics=("parallel", "arbitrary"), vmem_limit_bytes=V7X_VMEM_LIMIT),
        name="ffn_ple_ln",
    )(h2, p2, wup, cw, cb, wdn, wpg, wpp, lnw, lnb)


def kernel(x, p, w_in, a_lambda, a_norm_w, ret_norm_w, gla_w_a2, gla_b_a, gla_norm_w, w_branch, w_out,
           ln1_w, ln1_b, w_ffn_up, ffn_conv_w, ffn_conv_b, w_ffn_down, w_ple_gate, w_ple_proj, ln2_w, ln2_b):
    bn, s, _ = x.shape
    t = bn * s
    slopes = [2.0 ** (-(8.0 / N_SOFTMAX_HEADS) * i) for i in range(1, N_SOFTMAX_HEADS + 1)]
    slopes_a, slopes_b = tuple(slopes[0::2]), tuple(slopes[1::2])
    row = lambda v: v.astype(_F32).reshape(1, -1)
    x2 = x.reshape(t, D_MODEL)
    for i in range(DEPTH):
        *proj_w, wg = _prep_weights(w_in, i)
        n32, n16, t16, t32 = _project(x2, *proj_w, bn, s)
        n32 = n32.reshape(bn, s, -1)
        n16 = n16.reshape(bn, s, -1)
        lam_init = 0.8 - 0.6 * math.exp(-0.3 * i)
        y_a = _diff_attention(a_lambda[i].astype(_F32), a_norm_w[i].astype(_F32).reshape(DA_V, 1),
                              n16, t16, bn, s, slopes_a, lam_init)
        y_b = _dsa_attention(n16, t16, t32, bn, s, slopes_b)
        y_c = _retention(row(ret_norm_w[i]), n16, n32, t32, bn, s)
        wa = jnp.pad(gla_w_a2[i], ((0, 128 - GLA_RANK), (0, 0))).astype(_BF16)
        y_d = _gla(wa, row(gla_b_a[i]), row(jnp.tile(gla_norm_w[i], GLA_HEADS)), n16, n32, t16, bn, s)
        ys = [y.reshape(t, BRANCH_W) for y in (y_a, y_b, y_c, y_d)]
        h2 = _merge(x2, ys, wg, w_branch[i].astype(_BF16), w_out[i].astype(_BF16), row(ln1_w[i]), row(ln1_b[i]))
        x2 = _ffn(h2, p[i].reshape(t, P_DIM), w_ffn_up[i].astype(_BF16), ffn_conv_w[i].astype(_F32),
                  row(ffn_conv_b[i]), w_ffn_down[i].astype(_BF16), w_ple_gate[i].astype(_BF16),
                  w_ple_proj[i].astype(_BF16), row(ln2_w[i]), row(ln2_b[i]), bn, s)
    return x2.reshape(bn, s, D_MODEL)
```

```python
import functools
import math

import numpy as np
import jax
import jax.numpy as jnp
from jax import lax
from jax.experimental import pallas as pl
from jax.experimental.pallas import tpu as pltpu

D_MODEL = 1024
DEPTH = 2
P_DIM = 256
N_BRANCH = 4
BRANCH_W = 256
DA_HEADS = 4
DA_QK = 32
DA_V = 64
DSA_HEADS = 4
DSA_HD = 64
IDX_HEADS = 4
IDX_HD = 32
TOPK_MAX = 256
RET_HEADS = 4
RET_QK = 64
RET_V = 64
GLA_HEADS = 4
GLA_QK = 32
GLA_V = 64
GLA_RANK = 16
GLA_GATE_TEMP = 16.0
D_FF = 2816
CONV_W = 3
N_SOFTMAX_HEADS = DA_HEADS + DSA_HEADS
LN_EPS = 1e-5
NEG_INF = -1e30
DEEPNORM_ALPHA = (2.0 * DEPTH) ** 0.25

IN_SIZES = (256, 256, 256, 256, 256, 256, 128, 32, 4, 256, 256, 256, 256, 128, 128, 256, 16, 256, 4096)
IN_NAMES = ("a_q", "a_k", "a_v", "b_q", "b_k", "b_v", "b_iq", "b_ik", "b_iw",
            "c_q", "c_k", "c_v", "c_g", "d_q", "d_k", "d_v", "d_a", "d_g", "m_g")

_BF16 = jnp.bfloat16
_F32 = jnp.float32
_INT_MIN = -2 ** 31
_LOG2E = math.log2(math.e)

CH = 256
TQ = 256
RET_C = 128
GLA_SUB = 16
COUNT_UNROLL = 8
SCORE_UNROLL = 4
ATTEND_UNROLL = 4
TS_DENSE = 512
FF_COLS = 256
PREP_ROWS = 256
NEAR_CHUNKS = 4
SKIP_NATS = 32.0
FIXED_MAX_BITS = 100.0
V7X_VMEM_LIMIT = 56 * 1024 * 1024


def _dot(a, b, precision=None):
    return jnp.dot(a, b, preferred_element_type=_F32, precision=precision)


def _dot_nt(a, b):
    return lax.dot_general(a, b, (((1,), (1,)), ((), ())), preferred_element_type=_F32)


def _iota(shape, dim):
    return lax.broadcasted_iota(jnp.int32, shape, dim)


def _block_diag_tile(m_t, ngroups):
    r, tq = m_t.shape
    tiled = jnp.concatenate([m_t] * ngroups, axis=1)
    keep = (_iota(tiled.shape, 0) // (r // ngroups)) == (_iota(tiled.shape, 1) // tq)
    return jnp.where(keep, tiled, jnp.zeros_like(tiled))


def _layer_norm(x, w, b):
    mu = jnp.mean(x, -1, keepdims=True)
    var = jnp.mean(jnp.square(x - mu), -1, keepdims=True)
    return (x - mu) * lax.rsqrt(var + LN_EPS) * w + b


def _group_mean(x, group):
    lane_g = _iota(x.shape, 1) // group
    out = jnp.zeros_like(x)
    for h in range(x.shape[1] // group):
        mk = lane_g == h
        mh = jnp.sum(jnp.where(mk, x, 0.0), axis=1, keepdims=True) * (1.0 / group)
        out = jnp.where(mk, mh, out)
    return out


def _const_spec(shape):
    nd = len(shape)
    return pl.BlockSpec(shape, lambda *_: (0,) * nd, pipeline_mode=pl.Buffered(1))


def _prep_kernel(w_ref, wn32_ref, wn16_ref, wt16_ref, wt32_ref, wg_ref):
    offs = [0] + np.cumsum(IN_SIZES).tolist()
    col = {n: w_ref[0, :, offs[i]:offs[i + 1]] for i, n in enumerate(IN_NAMES)}
    zeros = lambda n: jnp.zeros((w_ref.shape[1], n), _F32)
    cat = lambda xs: jnp.concatenate(xs, axis=1)
    wn32_ref[...] = cat([col["c_g"], col["d_g"], col["d_q"], col["d_k"], col["d_a"],
                         zeros(128 - GLA_RANK)]).astype(_BF16)
    wn16_ref[...] = cat([col["a_k"], col["b_k"], col["c_q"], col["c_v"], col["d_v"]]
                        + [col["b_ik"]] * IDX_HEADS).astype(_BF16)
    wt16_ref[...] = cat([col["a_q"], col["a_v"], col["b_q"], col["b_v"], col["d_v"], col["b_iq"]]).T.astype(_BF16)
    wt32_ref[...] = cat([col["c_k"], col["b_iw"], zeros(16 - IDX_HEADS)]).T.astype(_BF16)
    wg_ref[...] = col["m_g"].astype(_BF16)


def _prep_weights(w_in, layer):
    rb = PREP_ROWS
    n32, n16, t16, t32, ng = 896, 1408, 1408, 272, N_BRANCH * D_MODEL
    rows = lambda n: pl.BlockSpec((rb, n), lambda r: (r, 0))
    cols = lambda n: pl.BlockSpec((n, rb), lambda r: (0, r))
    return pl.pallas_call(
        _prep_kernel,
        grid=(D_MODEL // rb,),
        in_specs=[pl.BlockSpec((1, rb, w_in.shape[2]), lambda r: (layer, r, 0))],
        out_specs=[rows(n32), rows(n16), cols(t16), cols(t32), rows(ng)],
        out_shape=[jax.ShapeDtypeStruct((D_MODEL, n32), _BF16), jax.ShapeDtypeStruct((D_MODEL, n16), _BF16),
                   jax.ShapeDtypeStruct((t16, D_MODEL), _BF16), jax.ShapeDtypeStruct((t32, D_MODEL), _BF16),
                   jax.ShapeDtypeStruct((D_MODEL, ng), _BF16)],
        compiler_params=pltpu.CompilerParams(
            dimension_semantics=("parallel",), vmem_limit_bytes=V7X_VMEM_LIMIT),
        name="prep_weights",
    )(w_in)


def _proj_kernel(x_ref, wn32_ref, wn16_ref, wt16_ref, wt32_ref, n32_ref, n16_ref, t16_ref, t32_ref):
    x = x_ref[...].astype(_BF16)
    n32_ref[...] = _dot(x, wn32_ref[...])
    n16_ref[...] = _dot(x, wn16_ref[...]).astype(_BF16)
    t16_ref[0, 0] = _dot_nt(wt16_ref[...], x).astype(_BF16)
    t32_ref[0, 0] = _dot_nt(wt32_ref[...], x)


def _project(x2, wn32, wn16, wt16, wt32, bn, s):
    ns = s // CH
    n32, n16, t16, t32 = wn32.shape[1], wn16.shape[1], wt16.shape[0], wt32.shape[0]
    return pl.pallas_call(
        _proj_kernel,
        grid=(bn, ns),
        in_specs=[
            pl.BlockSpec((CH, D_MODEL), lambda b, i: (b * ns + i, 0)),
            _const_spec(wn32.shape), _const_spec(wn16.shape), _const_spec(wt16.shape), _const_spec(wt32.shape),
        ],
        out_specs=[
            pl.BlockSpec((CH, n32), lambda b, i: (b * ns + i, 0)),
            pl.BlockSpec((CH, n16), lambda b, i: (b * ns + i, 0)),
            pl.BlockSpec((1, 1, t16, CH), lambda b, i: (b, i, 0, 0)),
            pl.BlockSpec((1, 1, t32, CH), lambda b, i: (b, i, 0, 0)),
        ],
        out_shape=[
            jax.ShapeDtypeStruct((bn * s, n32), _F32),
            jax.ShapeDtypeStruct((bn * s, n16), _BF16),
            jax.ShapeDtypeStruct((bn, ns, t16, CH), _BF16),
            jax.ShapeDtypeStruct((bn, ns, t32, CH), _F32),
        ],
        compiler_params=pltpu.CompilerParams(
            dimension_semantics=("parallel", "parallel"), vmem_limit_bytes=V7X_VMEM_LIMIT),
        name="proj",
    )(x2, wn32, wn16, wt16, wt32)


def _slope_row(slopes, reps, tq):
    return jnp.concatenate([jnp.full((1, tq), s * _LOG2E, _F32) for s in slopes for _ in range(reps)], axis=1)


def _alibi_rows(slopes):
    rows = _iota((CH, 128), 0).astype(_F32)
    return jnp.stack([rows * (s * _LOG2E) for s in slopes], axis=0)


def _key_norm_bound(k_ref, kn_ref, ngroups, nchunks):
    gt = jnp.where(_iota((16, 256), 0) == _iota((16, 256), 1) // (256 // ngroups), 1.0, 0.0).astype(_BF16)

    def body(j, best):
        kc = k_ref[0, pl.ds(pl.multiple_of(j * CH, CH), CH), :].astype(_F32)
        return jnp.maximum(best, _dot_nt(gt, (kc * kc).astype(_BF16)))

    best = lax.fori_loop(0, nchunks, body, jnp.zeros((16, CH), _F32))
    kn_ref[...] = jnp.broadcast_to(jnp.sqrt(jnp.max(best, axis=1, keepdims=True) * (1.0 + 2.0 ** -7)), kn_ref.shape)


def _logit_bound(qt, kn_ref, ngroups, scale):
    q = qt.astype(_F32)
    tq = q.shape[1]
    qn = jnp.sqrt(jnp.max(jnp.sum((q * q).reshape(ngroups, 256 // ngroups, tq), axis=1), axis=1, keepdims=True))
    return scale * qn * kn_ref[0:ngroups, 0:1]


def _first_chunks(bound, slopes, q0, nfull, nearest=0):
    per_head = bound.shape[0] // len(slopes)
    inv_slope = jnp.concatenate([jnp.full((per_head, 1), 1.0 / s, _F32) for s in slopes], axis=0)
    reach = ((2.0 * bound + SKIP_NATS) * inv_slope).astype(jnp.int32) + 2 + nearest
    last_far = q0 - CH + 1 - reach
    first = jnp.where(last_far < 0, 0, last_far // CH + 1)
    first = jnp.minimum(first, nfull)
    return [jnp.min(first[h * per_head:(h + 1) * per_head]) for h in range(len(slopes))]


def _logit_ceiling(bound, slopes, tq, nearest=None):
    per_head = bound.shape[0] // len(slopes)
    lane = _iota((1, tq), 1).astype(_F32)
    if nearest is not None:
        lane = lane - nearest
    rows = [bound[g:g + 1, :] * _LOG2E + (slopes[g // per_head] * _LOG2E) * lane for g in range(bound.shape[0])]
    deep = jnp.max(jnp.where(2.0 * _LOG2E * bound > FIXED_MAX_BITS, 1, 0)) > 0
    return jnp.concatenate(rows, axis=1), jnp.logical_not(deep)


def _attend(k_ref, vt_ref, bd_ref, ab_ref, slope_row, c1, q0, nfull, tq, w, mask_fn, aux0, first, ceiling=None):
    g_tq = bd_ref.shape[1]
    nheads = 4

    def pv(j, p, heads):
        vt_c = vt_ref[0, j]
        return {h: _dot(vt_c[h * 64:(h + 1) * 64, :], p[:, h * w:(h + 1) * w]) for h in heads}

    def qk(j, heads):
        kc = k_ref[0, pl.ds(pl.multiple_of(j * CH, CH), CH), :]
        tiles = sorted({c0 // 256 for h in heads for c0 in range(h * w, (h + 1) * w, 128)})
        return {t: _dot(kc, bd_ref[:, t * 256:(t + 1) * 256]) for t in tiles}

    def softmax(s, j, m, l, aux, heads, diag):
        crow = slope_row * (j * CH - q0).astype(_F32)
        amask, aux = mask_fn(j, aux, diag)
        ps, ms, ls, alphas = [], [], [], []
        for c0 in range(0, g_tq, 128):
            cols = slice(c0, c0 + 128)
            if c0 // w not in heads:
                ps.append(jnp.zeros((CH, 128), _BF16))
                ms.append(m[:, cols])
                ls.append(l[:, cols])
                alphas.append(jnp.ones((1, 128), _F32))
                continue
            t = s[c0 // 256][:, c0 % 256:c0 % 256 + 128] * c1 + ab_ref[c0 // w]
            if amask is not None:
                t = t + amask[:, c0 % tq:c0 % tq + 128]
            if ceiling is not None:
                p = jnp.exp2(t + (crow[:, cols] - ceiling[:, cols]))
                m_new, alpha = m[:, cols], jnp.ones((1, 128), _F32)
            else:
                m_new = jnp.maximum(m[:, cols], jnp.max(t, axis=0, keepdims=True) + crow[:, cols])
                alpha = jnp.exp2(m[:, cols] - m_new)
                p = jnp.exp2(t - (m_new - crow[:, cols]))
            ls.append(alpha * l[:, cols] + jnp.sum(p, axis=0, keepdims=True))
            ps.append(p.astype(_BF16))
            ms.append(m_new)
            alphas.append(alpha)
        cat = lambda xs: jnp.concatenate(xs, axis=1)
        return cat(ms), cat(ls), cat(alphas), cat(ps), aux

    def step(s, j, carry, heads, diag=False):
        m, l, acc, aux = carry
        m, l, alpha, p, aux = softmax(s, j, m, l, aux, heads, diag)
        pvs = pv(j, p, heads)
        if ceiling is not None:
            acc = [acc[h] + pvs[h] if h in heads else acc[h] for h in range(nheads)]
        else:
            acc = [alpha[:, h * w:(h + 1) * w] * acc[h] + pvs[h] if h in heads else acc[h] for h in range(nheads)]
        return m, l, acc, aux

    def run(lo, hi, heads, carry):
        unroll = ATTEND_UNROLL

        def group(i, c):
            j = lo + unroll * i
            ss = [qk(j + u, heads) for u in range(unroll)]
            for u in range(unroll):
                c = step(ss[u], j + u, c, heads)
            return c

        ngroup = jnp.maximum(hi - lo, 0) // unroll
        carry = lax.fori_loop(0, ngroup, group, carry)
        return lax.fori_loop(lo + unroll * ngroup, hi, lambda j, c: step(qk(j, heads), j, c, heads), carry)

    carry = (jnp.full((1, g_tq), NEG_INF, _F32), jnp.zeros((1, g_tq), _F32),
             [jnp.zeros((64, w), _F32) for _ in range(nheads)], aux0)
    lows, hi = [], nfull
    for h in range(nheads):
        u = ATTEND_UNROLL
        lo = jnp.maximum(hi - u * ((hi - jnp.minimum(first[h], hi) + u - 1) // u), 0)
        lows.append(lo)
        hi = lo
    for h in reversed(range(nheads)):
        carry = run(lows[h], lows[h - 1] if h else nfull, tuple(range(h, nheads)), carry)
    heads = tuple(range(nheads))
    m, l, acc, aux = step(qk(nfull, heads), nfull, carry, heads, True)
    return l, acc


def _attn_a_kernel(lam_ref, nw_ref, qt_ref, k_ref, vt_ref, o_ref, bd_ref, ab_ref, kn_ref, *, tq, slopes, lam_init):
    g = 2 * DA_HEADS
    q0 = pl.program_id(1) * tq
    nfull = q0 // CH

    @pl.when(pl.program_id(1) == 0)
    def _():
        _key_norm_bound(k_ref, kn_ref, g, k_ref.shape[1] // CH)

    bd_ref[...] = _block_diag_tile(qt_ref[0, 0], g)
    ab_ref[...] = _alibi_rows(slopes)

    def mask(j, aux, diag):
        if not diag:
            return None, aux
        rel = _iota((CH, tq), 0) - _iota((CH, tq), 1)
        return jnp.where(rel <= q0 - j * CH, 0.0, NEG_INF), aux

    bound = _logit_bound(qt_ref[0, 0], kn_ref, g, DA_QK ** -0.5)
    first = _first_chunks(bound, slopes, q0, nfull)
    ceiling, ceiling_ok = _logit_ceiling(bound, slopes, tq)

    def attend(ceil):
        l, acc = _attend(k_ref, vt_ref, bd_ref, ab_ref, _slope_row(slopes, 2, tq), DA_QK ** -0.5 * _LOG2E,
                         q0, nfull, tq, 2 * tq, mask, jnp.zeros((1, tq), _F32), first, ceil)
        lp = lam_ref[...]
        lam = (jnp.exp(jnp.sum(lp[0:1] * lp[1:2], axis=1, keepdims=True))
               - jnp.exp(jnp.sum(lp[2:3] * lp[3:4], axis=1, keepdims=True)) + lam_init)
        linv = 1.0 / l
        outs = []
        for h in range(DA_HEADS):
            a = acc[h] * linv[:, h * 2 * tq:(h + 1) * 2 * tq]
            o = a[:, :tq] - lam * a[:, tq:]
            ms = jnp.mean(o * o, axis=0, keepdims=True)
            outs.append(o * lax.rsqrt(ms + LN_EPS) * nw_ref[...] * (1.0 - lam_init))
        o_ref[0] = jnp.concatenate(outs, axis=0).T.astype(_BF16)

    @pl.when(ceiling_ok)
    def _():
        attend(ceiling)

    @pl.when(jnp.logical_not(ceiling_ok))
    def _():
        attend(None)


def _diff_attention(lam_p, norm_w, n16, t16, bn, s, slopes, lam_init):
    tq = TQ
    ns, per = s // CH, CH // tq
    g = 2 * DA_HEADS
    kern = functools.partial(_attn_a_kernel, tq=tq, slopes=slopes, lam_init=lam_init)
    return pl.pallas_call(
        kern,
        grid=(bn, s // tq),
        in_specs=[
            _const_spec(lam_p.shape), _const_spec(norm_w.shape),
            pl.BlockSpec((1, 1, 256, tq), lambda b, i: (b, i // per, 0, i % per)),
            pl.BlockSpec((1, s, 256), lambda b, i: (b, 0, 0)),
            pl.BlockSpec((1, ns, 256, CH), lambda b, i: (b, 0, 1, 0)),
        ],
        out_specs=pl.BlockSpec((1, tq, 256), lambda b, i: (b, i, 0)),
        out_shape=jax.ShapeDtypeStruct((bn, s, 256), _BF16),
        scratch_shapes=[pltpu.VMEM((256, g * tq), _BF16), pltpu.VMEM((DA_HEADS, CH, 128), _F32),
                        pltpu.VMEM((16, 128), _F32)],
        compiler_params=pltpu.CompilerParams(
            dimension_semantics=("parallel", "arbitrary"), vmem_limit_bytes=V7X_VMEM_LIMIT),
        name="diff_attn",
    )(lam_p, norm_w, t16, n16, t16)


def _bit_planes(rows):
    a = list(rows)
    j, m = 16, 0x0000FFFF
    while j:
        k = 0
        while k < 32:
            t = (a[k] ^ lax.shift_right_logical(a[k + j], jnp.int32(j))) & jnp.int32(m)
            a[k] = a[k] ^ t
            a[k + j] = a[k + j] ^ (t << j)
            k = (k + j + 1) & ~j
        j >>= 1
        if j:
            m = (m ^ (m << j)) & 0xFFFFFFFF
            m = m - (1 << 32) if m >= (1 << 31) else m
    return a[::-1]


def _dsa_kernel(iqt_ref, ik_ref, iwt_ref, qt_ref, k_ref, vt_ref, o_ref,
                iqbd_ref, bd_ref, key_ref, planes_ref, alive_ref, ab_ref, kn_ref, *, tq, slopes, topk):
    g = DSA_HEADS
    q0 = pl.program_id(1) * tq
    nfull = q0 // CH

    @pl.when(pl.program_id(1) == 0)
    def _():
        _key_norm_bound(k_ref, kn_ref, g, k_ref.shape[1] // CH)

    ngrp = (nfull + COUNT_UNROLL) // COUNT_UNROLL
    iqbd_ref[...] = _block_diag_tile(iqt_ref[0, 0], IDX_HEADS)
    bd_ref[...] = _block_diag_tile(qt_ref[0, 0], g)
    ab_ref[...] = _alibi_rows(slopes)
    w = iwt_ref[0, 0][0:IDX_HEADS, :] * (IDX_HEADS ** -0.5 * IDX_HD ** -0.5)

    def logits(j):
        return _dot(ik_ref[0, pl.ds(pl.multiple_of(j * CH, CH), CH), :], iqbd_ref[...])

    def score(lg, j, diag):
        half = CH // 2
        rows = []
        for r0 in (0, half):
            sc = jnp.maximum(lg[r0:r0 + half, 0:tq], 0.0) * w[0:1]
            for h in range(1, IDX_HEADS):
                sc = sc + jnp.maximum(lg[r0:r0 + half, h * tq:(h + 1) * tq], 0.0) * w[h:h + 1]
            sc = jnp.where(sc == 0.0, 0.0, sc)
            bits = pltpu.bitcast(sc, jnp.int32)
            key = bits ^ ((bits >> 31) & 0x7FFFFFFF)
            if diag:
                rel = _iota(key.shape, 0) - _iota(key.shape, 1)
                key = jnp.where(rel <= q0 - j * CH - r0, key, _INT_MIN)
            key_ref[j, r0:r0 + half, :] = key
            key3 = key.reshape(half // 8, 8, tq)
            rows += [key3[v] for v in range(half // 8)]
        planes = _bit_planes(rows)
        planes[31] = ~planes[31]
        planes_ref[j] = jnp.stack(planes, axis=0)
        if diag:
            lim = q0 - j * CH + _iota((8, tq), 1) - _iota((8, tq), 0)
            nbits = jnp.clip((lim >> 3) + 1, 0, 32)
            alive_ref[j] = jnp.where(nbits == 0, 0, jnp.left_shift(jnp.int32(-1), 32 - jnp.maximum(nbits, 1)))
        else:
            alive_ref[j] = jnp.full((8, tq), -1, jnp.int32)

    def score_group(i, c):
        lgs = [logits(SCORE_UNROLL * i + u) for u in range(SCORE_UNROLL)]
        for u in range(SCORE_UNROLL):
            score(lgs[u], SCORE_UNROLL * i + u, False)
        return c

    def score_one(j, c):
        score(logits(j), j, False)
        return c

    ngroup = nfull // SCORE_UNROLL
    lax.fori_loop(0, ngroup, score_group, 0)
    lax.fori_loop(SCORE_UNROLL * ngroup, nfull, score_one, 0)
    score(logits(nfull), nfull, True)

    for u in range(1, COUNT_UNROLL):
        @pl.when(nfull + u < ngrp * COUNT_UNROLL)
        def _():
            planes_ref[nfull + u] = jnp.zeros((32, 8, tq), jnp.int32)
            alive_ref[nfull + u] = jnp.zeros((8, tq), jnp.int32)

    def sweep(b_upd, keep, b_cnt):
        def body(gi, acc8):
            for u in range(COUNT_UNROLL):
                j = gi * COUNT_UNROLL + u
                a = alive_ref[j]
                if b_upd is not None:
                    a = a & ~(planes_ref[j, b_upd] ^ keep)
                    alive_ref[j] = a
                acc8 = acc8 + lax.population_count(a if b_cnt is None else a & planes_ref[j, b_cnt])
            return acc8
        acc8 = lax.fori_loop(0, ngrp, body, jnp.zeros((8, tq), jnp.int32))
        return jnp.sum(acc8, axis=0, keepdims=True)

    def decide(b, ones, want, thr):
        take = ones >= want
        thr = jnp.where(take, thr | jnp.left_shift(jnp.int32(1), b), thr)
        return jnp.where(take, want, want - ones), thr, jnp.where(take, -1, 0)

    want, thr, keep = decide(31, sweep(None, None, 31), jnp.full((1, tq), topk, jnp.int32),
                             jnp.zeros((1, tq), jnp.int32))

    def bit_body(i, c):
        want, thr, keep = c
        b = 30 - i
        return decide(b, sweep(b + 1, keep, b), want, thr)

    want, thr, keep = lax.fori_loop(0, 31, bit_body, (want, thr, keep))
    ties = sweep(0, keep, None)
    thr = thr ^ _INT_MIN
    tie_lane = (ties > want) & (thr > _INT_MIN)

    @pl.when(jnp.max(jnp.where(tie_lane, 1, 0)) > 0)
    def _():
        room = want.astype(_F32)
        lower = jnp.where(_iota((CH, CH), 0) > _iota((CH, CH), 1), 1.0, 0.0).astype(_BF16)

        def demote(j0, n, seen):
            keys = [key_ref[j0 + u] for u in range(n)]
            eqs = [(key == thr) & tie_lane for key in keys]
            eqfs = [jnp.where(eq, 1.0, 0.0) for eq in eqs]
            inside = [_dot(lower, eqf.astype(_BF16)) for eqf in eqfs]
            for u in range(n):
                key_ref[j0 + u] = jnp.where(eqs[u] & (inside[u] + seen >= room), thr - 1, keys[u])
                seen = seen + jnp.sum(eqfs[u], axis=0, keepdims=True)
            return seen

        ngroup = (nfull + 1) // COUNT_UNROLL
        seen = lax.fori_loop(0, ngroup, lambda i, c: demote(COUNT_UNROLL * i, COUNT_UNROLL, c),
                             jnp.zeros((1, tq), _F32))
        lax.fori_loop(COUNT_UNROLL * ngroup, nfull + 1, lambda j, c: demote(j, 1, c), seen)

    thr_eff = jnp.maximum(thr, _INT_MIN + 1)

    def near_body(j, best):
        pos = jnp.where(key_ref[j] >= thr_eff, _iota((CH, tq), 0) + j * CH, -1)
        return jnp.maximum(best, jnp.max(pos, axis=0, keepdims=True))

    best = lax.fori_loop(jnp.maximum(nfull - (NEAR_CHUNKS - 1), 0), nfull + 1, near_body,
                         jnp.full((1, tq), -1, jnp.int32))
    gap = jnp.where(best >= 0, q0 + _iota((1, tq), 1) - best, 2 ** 24)
    bound = _logit_bound(qt_ref[0, 0], kn_ref, g, DSA_HD ** -0.5)
    first = _first_chunks(bound, slopes, q0, nfull, jnp.max(gap, axis=1, keepdims=True))
    ceiling, ceiling_ok = _logit_ceiling(bound, slopes, tq, gap.astype(_F32))
    ceiling_ok = ceiling_ok & (jnp.min(best) >= 0)

    def mask(j, aux, diag):
        return jnp.where(key_ref[j] >= thr_eff, 0.0, NEG_INF), aux

    def attend(ceil):
        l, acc = _attend(k_ref, vt_ref, bd_ref, ab_ref, _slope_row(slopes, 1, tq), DSA_HD ** -0.5 * _LOG2E,
                         q0, nfull, tq, tq, mask, jnp.zeros((1, tq), _F32), first, ceil)
        linv = 1.0 / l
        outs = [acc[h] * linv[:, h * tq:(h + 1) * tq] for h in range(g)]
        o_ref[0] = jnp.concatenate(outs, axis=0).T.astype(_BF16)

    @pl.when(ceiling_ok)
    def _():
        attend(ceiling)

    @pl.when(jnp.logical_not(ceiling_ok))
    def _():
        attend(None)


def _dsa_attention(n16, t16, t32, bn, s, slopes):
    tq = TQ
    ns, per = s // CH, CH // tq
    g = DSA_HEADS
    topk = min(TOPK_MAX, s // 4)
    assert ns % COUNT_UNROLL == 0
    kern = functools.partial(_dsa_kernel, tq=tq, slopes=slopes, topk=topk)
    return pl.pallas_call(
        kern,
        grid=(bn, s // tq),
        in_specs=[
            pl.BlockSpec((1, 1, 128, tq), lambda b, i: (b, i // per, 10, i % per)),
            pl.BlockSpec((1, s, 128), lambda b, i: (b, 0, 10)),
            pl.BlockSpec((1, 1, 16, tq), lambda b, i: (b, i // per, 16, i % per)),
            pl.BlockSpec((1, 1, 256, tq), lambda b, i: (b, i // per, 2, i % per)),
            pl.BlockSpec((1, s, 256), lambda b, i: (b, 0, 1)),
            pl.BlockSpec((1, ns, 256, CH), lambda b, i: (b, 0, 3, 0)),
        ],
        out_specs=pl.BlockSpec((1, tq, 256), lambda b, i: (b, i, 0)),
        out_shape=jax.ShapeDtypeStruct((bn, s, 256), _BF16),
        scratch_shapes=[
            pltpu.VMEM((128, IDX_HEADS * tq), _BF16), pltpu.VMEM((256, g * tq), _BF16),
            pltpu.VMEM((ns, CH, tq), jnp.int32), pltpu.VMEM((ns, 32, 8, tq), jnp.int32),
            pltpu.VMEM((ns, 8, tq), jnp.int32), pltpu.VMEM((DSA_HEADS, CH, 128), _F32),
            pltpu.VMEM((16, 128), _F32),
        ],
        compiler_params=pltpu.CompilerParams(
            dimension_semantics=("parallel", "arbitrary"), vmem_limit_bytes=V7X_VMEM_LIMIT),
        name="dsa_attn",
    )(t16, n16, t32, t16, n16, t16)


def _ret_kernel(q_ref, kt_ref, v_ref, g_ref, intra_ref, qdec_ref, kdect_ref, cd_ref, nw_ref, o_ref, s_ref, *, c):
    @pl.when(pl.program_id(1) == 0)
    def _():
        s_ref[...] = jnp.zeros(s_ref.shape, _F32)

    q = q_ref[0]
    v = v_ref[0]
    kt = kt_ref[0, 0] * (RET_QK ** -0.5)
    att = _dot(q, _block_diag_tile(kt.astype(_BF16), RET_HEADS)) * intra_ref[...]
    vt = jnp.concatenate([v] * RET_HEADS, axis=0)
    vbd = jnp.where((_iota(vt.shape, 0) // c) == (_iota(vt.shape, 1) // RET_V), vt, jnp.zeros_like(vt))
    st = s_ref[...]
    o = _dot(att.astype(_BF16), vbd) + _dot(q, st.astype(_BF16)) * qdec_ref[...]
    upd = _dot((kt * kdect_ref[...]).astype(_BF16), v)
    same_head = (_iota(upd.shape, 0) // RET_QK) == (_iota(upd.shape, 1) // RET_V)
    s_ref[...] = st * cd_ref[...] + jnp.where(same_head, upd, 0.0)

    mu = _group_mean(o, RET_V)
    d = o - mu
    var = _group_mean(d * d, RET_V)
    y = d * lax.rsqrt(var + LN_EPS) * nw_ref[...]
    gate = g_ref[0]
    o_ref[0] = (gate * jax.nn.sigmoid(gate) * y).astype(_BF16)


def _retention_consts(c):
    h = RET_HEADS
    log_g = np.log1p(-np.power(2.0, -5.0 - np.arange(h, dtype=np.float64)))
    pos = np.arange(c, dtype=np.float64)
    rel = pos[:, None] - pos[None, :]
    intra = np.where(rel >= 0, np.exp(log_g[:, None, None] * np.maximum(rel, 0.0)), 0.0)
    intra = np.transpose(intra, (1, 0, 2)).reshape(c, h * c)
    qdec = np.repeat(np.exp(log_g[:, None] * (pos[None, :] + 1.0)).T, RET_V, axis=1)
    kdect = np.repeat(np.exp(log_g[:, None] * (c - 1.0 - pos[None, :])), RET_QK, axis=0)
    cd = np.repeat(np.exp(log_g * c), RET_QK)[:, None] * np.ones((1, h * RET_V))
    return tuple(jnp.asarray(a, _F32) for a in (intra, qdec, kdect, cd))


def _retention(norm_w, n16, n32, t32, bn, s):
    c = RET_C
    per = CH // c
    intra, qdec, kdect, cd = _retention_consts(c)
    return pl.pallas_call(
        functools.partial(_ret_kernel, c=c),
        grid=(bn, s // c),
        in_specs=[
            pl.BlockSpec((1, c, 256), lambda b, i: (b, i, 2)),
            pl.BlockSpec((1, 1, 256, c), lambda b, i: (b, i // per, 0, i % per)),
            pl.BlockSpec((1, c, 256), lambda b, i: (b, i, 3)),
            pl.BlockSpec((1, c, 256), lambda b, i: (b, i, 0)),
            _const_spec(intra.shape), _const_spec(qdec.shape), _const_spec(kdect.shape),
            _const_spec(cd.shape), _const_spec(norm_w.shape),
        ],
        out_specs=pl.BlockSpec((1, c, 256), lambda b, i: (b, i, 0)),
        out_shape=jax.ShapeDtypeStruct((bn, s, 256), _BF16),
        scratch_shapes=[pltpu.VMEM((RET_HEADS * RET_QK, RET_HEADS * RET_V), _F32)],
        compiler_params=pltpu.CompilerParams(dimension_semantics=("parallel", "arbitrary")),
        name="retention",
    )(n16, t32, n16, n32, intra, qdec, kdect, cd, norm_w)


def _gla_kernel(q_ref, k_ref, a_ref, v_ref, vt_ref, g_ref, wa_ref, ba_ref, nw_ref, o_ref,
                st_ref, u_ref, oacc_ref, qs_ref, kk_ref, b_ref, qh_ref, dec_ref, *, ts, sub):
    nsub = ts // sub
    assert sub == 16

    @pl.when(pl.program_id(1) == 0)
    def _():
        st_ref[...] = jnp.zeros(st_ref.shape, _F32)

    la = jax.nn.log_sigmoid(_dot(a_ref[0].astype(_BF16), wa_ref[...]) + ba_ref[...]) * (1.0 / GLA_GATE_TEMP)
    in_blk = _iota(la.shape, 0) % sub
    b = la
    for sh in (1, 2, 4, 8):
        b = b + jnp.where(in_blk >= sh, pltpu.roll(b, sh, 0), 0.0)
    bl = jnp.where(in_blk == sub - 1, b, 0.0)
    for sh in (1, 2, 4, 8):
        bl = bl + jnp.where(in_blk < sub - sh, pltpu.roll(bl, ts - sh, 0), 0.0)
    qs = q_ref[0] * (GLA_QK ** -0.5)
    kk = k_ref[0]
    kd = (kk * jnp.exp(bl - b)).astype(_BF16)
    qs_ref[...] = qs
    kk_ref[...] = kk
    b_ref[...] = b
    qh_ref[...] = (qs * jnp.exp(b)).astype(_BF16)
    dec_ref[...] = jnp.exp(bl)
    vt = vt_ref[0, 0]
    row_blk = _iota(kd.shape, 0) // sub
    kd_wide = jnp.concatenate([jnp.where(row_blk == n, kd, jnp.zeros_like(kd)) for n in range(nsub)], axis=1)
    u_all = _dot(vt, kd_wide)
    for n in range(nsub):
        u_ref[n] = u_all[:, n * 128:(n + 1) * 128]

    st_keep = (_iota(st_ref.shape, 0) // GLA_V) == (_iota(st_ref.shape, 1) // GLA_QK)
    spread = jnp.where((_iota((128, 256), 0) // GLA_QK) == (_iota((128, 256), 1) // GLA_V), 1.0, 0.0).astype(_BF16)
    row16 = _iota((sub, 128), 0)

    def body(n, carry):
        r0 = pl.multiple_of(n * sub, sub)
        st = st_ref[...]
        o_cross = _dot_nt(qh_ref[pl.ds(r0, sub), :], st.astype(_BF16))
        q16 = qs_ref[pl.ds(r0, sub), :]
        k16 = kk_ref[pl.ds(r0, sub), :]
        b16 = b_ref[pl.ds(r0, sub), :]
        v16 = v_ref[0, pl.ds(r0, sub), :].astype(_F32)
        es = []
        for j in range(sub):
            e = q16 * k16[j:j + 1] * jnp.exp(jnp.minimum(b16 - b16[j:j + 1], 0.0))
            es.append(jnp.where(row16 >= j, e, 0.0))
        e_all = jnp.concatenate(es, axis=0)
        e_hi = e_all.astype(_BF16)
        e_lo = (e_all - e_hi.astype(_F32)).astype(_BF16)
        att = _dot(e_hi, spread) + _dot(e_lo, spread)
        o_diag = att[0:sub] * v16[0:1]
        for j in range(1, sub):
            o_diag = o_diag + att[j * sub:(j + 1) * sub] * v16[j:j + 1]
        oacc_ref[pl.ds(r0, sub), :] = o_cross + o_diag
        st_ref[...] = st * dec_ref[pl.ds(r0, 1), :] + jnp.where(st_keep, u_ref[n], 0.0)
        return carry

    lax.fori_loop(0, nsub, body, 0, unroll=4)

    o = oacc_ref[...]
    ms = _group_mean(o * o, GLA_V)
    y = o * lax.rsqrt(ms + LN_EPS) * nw_ref[...]
    gate = g_ref[0]
    o_ref[0] = (gate * jax.nn.sigmoid(gate) * y).astype(_BF16)


def _gla(wa, ba, norm_w, n16, n32, t16, bn, s):
    ts, sub = CH, GLA_SUB
    return pl.pallas_call(
        functools.partial(_gla_kernel, ts=ts, sub=sub),
        grid=(bn, s // ts),
        in_specs=[
            pl.BlockSpec((1, ts, 128), lambda b, i: (b, i, 4)),
            pl.BlockSpec((1, ts, 128), lambda b, i: (b, i, 5)),
            pl.BlockSpec((1, ts, 128), lambda b, i: (b, i, 6)),
            pl.BlockSpec((1, ts, 256), lambda b, i: (b, i, 4)),
            pl.BlockSpec((1, 1, 256, ts), lambda b, i: (b, i, 4, 0)),
            pl.BlockSpec((1, ts, 256), lambda b, i: (b, i, 1)),
            _const_spec(wa.shape), _const_spec(ba.shape), _const_spec(norm_w.shape),
        ],
        out_specs=pl.BlockSpec((1, ts, 256), lambda b, i: (b, i, 0)),
        out_shape=jax.ShapeDtypeStruct((bn, s, 256), _BF16),
        scratch_shapes=[
            pltpu.VMEM((GLA_HEADS * GLA_V, GLA_HEADS * GLA_QK), _F32),
            pltpu.VMEM((ts // sub, GLA_HEADS * GLA_V, GLA_HEADS * GLA_QK), _F32),
            pltpu.VMEM((ts, 256), _F32),
            pltpu.VMEM((ts, 128), _F32), pltpu.VMEM((ts, 128), _F32), pltpu.VMEM((ts, 128), _F32),
            pltpu.VMEM((ts, 128), _BF16), pltpu.VMEM((ts, 128), _F32),
        ],
        compiler_params=pltpu.CompilerParams(dimension_semantics=("parallel", "arbitrary")),
        name="gla",
    )(n32, n32, n32, n16, t16, n32, wa, ba, norm_w)


def _merge_kernel(x_ref, ya_ref, yb_ref, yc_ref, yd_ref, wg_ref, wbr_ref, wout_ref, lnw_ref, lnb_ref, h_ref):
    x = x_ref[...]
    xb = x.astype(_BF16)
    merged = None
    for n, y_ref in enumerate((ya_ref, yb_ref, yc_ref, yd_ref)):
        gate = jax.nn.sigmoid(_dot(xb, wg_ref[:, n * D_MODEL:(n + 1) * D_MODEL]))
        term = gate * _dot(y_ref[...], wbr_ref[n])
        merged = term if merged is None else merged + term
    mix = _dot(merged.astype(_BF16), wout_ref[...])
    h_ref[...] = _layer_norm(DEEPNORM_ALPHA * x + mix, lnw_ref[...], lnb_ref[...])


def _merge(x2, ys, wg, wbr, wout, lnw, lnb):
    t = x2.shape[0]
    ts = TS_DENSE
    tok = lambda w: pl.BlockSpec((ts, w), lambda i: (i, 0))
    return pl.pallas_call(
        _merge_kernel,
        grid=(t // ts,),
        in_specs=[tok(D_MODEL)] + [tok(BRANCH_W)] * 4 + [
            _const_spec(wg.shape), _const_spec(wbr.shape), _const_spec(wout.shape),
            _const_spec(lnw.shape), _const_spec(lnb.shape)],
        out_specs=tok(D_MODEL),
        out_shape=jax.ShapeDtypeStruct((t, D_MODEL), _F32),
        compiler_params=pltpu.CompilerParams(
            dimension_semantics=("parallel",), vmem_limit_bytes=V7X_VMEM_LIMIT),
        name="merge_ln",
    )(x2, *ys, wg, wbr, wout, lnw, lnb)


def _ffn_kernel(h_ref, p_ref, wup_ref, cw_ref, cb_ref, wdn_ref, wpg_ref, wpp_ref, lnw_ref, lnb_ref,
                o_ref, tail_ref, *, ts):
    @pl.when(pl.program_id(1) == 0)
    def _():
        tail_ref[...] = jnp.zeros(tail_ref.shape, _F32)

    h = h_ref[...]
    hb = h.astype(_BF16)
    row = _iota((ts, FF_COLS), 0)
    f = None
    for c0 in range(0, D_FF, FF_COLS):
        cols = slice(c0, c0 + FF_COLS)
        u = _dot(hb, wup_ref[:, cols])
        gt = _dot(hb, wup_ref[:, D_FF + c0:D_FF + c0 + FF_COLS])
        prev = tail_ref[:, cols]
        g1 = jnp.where(row == 0, prev[7:8], pltpu.roll(gt, 1, 0))
        g2 = jnp.where(row == 0, prev[6:7], jnp.where(row == 1, prev[7:8], pltpu.roll(gt, 2, 0)))
        tail_ref[:, cols] = gt[ts - 8:ts]
        gc = cb_ref[:, cols] + cw_ref[0:1, cols] * g2
        gc = gc + cw_ref[1:2, cols] * g1
        gc = gc + cw_ref[2:3, cols] * gt
        term = _dot((jax.nn.gelu(gc) * u).astype(_BF16), wdn_ref[cols, :])
        f = term if f is None else f + term
    e = jax.nn.sigmoid(_dot(hb, wpg_ref[...])) * _dot(p_ref[...].astype(_BF16), wpp_ref[...])
    o_ref[...] = _layer_norm(DEEPNORM_ALPHA * h + f + e, lnw_ref[...], lnb_ref[...])


def _ffn(h2, p2, wup, cw, cb, wdn, wpg, wpp, lnw, lnb, bn, s):
    ts = TS_DENSE
    ns = s // ts
    tok = lambda w: pl.BlockSpec((ts, w), lambda b, i: (b * ns + i, 0))
    return pl.pallas_call(
        functools.partial(_ffn_kernel, ts=ts),
        grid=(bn, ns),
        in_specs=[tok(D_MODEL), tok(P_DIM)] + [_const_spec(a.shape) for a in (wup, cw, cb, wdn, wpg, wpp, lnw, lnb)],
        out_specs=tok(D_MODEL),
        out_shape=jax.ShapeDtypeStruct((bn * s, D_MODEL), _F32),
        scratch_shapes=[pltpu.VMEM((8, D_FF), _F32)],
        compiler_params=pltpu.CompilerParams(
            dimension_semantics=("parallel", "arbitrary"), vmem_limit_bytes=V7X_VMEM_LIMIT),
        name="ffn_ple_ln",
    )(h2, p2, wup, cw, cb, wdn, wpg, wpp, lnw, lnb)


def kernel(x, p, w_in, a_lambda, a_norm_w, ret_norm_w, gla_w_a2, gla_b_a, gla_norm_w, w_branch, w_out,
           ln1_w, ln1_b, w_ffn_up, ffn_conv_w, ffn_conv_b, w_ffn_down, w_ple_gate, w_ple_proj, ln2_w, ln2_b):
    bn, s, _ = x.shape
    t = bn * s
    slopes = [2.0 ** (-(8.0 / N_SOFTMAX_HEADS) * i) for i in range(1, N_SOFTMAX_HEADS + 1)]
    slopes_a, slopes_b = tuple(slopes[0::2]), tuple(slopes[1::2])
    row = lambda v: v.astype(_F32).reshape(1, -1)
    x2 = x.reshape(t, D_MODEL)
    for i in range(DEPTH):
        *proj_w, wg = _prep_weights(w_in, i)
        n32, n16, t16, t32 = _project(x2, *proj_w, bn, s)
        n32 = n32.reshape(bn, s, -1)
        n16 = n16.reshape(bn, s, -1)
        lam_init = 0.8 - 0.6 * math.exp(-0.3 * i)
        y_a = _diff_attention(a_lambda[i].astype(_F32), a_norm_w[i].astype(_F32).reshape(DA_V, 1),
                              n16, t16, bn, s, slopes_a, lam_init)
        y_b = _dsa_attention(n16, t16, t32, bn, s, slopes_b)
        y_c = _retention(row(ret_norm_w[i]), n16, n32, t32, bn, s)
        wa = jnp.pad(gla_w_a2[i], ((0, 128 - GLA_RANK), (0, 0))).astype(_BF16)
        y_d = _gla(wa, row(gla_b_a[i]), row(jnp.tile(gla_norm_w[i], GLA_HEADS)), n16, n32, t16, bn, s)
        ys = [y.reshape(t, BRANCH_W) for y in (y_a, y_b, y_c, y_d)]
        h2 = _merge(x2, ys, wg, w_branch[i].astype(_BF16), w_out[i].astype(_BF16), row(ln1_w[i]), row(ln1_b[i]))
        x2 = _ffn(h2, p[i].reshape(t, P_DIM), w_ffn_up[i].astype(_BF16), ffn_conv_w[i].astype(_F32),
                  row(ffn_conv_b[i]), w_ffn_down[i].astype(_BF16), w_ple_gate[i].astype(_BF16),
                  w_ple_proj[i].astype(_BF16), row(ln2_w[i]), row(ln2_b[i]), bn, s)
    return x2.reshape(bn, s, D_MODEL)
```

```python
import functools
import math

import numpy as np
import jax
import jax.numpy as jnp
from jax import lax
from jax.experimental import pallas as pl
from jax.experimental.pallas import tpu as pltpu

D_MODEL = 1024
DEPTH = 2
P_DIM = 256
N_BRANCH = 4
BRANCH_W = 256
DA_HEADS = 4
DA_QK = 32
DA_V = 64
DSA_HEADS = 4
DSA_HD = 64
IDX_HEADS = 4
IDX_HD = 32
TOPK_MAX = 256
RET_HEADS = 4
RET_QK = 64
RET_V = 64
GLA_HEADS = 4
GLA_QK = 32
GLA_V = 64
GLA_RANK = 16
GLA_GATE_TEMP = 16.0
D_FF = 2816
CONV_W = 3
N_SOFTMAX_HEADS = DA_HEADS + DSA_HEADS
LN_EPS = 1e-5
NEG_INF = -1e30
DEEPNORM_ALPHA = (2.0 * DEPTH) ** 0.25

IN_SIZES = (256, 256, 256, 256, 256, 256, 128, 32, 4, 256, 256, 256, 256, 128, 128, 256, 16, 256, 4096)
IN_NAMES = ("a_q", "a_k", "a_v", "b_q", "b_k", "b_v", "b_iq", "b_ik", "b_iw",
            "c_q", "c_k", "c_v", "c_g", "d_q", "d_k", "d_v", "d_a", "d_g", "m_g")

_BF16 = jnp.bfloat16
_F32 = jnp.float32
_INT_MIN = -2 ** 31
_LOG2E = math.log2(math.e)

CH = 256
TQ = 256
RET_C = 256
GLA_SUB = 16
COUNT_UNROLL = 8
SCORE_UNROLL = 4
ATTEND_UNROLL = 4
TS_DENSE = 512
FF_COLS = 1408
PREP_ROWS = 256
NEAR_CHUNKS = 4
SKIP_NATS = 32.0
FIXED_MAX_BITS = 100.0
V7X_VMEM_LIMIT = 56 * 1024 * 1024


def _dot(a, b, precision=None):
    return jnp.dot(a, b, preferred_element_type=_F32, precision=precision)


def _dot_nt(a, b):
    return lax.dot_general(a, b, (((1,), (1,)), ((), ())), preferred_element_type=_F32)


def _iota(shape, dim):
    return lax.broadcasted_iota(jnp.int32, shape, dim)


def _block_diag_tile(m_t, ngroups):
    r, tq = m_t.shape
    tiled = jnp.concatenate([m_t] * ngroups, axis=1)
    keep = (_iota(tiled.shape, 0) // (r // ngroups)) == (_iota(tiled.shape, 1) // tq)
    return jnp.where(keep, tiled, jnp.zeros_like(tiled))


def _layer_norm(x, w, b):
    mu = jnp.mean(x, -1, keepdims=True)
    var = jnp.mean(jnp.square(x - mu), -1, keepdims=True)
    return (x - mu) * lax.rsqrt(var + LN_EPS) * w + b


def _group_mean(x, group):
    lane_g = _iota(x.shape, 1) // group
    out = jnp.zeros_like(x)
    for h in range(x.shape[1] // group):
        mk = lane_g == h
        mh = jnp.sum(jnp.where(mk, x, 0.0), axis=1, keepdims=True) * (1.0 / group)
        out = jnp.where(mk, mh, out)
    return out


def _const_spec(shape):
    nd = len(shape)
    return pl.BlockSpec(shape, lambda *_: (0,) * nd, pipeline_mode=pl.Buffered(1))


def _prep_kernel(w_ref, wn32_ref, wn16_ref, wt16_ref, wt32_ref, wg_ref):
    offs = [0] + np.cumsum(IN_SIZES).tolist()
    col = {n: w_ref[0, :, offs[i]:offs[i + 1]] for i, n in enumerate(IN_NAMES)}
    zeros = lambda n: jnp.zeros((w_ref.shape[1], n), _F32)
    cat = lambda xs: jnp.concatenate(xs, axis=1)
    wn32_ref[...] = cat([col["c_g"], col["d_g"], col["d_q"], col["d_k"], col["d_a"],
                         zeros(128 - GLA_RANK)]).astype(_BF16)
    wn16_ref[...] = cat([col["a_k"], col["b_k"], col["c_q"], col["c_v"], col["d_v"]]
                        + [col["b_ik"]] * IDX_HEADS).astype(_BF16)
    wt16_ref[...] = cat([col["a_q"], col["a_v"], col["b_q"], col["b_v"], col["d_v"], col["b_iq"]]).T.astype(_BF16)
    wt32_ref[...] = cat([col["c_k"], col["b_iw"], zeros(16 - IDX_HEADS)]).T.astype(_BF16)
    wg_ref[...] = col["m_g"].astype(_BF16)


def _prep_weights(w_in, layer):
    rb = PREP_ROWS
    n32, n16, t16, t32, ng = 896, 1408, 1408, 272, N_BRANCH * D_MODEL
    rows = lambda n: pl.BlockSpec((rb, n), lambda r: (r, 0))
    cols = lambda n: pl.BlockSpec((n, rb), lambda r: (0, r))
    return pl.pallas_call(
        _prep_kernel,
        grid=(D_MODEL // rb,),
        in_specs=[pl.BlockSpec((1, rb, w_in.shape[2]), lambda r: (layer, r, 0))],
        out_specs=[rows(n32), rows(n16), cols(t16), cols(t32), rows(ng)],
        out_shape=[jax.ShapeDtypeStruct((D_MODEL, n32), _BF16), jax.ShapeDtypeStruct((D_MODEL, n16), _BF16),
                   jax.ShapeDtypeStruct((t16, D_MODEL), _BF16), jax.ShapeDtypeStruct((t32, D_MODEL), _BF16),
                   jax.ShapeDtypeStruct((D_MODEL, ng), _BF16)],
        compiler_params=pltpu.CompilerParams(
            dimension_semantics=("parallel",), vmem_limit_bytes=V7X_VMEM_LIMIT),
        name="prep_weights",
    )(w_in)


def _proj_kernel(x_ref, wn32_ref, wn16_ref, wt16_ref, wt32_ref, n32_ref, n16_ref, t16_ref, t32_ref):
    x = x_ref[...].astype(_BF16)
    n32_ref[...] = _dot(x, wn32_ref[...])
    n16_ref[...] = _dot(x, wn16_ref[...]).astype(_BF16)
    t16_ref[0, 0] = _dot_nt(wt16_ref[...], x).astype(_BF16)
    t32_ref[0, 0] = _dot_nt(wt32_ref[...], x)


def _project(x2, wn32, wn16, wt16, wt32, bn, s):
    ns = s // CH
    n32, n16, t16, t32 = wn32.shape[1], wn16.shape[1], wt16.shape[0], wt32.shape[0]
    return pl.pallas_call(
        _proj_kernel,
        grid=(bn, ns),
        in_specs=[
            pl.BlockSpec((CH, D_MODEL), lambda b, i: (b * ns + i, 0)),
            _const_spec(wn32.shape), _const_spec(wn16.shape), _const_spec(wt16.shape), _const_spec(wt32.shape),
        ],
        out_specs=[
            pl.BlockSpec((CH, n32), lambda b, i: (b * ns + i, 0)),
            pl.BlockSpec((CH, n16), lambda b, i: (b * ns + i, 0)),
            pl.BlockSpec((1, 1, t16, CH), lambda b, i: (b, i, 0, 0)),
            pl.BlockSpec((1, 1, t32, CH), lambda b, i: (b, i, 0, 0)),
        ],
        out_shape=[
            jax.ShapeDtypeStruct((bn * s, n32), _F32),
            jax.ShapeDtypeStruct((bn * s, n16), _BF16),
            jax.ShapeDtypeStruct((bn, ns, t16, CH), _BF16),
            jax.ShapeDtypeStruct((bn, ns, t32, CH), _F32),
        ],
        compiler_params=pltpu.CompilerParams(
            dimension_semantics=("parallel", "parallel"), vmem_limit_bytes=V7X_VMEM_LIMIT),
        name="proj",
    )(x2, wn32, wn16, wt16, wt32)


def _slope_row(slopes, reps, tq):
    return jnp.concatenate([jnp.full((1, tq), s * _LOG2E, _F32) for s in slopes for _ in range(reps)], axis=1)


def _alibi_rows(slopes):
    rows = _iota((CH, 128), 0).astype(_F32)
    return jnp.stack([rows * (s * _LOG2E) for s in slopes], axis=0)


def _key_norm_bound(k_ref, kn_ref, ngroups, nchunks):
    gt = jnp.where(_iota((16, 256), 0) == _iota((16, 256), 1) // (256 // ngroups), 1.0, 0.0).astype(_BF16)

    def body(j, best):
        kc = k_ref[0, pl.ds(pl.multiple_of(j * CH, CH), CH), :].astype(_F32)
        return jnp.maximum(best, _dot_nt(gt, (kc * kc).astype(_BF16)))

    best = lax.fori_loop(0, nchunks, body, jnp.zeros((16, CH), _F32))
    kn_ref[...] = jnp.broadcast_to(jnp.sqrt(jnp.max(best, axis=1, keepdims=True) * (1.0 + 2.0 ** -7)), kn_ref.shape)


def _logit_bound(qt, kn_ref, ngroups, scale):
    q = qt.astype(_F32)
    tq = q.shape[1]
    qn = jnp.sqrt(jnp.max(jnp.sum((q * q).reshape(ngroups, 256 // ngroups, tq), axis=1), axis=1, keepdims=True))
    return scale * qn * kn_ref[0:ngroups, 0:1]


def _first_chunks(bound, slopes, q0, nfull, nearest=0):
    per_head = bound.shape[0] // len(slopes)
    inv_slope = jnp.concatenate([jnp.full((per_head, 1), 1.0 / s, _F32) for s in slopes], axis=0)
    reach = ((2.0 * bound + SKIP_NATS) * inv_slope).astype(jnp.int32) + 2 + nearest
    last_far = q0 - CH + 1 - reach
    first = jnp.where(last_far < 0, 0, last_far // CH + 1)
    first = jnp.minimum(first, nfull)
    return [jnp.min(first[h * per_head:(h + 1) * per_head]) for h in range(len(slopes))]


def _logit_ceiling(bound, slopes, tq, nearest=None):
    per_head = bound.shape[0] // len(slopes)
    lane = _iota((1, tq), 1).astype(_F32)
    if nearest is not None:
        lane = lane - nearest
    rows = [bound[g:g + 1, :] * _LOG2E + (slopes[g // per_head] * _LOG2E) * lane for g in range(bound.shape[0])]
    deep = jnp.max(jnp.where(2.0 * _LOG2E * bound > FIXED_MAX_BITS, 1, 0)) > 0
    return jnp.concatenate(rows, axis=1), jnp.logical_not(deep)


def _attend(k_ref, vt_ref, bd_ref, ab_ref, slope_row, c1, q0, nfull, tq, w, mask_fn, aux0, first, ceiling=None):
    g_tq = bd_ref.shape[1]
    nheads = 4

    def pv(j, p, heads):
        vt_c = vt_ref[0, j]
        return {h: _dot(vt_c[h * 64:(h + 1) * 64, :], p[:, h * w:(h + 1) * w]) for h in heads}

    def qk(j, heads):
        kc = k_ref[0, pl.ds(pl.multiple_of(j * CH, CH), CH), :]
        tiles = sorted({c0 // 256 for h in heads for c0 in range(h * w, (h + 1) * w, 128)})
        return {t: _dot(kc, bd_ref[:, t * 256:(t + 1) * 256]) for t in tiles}

    def softmax(s, j, m, l, aux, heads, diag):
        crow = slope_row * (j * CH - q0).astype(_F32)
        amask, aux = mask_fn(j, aux, diag)
        ps, ms, ls, alphas = [], [], [], []
        for c0 in range(0, g_tq, 128):
            cols = slice(c0, c0 + 128)
            if c0 // w not in heads:
                ps.append(jnp.zeros((CH, 128), _BF16))
                ms.append(m[:, cols])
                ls.append(l[:, cols])
                alphas.append(jnp.ones((1, 128), _F32))
                continue
            t = s[c0 // 256][:, c0 % 256:c0 % 256 + 128] * c1 + ab_ref[c0 // w]
            if amask is not None:
                t = t + amask[:, c0 % tq:c0 % tq + 128]
            if ceiling is not None:
                p = jnp.exp2(t + (crow[:, cols] - ceiling[:, cols]))
                m_new, alpha = m[:, cols], jnp.ones((1, 128), _F32)
            else:
                m_new = jnp.maximum(m[:, cols], jnp.max(t, axis=0, keepdims=True) + crow[:, cols])
                alpha = jnp.exp2(m[:, cols] - m_new)
                p = jnp.exp2(t - (m_new - crow[:, cols]))
            ls.append(alpha * l[:, cols] + jnp.sum(p, axis=0, keepdims=True))
            ps.append(p.astype(_BF16))
            ms.append(m_new)
            alphas.append(alpha)
        cat = lambda xs: jnp.concatenate(xs, axis=1)
        return cat(ms), cat(ls), cat(alphas), cat(ps), aux

    def step(s, j, carry, heads, diag=False):
        m, l, acc, aux = carry
        m, l, alpha, p, aux = softmax(s, j, m, l, aux, heads, diag)
        pvs = pv(j, p, heads)
        if ceiling is not None:
            acc = [acc[h] + pvs[h] if h in heads else acc[h] for h in range(nheads)]
        else:
            acc = [alpha[:, h * w:(h + 1) * w] * acc[h] + pvs[h] if h in heads else acc[h] for h in range(nheads)]
        return m, l, acc, aux

    def run(lo, hi, heads, carry):
        unroll = ATTEND_UNROLL

        def group(i, c):
            j = lo + unroll * i
            ss = [qk(j + u, heads) for u in range(unroll)]
            for u in range(unroll):
                c = step(ss[u], j + u, c, heads)
            return c

        ngroup = jnp.maximum(hi - lo, 0) // unroll
        carry = lax.fori_loop(0, ngroup, group, carry)
        return lax.fori_loop(lo + unroll * ngroup, hi, lambda j, c: step(qk(j, heads), j, c, heads), carry)

    carry = (jnp.full((1, g_tq), NEG_INF, _F32), jnp.zeros((1, g_tq), _F32),
             [jnp.zeros((64, w), _F32) for _ in range(nheads)], aux0)
    lows, hi = [], nfull
    for h in range(nheads):
        u = ATTEND_UNROLL
        lo = jnp.maximum(hi - u * ((hi - jnp.minimum(first[h], hi) + u - 1) // u), 0)
        lows.append(lo)
        hi = lo
    for h in reversed(range(nheads)):
        carry = run(lows[h], lows[h - 1] if h else nfull, tuple(range(h, nheads)), carry)
    heads = tuple(range(nheads))
    m, l, acc, aux = step(qk(nfull, heads), nfull, carry, heads, True)
    return l, acc


def _attn_a_kernel(lam_ref, nw_ref, qt_ref, k_ref, vt_ref, o_ref, bd_ref, ab_ref, kn_ref, *, tq, slopes, lam_init):
    g = 2 * DA_HEADS
    q0 = pl.program_id(1) * tq
    nfull = q0 // CH

    @pl.when(pl.program_id(1) == 0)
    def _():
        _key_norm_bound(k_ref, kn_ref, g, k_ref.shape[1] // CH)

    bd_ref[...] = _block_diag_tile(qt_ref[0, 0], g)
    ab_ref[...] = _alibi_rows(slopes)

    def mask(j, aux, diag):
        if not diag:
            return None, aux
        rel = _iota((CH, tq), 0) - _iota((CH, tq), 1)
        return jnp.where(rel <= q0 - j * CH, 0.0, NEG_INF), aux

    bound = _logit_bound(qt_ref[0, 0], kn_ref, g, DA_QK ** -0.5)
    first = _first_chunks(bound, slopes, q0, nfull)
    ceiling, ceiling_ok = _logit_ceiling(bound, slopes, tq)

    def attend(ceil):
        l, acc = _attend(k_ref, vt_ref, bd_ref, ab_ref, _slope_row(slopes, 2, tq), DA_QK ** -0.5 * _LOG2E,
                         q0, nfull, tq, 2 * tq, mask, jnp.zeros((1, tq), _F32), first, ceil)
        lp = lam_ref[...]
        lam = (jnp.exp(jnp.sum(lp[0:1] * lp[1:2], axis=1, keepdims=True))
               - jnp.exp(jnp.sum(lp[2:3] * lp[3:4], axis=1, keepdims=True)) + lam_init)
        linv = 1.0 / l
        outs = []
        for h in range(DA_HEADS):
            a = acc[h] * linv[:, h * 2 * tq:(h + 1) * 2 * tq]
            o = a[:, :tq] - lam * a[:, tq:]
            ms = jnp.mean(o * o, axis=0, keepdims=True)
            outs.append(o * lax.rsqrt(ms + LN_EPS) * nw_ref[...] * (1.0 - lam_init))
        o_ref[0] = jnp.concatenate(outs, axis=0).T.astype(_BF16)

    @pl.when(ceiling_ok)
    def _():
        attend(ceiling)

    @pl.when(jnp.logical_not(ceiling_ok))
    def _():
        attend(None)


def _diff_attention(lam_p, norm_w, n16, t16, bn, s, slopes, lam_init):
    tq = TQ
    ns, per = s // CH, CH // tq
    g = 2 * DA_HEADS
    kern = functools.partial(_attn_a_kernel, tq=tq, slopes=slopes, lam_init=lam_init)
    return pl.pallas_call(
        kern,
        grid=(bn, s // tq),
        in_specs=[
            _const_spec(lam_p.shape), _const_spec(norm_w.shape),
            pl.BlockSpec((1, 1, 256, tq), lambda b, i: (b, i // per, 0, i % per)),
            pl.BlockSpec((1, s, 256), lambda b, i: (b, 0, 0)),
            pl.BlockSpec((1, ns, 256, CH), lambda b, i: (b, 0, 1, 0)),
        ],
        out_specs=pl.BlockSpec((1, tq, 256), lambda b, i: (b, i, 0)),
        out_shape=jax.ShapeDtypeStruct((bn, s, 256), _BF16),
        scratch_shapes=[pltpu.VMEM((256, g * tq), _BF16), pltpu.VMEM((DA_HEADS, CH, 128), _F32),
                        pltpu.VMEM((16, 128), _F32)],
        compiler_params=pltpu.CompilerParams(
            dimension_semantics=("parallel", "arbitrary"), vmem_limit_bytes=V7X_VMEM_LIMIT),
        name="diff_attn",
    )(lam_p, norm_w, t16, n16, t16)


def _bit_planes(rows):
    a = list(rows)
    j, m = 16, 0x0000FFFF
    while j:
        k = 0
        while k < 32:
            t = (a[k] ^ lax.shift_right_logical(a[k + j], jnp.int32(j))) & jnp.int32(m)
            a[k] = a[k] ^ t
            a[k + j] = a[k + j] ^ (t << j)
            k = (k + j + 1) & ~j
        j >>= 1
        if j:
            m = (m ^ (m << j)) & 0xFFFFFFFF
            m = m - (1 << 32) if m >= (1 << 31) else m
    return a[::-1]


def _dsa_kernel(iqt_ref, ik_ref, iwt_ref, qt_ref, k_ref, vt_ref, o_ref,
                iqbd_ref, bd_ref, key_ref, planes_ref, alive_ref, ab_ref, kn_ref, *, tq, slopes, topk):
    g = DSA_HEADS
    q0 = pl.program_id(1) * tq
    nfull = q0 // CH

    @pl.when(pl.program_id(1) == 0)
    def _():
        _key_norm_bound(k_ref, kn_ref, g, k_ref.shape[1] // CH)

    ngrp = (nfull + COUNT_UNROLL) // COUNT_UNROLL
    iqbd_ref[...] = _block_diag_tile(iqt_ref[0, 0], IDX_HEADS)
    bd_ref[...] = _block_diag_tile(qt_ref[0, 0], g)
    ab_ref[...] = _alibi_rows(slopes)
    w = iwt_ref[0, 0][0:IDX_HEADS, :] * (IDX_HEADS ** -0.5 * IDX_HD ** -0.5)

    def logits(j):
        return _dot(ik_ref[0, pl.ds(pl.multiple_of(j * CH, CH), CH), :], iqbd_ref[...])

    def score(lg, j, diag):
        half = CH // 2
        rows = []
        for r0 in (0, half):
            sc = jnp.maximum(lg[r0:r0 + half, 0:tq], 0.0) * w[0:1]
            for h in range(1, IDX_HEADS):
                sc = sc + jnp.maximum(lg[r0:r0 + half, h * tq:(h + 1) * tq], 0.0) * w[h:h + 1]
            sc = jnp.where(sc == 0.0, 0.0, sc)
            bits = pltpu.bitcast(sc, jnp.int32)
            key = bits ^ ((bits >> 31) & 0x7FFFFFFF)
            if diag:
                rel = _iota(key.shape, 0) - _iota(key.shape, 1)
                key = jnp.where(rel <= q0 - j * CH - r0, key, _INT_MIN)
            key_ref[j, r0:r0 + half, :] = key
            key3 = key.reshape(half // 8, 8, tq)
            rows += [key3[v] for v in range(half // 8)]
        planes = _bit_planes(rows)
        planes[31] = ~planes[31]
        planes_ref[j] = jnp.stack(planes, axis=0)
        if diag:
            lim = q0 - j * CH + _iota((8, tq), 1) - _iota((8, tq), 0)
            nbits = jnp.clip((lim >> 3) + 1, 0, 32)
            alive_ref[j] = jnp.where(nbits == 0, 0, jnp.left_shift(jnp.int32(-1), 32 - jnp.maximum(nbits, 1)))
        else:
            alive_ref[j] = jnp.full((8, tq), -1, jnp.int32)

    def score_group(i, c):
        lgs = [logits(SCORE_UNROLL * i + u) for u in range(SCORE_UNROLL)]
        for u in range(SCORE_UNROLL):
            score(lgs[u], SCORE_UNROLL * i + u, False)
        return c

    def score_one(j, c):
        score(logits(j), j, False)
        return c

    ngroup = nfull // SCORE_UNROLL
    lax.fori_loop(0, ngroup, score_group, 0)
    lax.fori_loop(SCORE_UNROLL * ngroup, nfull, score_one, 0)
    score(logits(nfull), nfull, True)

    for u in range(1, COUNT_UNROLL):
        @pl.when(nfull + u < ngrp * COUNT_UNROLL)
        def _():
            planes_ref[nfull + u] = jnp.zeros((32, 8, tq), jnp.int32)
            alive_ref[nfull + u] = jnp.zeros((8, tq), jnp.int32)

    def sweep(b_upd, keep, b_cnt):
        def body(gi, acc8):
            for u in range(COUNT_UNROLL):
                j = gi * COUNT_UNROLL + u
                a = alive_ref[j]
                if b_upd is not None:
                    a = a & ~(planes_ref[j, b_upd] ^ keep)
                    alive_ref[j] = a
                acc8 = acc8 + lax.population_count(a if b_cnt is None else a & planes_ref[j, b_cnt])
            return acc8
        acc8 = lax.fori_loop(0, ngrp, body, jnp.zeros((8, tq), jnp.int32))
        return jnp.sum(acc8, axis=0, keepdims=True)

    def decide(b, ones, want, thr):
        take = ones >= want
        thr = jnp.where(take, thr | jnp.left_shift(jnp.int32(1), b), thr)
        return jnp.where(take, want, want - ones), thr, jnp.where(take, -1, 0)

    want, thr, keep = decide(31, sweep(None, None, 31), jnp.full((1, tq), topk, jnp.int32),
                             jnp.zeros((1, tq), jnp.int32))

    def bit_body(i, c):
        want, thr, keep = c
        b = 30 - i
        return decide(b, sweep(b + 1, keep, b), want, thr)

    want, thr, keep = lax.fori_loop(0, 31, bit_body, (want, thr, keep))
    ties = sweep(0, keep, None)
    thr = thr ^ _INT_MIN
    tie_lane = (ties > want) & (thr > _INT_MIN)

    @pl.when(jnp.max(jnp.where(tie_lane, 1, 0)) > 0)
    def _():
        room = want.astype(_F32)
        lower = jnp.where(_iota((CH, CH), 0) > _iota((CH, CH), 1), 1.0, 0.0).astype(_BF16)

        def demote(j0, n, seen):
            keys = [key_ref[j0 + u] for u in range(n)]
            eqs = [(key == thr) & tie_lane for key in keys]
            eqfs = [jnp.where(eq, 1.0, 0.0) for eq in eqs]
            inside = [_dot(lower, eqf.astype(_BF16)) for eqf in eqfs]
            for u in range(n):
                key_ref[j0 + u] = jnp.where(eqs[u] & (inside[u] + seen >= room), thr - 1, keys[u])
                seen = seen + jnp.sum(eqfs[u], axis=0, keepdims=True)
            return seen

        ngroup = (nfull + 1) // COUNT_UNROLL
        seen = lax.fori_loop(0, ngroup, lambda i, c: demote(COUNT_UNROLL * i, COUNT_UNROLL, c),
                             jnp.zeros((1, tq), _F32))
        lax.fori_loop(COUNT_UNROLL * ngroup, nfull + 1, lambda j, c: demote(j, 1, c), seen)

    thr_eff = jnp.maximum(thr, _INT_MIN + 1)

    def near_body(j, best):
        pos = jnp.where(key_ref[j] >= thr_eff, _iota((CH, tq), 0) + j * CH, -1)
        return jnp.maximum(best, jnp.max(pos, axis=0, keepdims=True))

    best = lax.fori_loop(jnp.maximum(nfull - (NEAR_CHUNKS - 1), 0), nfull + 1, near_body,
                         jnp.full((1, tq), -1, jnp.int32))
    gap = jnp.where(best >= 0, q0 + _iota((1, tq), 1) - best, 2 ** 24)
    bound = _logit_bound(qt_ref[0, 0], kn_ref, g, DSA_HD ** -0.5)
    first = _first_chunks(bound, slopes, q0, nfull, jnp.max(gap, axis=1, keepdims=True))
    ceiling, ceiling_ok = _logit_ceiling(bound, slopes, tq, gap.astype(_F32))
    ceiling_ok = ceiling_ok & (jnp.min(best) >= 0)

    def mask(j, aux, diag):
        return jnp.where(key_ref[j] >= thr_eff, 0.0, NEG_INF), aux

    def attend(ceil):
        l, acc = _attend(k_ref, vt_ref, bd_ref, ab_ref, _slope_row(slopes, 1, tq), DSA_HD ** -0.5 * _LOG2E,
                         q0, nfull, tq, tq, mask, jnp.zeros((1, tq), _F32), first, ceil)
        linv = 1.0 / l
        outs = [acc[h] * linv[:, h * tq:(h + 1) * tq] for h in range(g)]
        o_ref[0] = jnp.concatenate(outs, axis=0).T.astype(_BF16)

    @pl.when(ceiling_ok)
    def _():
        attend(ceiling)

    @pl.when(jnp.logical_not(ceiling_ok))
    def _():
        attend(None)


def _dsa_attention(n16, t16, t32, bn, s, slopes):
    tq = TQ
    ns, per = s // CH, CH // tq
    g = DSA_HEADS
    topk = min(TOPK_MAX, s // 4)
    assert ns % COUNT_UNROLL == 0
    kern = functools.partial(_dsa_kernel, tq=tq, slopes=slopes, topk=topk)
    return pl.pallas_call(
        kern,
        grid=(bn, s // tq),
        in_specs=[
            pl.BlockSpec((1, 1, 128, tq), lambda b, i: (b, i // per, 10, i % per)),
            pl.BlockSpec((1, s, 128), lambda b, i: (b, 0, 10)),
            pl.BlockSpec((1, 1, 16, tq), lambda b, i: (b, i // per, 16, i % per)),
            pl.BlockSpec((1, 1, 256, tq), lambda b, i: (b, i // per, 2, i % per)),
            pl.BlockSpec((1, s, 256), lambda b, i: (b, 0, 1)),
            pl.BlockSpec((1, ns, 256, CH), lambda b, i: (b, 0, 3, 0)),
        ],
        out_specs=pl.BlockSpec((1, tq, 256), lambda b, i: (b, i, 0)),
        out_shape=jax.ShapeDtypeStruct((bn, s, 256), _BF16),
        scratch_shapes=[
            pltpu.VMEM((128, IDX_HEADS * tq), _BF16), pltpu.VMEM((256, g * tq), _BF16),
            pltpu.VMEM((ns, CH, tq), jnp.int32), pltpu.VMEM((ns, 32, 8, tq), jnp.int32),
            pltpu.VMEM((ns, 8, tq), jnp.int32), pltpu.VMEM((DSA_HEADS, CH, 128), _F32),
            pltpu.VMEM((16, 128), _F32),
        ],
        compiler_params=pltpu.CompilerParams(
            dimension_semantics=("parallel", "arbitrary"), vmem_limit_bytes=V7X_VMEM_LIMIT),
        name="dsa_attn",
    )(t16, n16, t32, t16, n16, t16)


def _ret_kernel(q_ref, kt_ref, v_ref, g_ref, intra_ref, qdec_ref, kdect_ref, cd_ref, nw_ref, o_ref, s_ref, *, c):
    @pl.when(pl.program_id(1) == 0)
    def _():
        s_ref[...] = jnp.zeros(s_ref.shape, _F32)

    q = q_ref[0]
    v = v_ref[0]
    kt = kt_ref[0, 0] * (RET_QK ** -0.5)
    att = _dot(q, _block_diag_tile(kt.astype(_BF16), RET_HEADS)) * intra_ref[...]
    vt = jnp.concatenate([v] * RET_HEADS, axis=0)
    vbd = jnp.where((_iota(vt.shape, 0) // c) == (_iota(vt.shape, 1) // RET_V), vt, jnp.zeros_like(vt))
    st = s_ref[...]
    o = _dot(att.astype(_BF16), vbd) + _dot(q, st.astype(_BF16)) * qdec_ref[...]
    upd = _dot((kt * kdect_ref[...]).astype(_BF16), v)
    same_head = (_iota(upd.shape, 0) // RET_QK) == (_iota(upd.shape, 1) // RET_V)
    s_ref[...] = st * cd_ref[...] + jnp.where(same_head, upd, 0.0)

    mu = _group_mean(o, RET_V)
    d = o - mu
    var = _group_mean(d * d, RET_V)
    y = d * lax.rsqrt(var + LN_EPS) * nw_ref[...]
    gate = g_ref[0]
    o_ref[0] = (gate * jax.nn.sigmoid(gate) * y).astype(_BF16)


def _retention_consts(c):
    h = RET_HEADS
    log_g = np.log1p(-np.power(2.0, -5.0 - np.arange(h, dtype=np.float64)))
    pos = np.arange(c, dtype=np.float64)
    rel = pos[:, None] - pos[None, :]
    intra = np.where(rel >= 0, np.exp(log_g[:, None, None] * np.maximum(rel, 0.0)), 0.0)
    intra = np.transpose(intra, (1, 0, 2)).reshape(c, h * c)
    qdec = np.repeat(np.exp(log_g[:, None] * (pos[None, :] + 1.0)).T, RET_V, axis=1)
    kdect = np.repeat(np.exp(log_g[:, None] * (c - 1.0 - pos[None, :])), RET_QK, axis=0)
    cd = np.repeat(np.exp(log_g * c), RET_QK)[:, None] * np.ones((1, h * RET_V))
    return tuple(jnp.asarray(a, _F32) for a in (intra, qdec, kdect, cd))


def _retention(norm_w, n16, n32, t32, bn, s):
    c = RET_C
    per = CH // c
    intra, qdec, kdect, cd = _retention_consts(c)
    return pl.pallas_call(
        functools.partial(_ret_kernel, c=c),
        grid=(bn, s // c),
        in_specs=[
            pl.BlockSpec((1, c, 256), lambda b, i: (b, i, 2)),
            pl.BlockSpec((1, 1, 256, c), lambda b, i: (b, i // per, 0, i % per)),
            pl.BlockSpec((1, c, 256), lambda b, i: (b, i, 3)),
            pl.BlockSpec((1, c, 256), lambda b, i: (b, i, 0)),
            _const_spec(intra.shape), _const_spec(qdec.shape), _const_spec(kdect.shape),
            _const_spec(cd.shape), _const_spec(norm_w.shape),
        ],
        out_specs=pl.BlockSpec((1, c, 256), lambda b, i: (b, i, 0)),
        out_shape=jax.ShapeDtypeStruct((bn, s, 256), _BF16),
        scratch_shapes=[pltpu.VMEM((RET_HEADS * RET_QK, RET_HEADS * RET_V), _F32)],
        compiler_params=pltpu.CompilerParams(dimension_semantics=("parallel", "arbitrary")),
        name="retention",
    )(n16, t32, n16, n32, intra, qdec, kdect, cd, norm_w)


def _gla_kernel(q_ref, k_ref, a_ref, v_ref, vt_ref, g_ref, wa_ref, ba_ref, nw_ref, o_ref,
                st_ref, u_ref, oacc_ref, qs_ref, kk_ref, b_ref, qh_ref, dec_ref, *, ts, sub):
    nsub = ts // sub
    assert sub == 16

    @pl.when(pl.program_id(1) == 0)
    def _():
        st_ref[...] = jnp.zeros(st_ref.shape, _F32)

    la = jax.nn.log_sigmoid(_dot(a_ref[0].astype(_BF16), wa_ref[...]) + ba_ref[...]) * (1.0 / GLA_GATE_TEMP)
    in_blk = _iota(la.shape, 0) % sub
    b = la
    for sh in (1, 2, 4, 8):
        b = b + jnp.where(in_blk >= sh, pltpu.roll(b, sh, 0), 0.0)
    bl = jnp.where(in_blk == sub - 1, b, 0.0)
    for sh in (1, 2, 4, 8):
        bl = bl + jnp.where(in_blk < sub - sh, pltpu.roll(bl, ts - sh, 0), 0.0)
    qs = q_ref[0] * (GLA_QK ** -0.5)
    kk = k_ref[0]
    kd = (kk * jnp.exp(bl - b)).astype(_BF16)
    qs_ref[...] = qs
    kk_ref[...] = kk
    b_ref[...] = b
    qh_ref[...] = (qs * jnp.exp(b)).astype(_BF16)
    dec_ref[...] = jnp.exp(bl)
    vt = vt_ref[0, 0]
    row_blk = _iota(kd.shape, 0) // sub
    kd_wide = jnp.concatenate([jnp.where(row_blk == n, kd, jnp.zeros_like(kd)) for n in range(nsub)], axis=1)
    u_all = _dot(vt, kd_wide)
    for n in range(nsub):
        u_ref[n] = u_all[:, n * 128:(n + 1) * 128]

    st_keep = (_iota(st_ref.shape, 0) // GLA_V) == (_iota(st_ref.shape, 1) // GLA_QK)
    spread = jnp.where((_iota((128, 256), 0) // GLA_QK) == (_iota((128, 256), 1) // GLA_V), 1.0, 0.0).astype(_BF16)
    row16 = _iota((sub, 128), 0)

    def body(n, carry):
        r0 = pl.multiple_of(n * sub, sub)
        st = st_ref[...]
        o_cross = _dot_nt(qh_ref[pl.ds(r0, sub), :], st.astype(_BF16))
        q16 = qs_ref[pl.ds(r0, sub), :]
        k16 = kk_ref[pl.ds(r0, sub), :]
        b16 = b_ref[pl.ds(r0, sub), :]
        v16 = v_ref[0, pl.ds(r0, sub), :].astype(_F32)
        es = []
        for j in range(sub):
            e = q16 * k16[j:j + 1] * jnp.exp(jnp.minimum(b16 - b16[j:j + 1], 0.0))
            es.append(jnp.where(row16 >= j, e, 0.0))
        e_all = jnp.concatenate(es, axis=0)
        e_hi = e_all.astype(_BF16)
        e_lo = (e_all - e_hi.astype(_F32)).astype(_BF16)
        att = _dot(e_hi, spread) + _dot(e_lo, spread)
        o_diag = att[0:sub] * v16[0:1]
        for j in range(1, sub):
            o_diag = o_diag + att[j * sub:(j + 1) * sub] * v16[j:j + 1]
        oacc_ref[pl.ds(r0, sub), :] = o_cross + o_diag
        st_ref[...] = st * dec_ref[pl.ds(r0, 1), :] + jnp.where(st_keep, u_ref[n], 0.0)
        return carry

    lax.fori_loop(0, nsub, body, 0, unroll=4)

    o = oacc_ref[...]
    ms = _group_mean(o * o, GLA_V)
    y = o * lax.rsqrt(ms + LN_EPS) * nw_ref[...]
    gate = g_ref[0]
    o_ref[0] = (gate * jax.nn.sigmoid(gate) * y).astype(_BF16)


def _gla(wa, ba, norm_w, n16, n32, t16, bn, s):
    ts, sub = CH, GLA_SUB
    return pl.pallas_call(
        functools.partial(_gla_kernel, ts=ts, sub=sub),
        grid=(bn, s // ts),
        in_specs=[
            pl.BlockSpec((1, ts, 128), lambda b, i: (b, i, 4)),
            pl.BlockSpec((1, ts, 128), lambda b, i: (b, i, 5)),
            pl.BlockSpec((1, ts, 128), lambda b, i: (b, i, 6)),
            pl.BlockSpec((1, ts, 256), lambda b, i: (b, i, 4)),
            pl.BlockSpec((1, 1, 256, ts), lambda b, i: (b, i, 4, 0)),
            pl.BlockSpec((1, ts, 256), lambda b, i: (b, i, 1)),
            _const_spec(wa.shape), _const_spec(ba.shape), _const_spec(norm_w.shape),
        ],
        out_specs=pl.BlockSpec((1, ts, 256), lambda b, i: (b, i, 0)),
        out_shape=jax.ShapeDtypeStruct((bn, s, 256), _BF16),
        scratch_shapes=[
            pltpu.VMEM((GLA_HEADS * GLA_V, GLA_HEADS * GLA_QK), _F32),
            pltpu.VMEM((ts // sub, GLA_HEADS * GLA_V, GLA_HEADS * GLA_QK), _F32),
            pltpu.VMEM((ts, 256), _F32),
            pltpu.VMEM((ts, 128), _F32), pltpu.VMEM((ts, 128), _F32), pltpu.VMEM((ts, 128), _F32),
            pltpu.VMEM((ts, 128), _BF16), pltpu.VMEM((ts, 128), _F32),
        ],
        compiler_params=pltpu.CompilerParams(dimension_semantics=("parallel", "arbitrary")),
        name="gla",
    )(n32, n32, n32, n16, t16, n32, wa, ba, norm_w)


def _merge_kernel(x_ref, ya_ref, yb_ref, yc_ref, yd_ref, wg_ref, wbr_ref, wout_ref, lnw_ref, lnb_ref, h_ref):
    x = x_ref[...]
    xb = x.astype(_BF16)
    merged = None
    for n, y_ref in enumerate((ya_ref, yb_ref, yc_ref, yd_ref)):
        gate = jax.nn.sigmoid(_dot(xb, wg_ref[:, n * D_MODEL:(n + 1) * D_MODEL]))
        term = gate * _dot(y_ref[...], wbr_ref[n])
        merged = term if merged is None else merged + term
    mix = _dot(merged.astype(_BF16), wout_ref[...])
    h_ref[...] = _layer_norm(DEEPNORM_ALPHA * x + mix, lnw_ref[...], lnb_ref[...])


def _merge(x2, ys, wg, wbr, wout, lnw, lnb):
    t = x2.shape[0]
    ts = TS_DENSE
    tok = lambda w: pl.BlockSpec((ts, w), lambda i: (i, 0))
    return pl.pallas_call(
        _merge_kernel,
        grid=(t // ts,),
        in_specs=[tok(D_MODEL)] + [tok(BRANCH_W)] * 4 + [
            _const_spec(wg.shape), _const_spec(wbr.shape), _const_spec(wout.shape),
            _const_spec(lnw.shape), _const_spec(lnb.shape)],
        out_specs=tok(D_MODEL),
        out_shape=jax.ShapeDtypeStruct((t, D_MODEL), _F32),
        compiler_params=pltpu.CompilerParams(
            dimension_semantics=("parallel",), vmem_limit_bytes=V7X_VMEM_LIMIT),
        name="merge_ln",
    )(x2, *ys, wg, wbr, wout, lnw, lnb)


def _ffn_kernel(h_ref, p_ref, wup_ref, cw_ref, cb_ref, wdn_ref, wpg_ref, wpp_ref, lnw_ref, lnb_ref,
                o_ref, tail_ref, *, ts):
    @pl.when(pl.program_id(1) == 0)
    def _():
        tail_ref[...] = jnp.zeros(tail_ref.shape, _F32)

    h = h_ref[...]
    hb = h.astype(_BF16)
    row = _iota((ts, FF_COLS), 0)
    f = None
    for c0 in range(0, D_FF, FF_COLS):
        cols = slice(c0, c0 + FF_COLS)
        u = _dot(hb, wup_ref[:, cols])
        gt = _dot(hb, wup_ref[:, D_FF + c0:D_FF + c0 + FF_COLS])
        prev = tail_ref[:, cols]
        g1 = jnp.where(row == 0, prev[7:8], pltpu.roll(gt, 1, 0))
        g2 = jnp.where(row == 0, prev[6:7], jnp.where(row == 1, prev[7:8], pltpu.roll(gt, 2, 0)))
        tail_ref[:, cols] = gt[ts - 8:ts]
        gc = cb_ref[:, cols] + cw_ref[0:1, cols] * g2
        gc = gc + cw_ref[1:2, cols] * g1
        gc = gc + cw_ref[2:3, cols] * gt
        term = _dot((jax.nn.gelu(gc) * u).astype(_BF16), wdn_ref[cols, :])
        f = term if f is None else f + term
    e = jax.nn.sigmoid(_dot(hb, wpg_ref[...])) * _dot(p_ref[...].astype(_BF16), wpp_ref[...])
    o_ref[...] = _layer_norm(DEEPNORM_ALPHA * h + f + e, lnw_ref[...], lnb_ref[...])


def _ffn(h2, p2, wup, cw, cb, wdn, wpg, wpp, lnw, lnb, bn, s):
    ts = TS_DENSE
    ns = s // ts
    tok = lambda w: pl.BlockSpec((ts, w), lambda b, i: (b * ns + i, 0))
    return pl.pallas_call(
        functools.partial(_ffn_kernel, ts=ts),
        grid=(bn, ns),
        in_specs=[tok(D_MODEL), tok(P_DIM)] + [_const_spec(a.shape) for a in (wup, cw, cb, wdn, wpg, wpp, lnw, lnb)],
        out_specs=tok(D_MODEL),
        out_shape=jax.ShapeDtypeStruct((bn * s, D_MODEL), _F32),
        scratch_shapes=[pltpu.VMEM((8, D_FF), _F32)],
        compiler_params=pltpu.CompilerParams(
            dimension_semantics=("parallel", "arbitrary"), vmem_limit_bytes=V7X_VMEM_LIMIT),
        name="ffn_ple_ln",
    )(h2, p2, wup, cw, cb, wdn, wpg, wpp, lnw, lnb)


def kernel(x, p, w_in, a_lambda, a_norm_w, ret_norm_w, gla_w_a2, gla_b_a, gla_norm_w, w_branch, w_out,
           ln1_w, ln1_b, w_ffn_up, ffn_conv_w, ffn_conv_b, w_ffn_down, w_ple_gate, w_ple_proj, ln2_w, ln2_b):
    bn, s, _ = x.shape
    t = bn * s
    slopes = [2.0 ** (-(8.0 / N_SOFTMAX_HEADS) * i) for i in range(1, N_SOFTMAX_HEADS + 1)]
    slopes_a, slopes_b = tuple(slopes[0::2]), tuple(slopes[1::2])
    row = lambda v: v.astype(_F32).reshape(1, -1)
    x2 = x.reshape(t, D_MODEL)
    for i in range(DEPTH):
        *proj_w, wg = _prep_weights(w_in, i)
        n32, n16, t16, t32 = _project(x2, *proj_w, bn, s)
        n32 = n32.reshape(bn, s, -1)
        n16 = n16.reshape(bn, s, -1)
        lam_init = 0.8 - 0.6 * math.exp(-0.3 * i)
        y_a = _diff_attention(a_lambda[i].astype(_F32), a_norm_w[i].astype(_F32).reshape(DA_V, 1),
                              n16, t16, bn, s, slopes_a, lam_init)
        y_b = _dsa_attention(n16, t16, t32, bn, s, slopes_b)
        y_c = _retention(row(ret_norm_w[i]), n16, n32, t32, bn, s)
        wa = jnp.pad(gla_w_a2[i], ((0, 128 - GLA_RANK), (0, 0))).astype(_BF16)
        y_d = _gla(wa, row(gla_b_a[i]), row(jnp.tile(gla_norm_w[i], GLA_HEADS)), n16, n32, t16, bn, s)
        ys = [y.reshape(t, BRANCH_W) for y in (y_a, y_b, y_c, y_d)]
        h2 = _merge(x2, ys, wg, w_branch[i].astype(_BF16), w_out[i].astype(_BF16), row(ln1_w[i]), row(ln1_b[i]))
        x2 = _ffn(h2, p[i].reshape(t, P_DIM), w_ffn_up[i].astype(_BF16), ffn_conv_w[i].astype(_F32),
                  row(ffn_conv_b[i]), w_ffn_down[i].astype(_BF16), w_ple_gate[i].astype(_BF16),
                  w_ple_proj[i].astype(_BF16), row(ln2_w[i]), row(ln2_b[i]), bn, s)
    return x2.reshape(bn, s, D_MODEL)
```
